```python
import math
import jax, jax.numpy as jnp
from jax import lax
import numpy as np

D_MODEL = 2048
BATCH = 16
SEQ = 256
DEPTH = 2
DEC_BATCH = 4
DEC_SEQ = 1024
PAST_LEN = 512

GRID_W = 64
POOL_WIDTH = 512
POOL_GROUPS = 4
POOL_CH = POOL_WIDTH // POOL_GROUPS
POOL_WINDOWS = (2, 4, 8, 16)
S5_WIDTH = 512
S5_CH = 16
S5_GROUPS = S5_WIDTH // S5_CH
S5_STATE = 64
N_HEADS = 8
N_KV_HEADS = 2
HEAD_DIM = 128
REP = N_HEADS // N_KV_HEADS
ATTN_WIDTH = N_HEADS * HEAD_DIM
KV_WIDTH = N_KV_HEADS * HEAD_DIM
WINDOW = 128
BLOCK = 128
ROPE_BASE = 10000.0
NEG_INF = -1e30
CONV_WIDTH = 512
CONV_K = 31
N_BRANCH = 4
D_FF = 5632
FFN_CONV_K = 3
ALPHA = (2.0 * DEPTH) ** 0.25
BETA = (8.0 * DEPTH) ** -0.25
LN_EPS = 1e-5
MIX_COLS = POOL_WIDTH + S5_WIDTH + ATTN_WIDTH + 2 * KV_WIDTH + 2 * CONV_WIDTH
N_IN = MIX_COLS + N_BRANCH * D_MODEL
IN_SPLITS = (
    POOL_WIDTH,
    POOL_WIDTH + S5_WIDTH,
    POOL_WIDTH + S5_WIDTH + ATTN_WIDTH,
    POOL_WIDTH + S5_WIDTH + ATTN_WIDTH + KV_WIDTH,
    POOL_WIDTH + S5_WIDTH + ATTN_WIDTH + 2 * KV_WIDTH,
    MIX_COLS,
)

kernel_name = "hybrid_flow_prefix_trunk_step"


def layer_norm(x, g=None, b=None):
    xf = x.astype(jnp.float32)
    mu = jnp.mean(xf, axis=-1, keepdims=True)
    var = jnp.mean(jnp.square(xf - mu), axis=-1, keepdims=True)
    y = (xf - mu) * lax.rsqrt(var + LN_EPS)
    if g is not None:
        y = y * g.astype(jnp.float32) + b.astype(jnp.float32)
    return y.astype(x.dtype)


def depthwise_conv(x, w, bias):
    k = w.shape[0]
    y = lax.conv_general_dilated(
        x, w[:, None, :].astype(x.dtype), window_strides=(1,),
        padding=[(k // 2, k // 2)], dimension_numbers=("NWC", "WIO", "NWC"),
        feature_group_count=x.shape[-1])
    return y + bias.astype(x.dtype)


def rope_2d(x):
    n = x.shape[1]
    t = jnp.arange(n)
    row, col = t // GRID_W, t % GRID_W
    half = HEAD_DIM // 2
    quarter = half // 2
    inv = ROPE_BASE ** (-jnp.arange(quarter, dtype=jnp.float32) / quarter)

    def rot(xp, pos):
        ang = pos.astype(jnp.float32)[:, None] * inv[None, :]
        cos = jnp.cos(ang)[None, :, None, :]
        sin = jnp.sin(ang)[None, :, None, :]
        x1, x2 = xp[..., :quarter], xp[..., quarter:]
        return jnp.concatenate([x1 * cos - x2 * sin, x2 * cos + x1 * sin], axis=-1)

    xf = x.astype(jnp.float32)
    return jnp.concatenate([rot(xf[..., :half], row), rot(xf[..., half:], col)], axis=-1).astype(x.dtype)


def pool_mixer(a, w_grp, scale):
    bsz, n, _ = a.shape
    ag = a.reshape(bsz, n, POOL_GROUPS, POOL_CH).astype(jnp.float32)
    cs = jnp.concatenate([jnp.zeros((bsz, 1, POOL_GROUPS, POOL_CH), jnp.float32),
                          jnp.cumsum(ag, axis=1)], axis=1)
    t = jnp.arange(n)
    outs = []
    for g, w in enumerate(POOL_WINDOWS):
        left = w // 2
        right = w - 1 - left
        lo = jnp.clip(t - left, 0, n)
        hi = jnp.clip(t + right + 1, 0, n)
        mean = (cs[:, hi, g] - cs[:, lo, g]) / (hi - lo).astype(jnp.float32)[None, :, None]
        outs.append(mean - ag[:, :, g])
    pooled = jnp.stack(outs, axis=2)
    mixed = jnp.einsum("bngc,gcd->bngd", pooled, w_grp.astype(jnp.float32))
    return (mixed.reshape(bsz, n, POOL_WIDTH) * scale.astype(jnp.float32)).astype(a.dtype)


def _lin_comb(left, right):
    a_l, b_l = left
    a_r, b_r = right
    return a_r * a_l, a_r * b_l + b_r


def s5_mixer(u, lam_re, lam_im, log_dt, b_re, b_im, c_re, c_im, d, w_glu, h0):
    bsz, n, _ = u.shape
    f32 = jnp.float32
    ug = u.reshape(bsz, n, S5_GROUPS, S5_CH).astype(f32)
    uc = ug.astype(jnp.complex64)
    lam = lax.complex(lam_re.astype(f32), lam_im.astype(f32))
    lam_dt = lam * jnp.exp(log_dt.astype(f32))[..., None]
    lam_bar = jnp.exp(lam_dt)
    b_mat = lax.complex(b_re.astype(f32), b_im.astype(f32))
    b_bar = ((lam_bar - 1.0) / lam)[..., None] * b_mat
    c_mat = lax.complex(c_re.astype(f32), c_im.astype(f32))
    pos = (jnp.arange(n) + 1).astype(f32)
    powers = jnp.exp(lam_dt[:, None] * pos[None, :, None, None])
    ys, finals = [], []
    for dirn in range(2):
        bu = jnp.einsum("gpc,bngc->bngp", b_bar[dirn], uc)
        if dirn == 1:
            bu = jnp.flip(bu, axis=1)
        a_el = jnp.broadcast_to(lam_bar[dirn], bu.shape)
        _, hs = lax.associative_scan(_lin_comb, (a_el, bu), axis=1)
        hs = hs + powers[dirn][None] * h0[:, dirn][:, None]
        finals.append(hs[:, -1])
        if dirn == 1:
            hs = jnp.flip(hs, axis=1)
        ys.append(jnp.real(jnp.einsum("gcp,bngp->bngc", c_mat[dirn], hs)))
    y = ys[0] + ys[1] + d.astype(f32).reshape(S5_GROUPS, S5_CH) * ug
    y = jax.nn.gelu(y.reshape(bsz, n, S5_WIDTH))
    y = y * jax.nn.sigmoid(y @ w_glu.astype(f32))
    return y.astype(u.dtype), jnp.stack(finals, axis=1)


def _attend_block(qi, keys, vals, mask, sink):
    s = jnp.einsum("bqkrd,bskd->bkrqs", qi, keys, preferred_element_type=jnp.float32) * (HEAD_DIM ** -0.5)
    s = jnp.where(mask, s, NEG_INF)
    sink_col = jnp.broadcast_to(sink.astype(jnp.float32)[None, :, :, None, None], s.shape[:-1] + (1,))
    p = jax.nn.softmax(jnp.concatenate([s, sink_col], axis=-1), axis=-1)[..., :-1]
    o = jnp.einsum("bkrqs,bskd->bqkrd", p.astype(vals.dtype), vals)
    return o.reshape(o.shape[0], BLOCK, ATTN_WIDTH)


def context_attention(q, k, v, sink):
    b, n = q.shape[:2]
    nblk = n // BLOCK
    qb = q.reshape(b, nblk, BLOCK, N_KV_HEADS, REP, HEAD_DIM).swapaxes(0, 1)
    mask = jnp.ones((BLOCK, k.shape[1]), bool)
    sk = sink.reshape(N_KV_HEADS, REP)
    out = lax.map(lambda qi: _attend_block(qi, k, v, mask, sk), qb)
    return out.swapaxes(0, 1).reshape(b, n, ATTN_WIDTH)


def latent_attention(q, k, v, k_ctx, v_ctx, sink):
    b, n = q.shape[:2]
    nblk = n // BLOCK
    n_ctx = k_ctx.shape[1]
    qb = q.reshape(b, nblk, BLOCK, N_KV_HEADS, REP, HEAD_DIM).swapaxes(0, 1)
    pad = ((0, 0), (BLOCK, BLOCK), (0, 0), (0, 0))
    kp = jnp.pad(k, pad)
    vp = jnp.pad(v, pad)
    sk = sink.reshape(N_KV_HEADS, REP)
    q_off = jnp.arange(BLOCK)
    k_off = jnp.arange(3 * BLOCK)
    ctx_mask = jnp.ones((BLOCK, n_ctx), bool)

    def one_block(args):
        i, qi = args
        start = i * BLOCK
        kw = lax.dynamic_slice_in_dim(kp, start, 3 * BLOCK, axis=1)
        vw = lax.dynamic_slice_in_dim(vp, start, 3 * BLOCK, axis=1)
        qpos = start + q_off
        kpos = start - BLOCK + k_off
        lat_mask = (kpos[None, :] >= 0) & (kpos[None, :] < n) & (jnp.abs(qpos[:, None] - kpos[None, :]) <= WINDOW)
        mask = jnp.concatenate([lat_mask, ctx_mask], axis=1)
        keys = jnp.concatenate([kw, k_ctx.astype(kw.dtype)], axis=1)
        vals = jnp.concatenate([vw, v_ctx.astype(vw.dtype)], axis=1)
        return _attend_block(qi, keys, vals, mask, sk)

    out = lax.map(one_block, (jnp.arange(nblk), qb))
    return out.swapaxes(0, 1).reshape(b, n, ATTN_WIDTH)


def conformer_conv(cv, dw, db, ln_g, ln_b):
    a, gt = jnp.split(cv, 2, axis=-1)
    x = a * jax.nn.sigmoid(gt)
    x = depthwise_conv(x, dw, db)
    return jax.nn.silu(layer_norm(x, ln_g, ln_b))


def conv_ffn(h, w_up, dw, db, w_down):
    u = depthwise_conv(h @ w_up, dw, db)
    gt, val = jnp.split(u, 2, axis=-1)
    return (jax.nn.silu(gt) * val) @ w_down


def trunk_layer(x, mod, p, ctx):
    shift1, scale1, gate1, shift2, scale2, gate2 = jnp.split(mod, 6, axis=-1)
    b, n = x.shape[:2]
    h = layer_norm(x) * (1.0 + scale1) + shift1
    proj = h @ p["w_in"]
    a, s, q, k, v, cv, gates = jnp.split(proj, IN_SPLITS, axis=-1)
    q = q.reshape(b, n, N_HEADS, HEAD_DIM)
    k = k.reshape(b, n, N_KV_HEADS, HEAD_DIM)
    v = v.reshape(b, n, N_KV_HEADS, HEAD_DIM)
    if ctx is None:
        h0 = jnp.zeros((b, 2, S5_GROUPS, S5_STATE), jnp.complex64)
        attn = context_attention(q, k, v, p["attn_sink"])
    else:
        k_ctx, v_ctx, h0 = ctx
        attn = latent_attention(rope_2d(q), rope_2d(k), v, k_ctx, v_ctx, p["attn_sink"])
    y_pool = pool_mixer(a, p["pool_w"], p["pool_scale"])
    y_s5, s5_final = s5_mixer(s, p["s5_lambda_re"], p["s5_lambda_im"], p["s5_log_dt"],
                              p["s5_b_re"], p["s5_b_im"], p["s5_c_re"], p["s5_c_im"],
                              p["s5_d"], p["s5_w_glu"], h0)
    y_conv = conformer_conv(cv, p["conv_dw"], p["conv_db"], p["conv_ln_g"], p["conv_ln_b"])
    g = jax.nn.sigmoid(gates.reshape(b, n, N_BRANCH, D_MODEL))
    merged = (g[:, :, 0] * (y_pool @ p["w_br_pool"]) + g[:, :, 1] * (y_s5 @ p["w_br_s5"])
              + g[:, :, 2] * (attn @ p["w_br_attn"]) + g[:, :, 3] * (y_conv @ p["w_br_conv"]))
    x = layer_norm(ALPHA * x + gate1 * (merged @ p["w_out"]), p["ln1_g"], p["ln1_b"])
    h = layer_norm(x) * (1.0 + scale2) + shift2
    f = conv_ffn(h, p["ffn_w_up"], p["ffn_dw"], p["ffn_db"], p["ffn_w_down"])
    x = layer_norm(ALPHA * x + gate2 * f, p["ln2_g"], p["ln2_b"])
    ctx_out = (k, v, s5_final) if ctx is None else None
    return x, ctx_out


def setup_inputs(seed: int = 0) -> dict:
    key = jax.random.key(seed)
    ks = jax.random.split(key, 48)
    f32 = jnp.float32

    def nrm(k, shape, scale):
        return scale * jax.random.normal(k, shape, f32)

    L = DEPTH
    s5_n = jnp.arange(S5_STATE, dtype=f32)
    return {
        "x_prompt": nrm(ks[0], (BATCH, SEQ, D_MODEL), 1.0),
        "x_sample": nrm(ks[1], (DEC_BATCH, DEC_SEQ, D_MODEL), 1.0),
        "cache_k": nrm(ks[2], (DEC_BATCH, DEPTH, PAST_LEN, N_KV_HEADS, HEAD_DIM), 1.0),
        "cache_v": nrm(ks[3], (DEC_BATCH, DEPTH, PAST_LEN, N_KV_HEADS, HEAD_DIM), 1.0),
        "state_s5": nrm(ks[4], (DEC_BATCH, DEPTH, 2, 2, S5_GROUPS, S5_STATE), 0.5),
        "c": nrm(ks[5], (DEC_BATCH, D_MODEL), 1.0),
        "c_ctx": nrm(ks[6], (D_MODEL,), 1.0),
        "w_ada": nrm(ks[7], (L, D_MODEL, 6 * D_MODEL), 0.5 * D_MODEL ** -0.5),
        "b_ada": nrm(ks[8], (L, 6 * D_MODEL), 0.02),
        "w_in": nrm(ks[9], (L, D_MODEL, N_IN), D_MODEL ** -0.5),
        "pool_w": nrm(ks[10], (L, POOL_GROUPS, POOL_CH, POOL_CH), POOL_CH ** -0.5),
        "pool_scale": 1.0 + nrm(ks[11], (L, POOL_WIDTH), 0.02),
        "s5_lambda_re": -0.5 + nrm(ks[12], (L, 2, S5_GROUPS, S5_STATE), 0.01),
        "s5_lambda_im": jnp.pi * s5_n + nrm(ks[13], (L, 2, S5_GROUPS, S5_STATE), 0.01),
        "s5_log_dt": jax.random.uniform(ks[14], (L, 2, S5_GROUPS), f32, math.log(1e-3), math.log(1e-1)),
        "s5_b_re": nrm(ks[15], (L, 2, S5_GROUPS, S5_STATE, S5_CH), (2 * S5_CH) ** -0.5),
        "s5_b_im": nrm(ks[16], (L, 2, S5_GROUPS, S5_STATE, S5_CH), (2 * S5_CH) ** -0.5),
        "s5_c_re": nrm(ks[17], (L, 2, S5_GROUPS, S5_CH, S5_STATE), (2 * S5_STATE) ** -0.5),
        "s5_c_im": nrm(ks[18], (L, 2, S5_GROUPS, S5_CH, S5_STATE), (2 * S5_STATE) ** -0.5),
        "s5_d": nrm(ks[19], (L, S5_WIDTH), 0.5),
        "s5_w_glu": nrm(ks[20], (L, S5_WIDTH, S5_WIDTH), S5_WIDTH ** -0.5),
        "attn_sink": nrm(ks[21], (L, N_HEADS), 0.5),
        "conv_dw": nrm(ks[22], (L, CONV_K, CONV_WIDTH), CONV_K ** -0.5),
        "conv_db": nrm(ks[23], (L, CONV_WIDTH), 0.02),
        "conv_ln_g": 1.0 + nrm(ks[24], (L, CONV_WIDTH), 0.02),
        "conv_ln_b": nrm(ks[25], (L, CONV_WIDTH), 0.02),
        "w_br_pool": nrm(ks[26], (L, POOL_WIDTH, D_MODEL), BETA * POOL_WIDTH ** -0.5),
        "w_br_s5": nrm(ks[27], (L, S5_WIDTH, D_MODEL), BETA * S5_WIDTH ** -0.5),
        "w_br_attn": nrm(ks[28], (L, ATTN_WIDTH, D_MODEL), BETA * ATTN_WIDTH ** -0.5),
        "w_br_conv": nrm(ks[29], (L, CONV_WIDTH, D_MODEL), BETA * CONV_WIDTH ** -0.5),
        "w_out": nrm(ks[30], (L, D_MODEL, D_MODEL), BETA * D_MODEL ** -0.5),
        "ln1_g": 1.0 + nrm(ks[31], (L, D_MODEL), 0.02),
        "ln1_b": nrm(ks[32], (L, D_MODEL), 0.02),
        "ffn_w_up": nrm(ks[33], (L, D_MODEL, 2 * D_FF), D_MODEL ** -0.5),
        "ffn_dw": nrm(ks[34], (L, FFN_CONV_K, 2 * D_FF), FFN_CONV_K ** -0.5),
        "ffn_db": nrm(ks[35], (L, 2 * D_FF), 0.02),
        "ffn_w_down": nrm(ks[36], (L, D_FF, D_MODEL), BETA * D_FF ** -0.5),
        "ln2_g": 1.0 + nrm(ks[37], (L, D_MODEL), 0.02),
        "ln2_b": nrm(ks[38], (L, D_MODEL), 0.02),
    }


def reference(x_prompt, x_sample, cache_k, cache_v, state_s5, c, c_ctx, w_ada, b_ada, w_in,
              pool_w, pool_scale, s5_lambda_re, s5_lambda_im, s5_log_dt, s5_b_re, s5_b_im,
              s5_c_re, s5_c_im, s5_d, s5_w_glu, attn_sink, conv_dw, conv_db, conv_ln_g, conv_ln_b,
              w_br_pool, w_br_s5, w_br_attn, w_br_conv, w_out, ln1_g, ln1_b, ffn_w_up, ffn_dw,
              ffn_db, ffn_w_down, ln2_g, ln2_b):
    y_prompt = x_prompt
    y_sample = x_sample
    ks_out, vs_out, ss_out = [], [], []
    for l in range(DEPTH):
        p = {
            "w_in": w_in[l], "pool_w": pool_w[l], "pool_scale": pool_scale[l],
            "s5_lambda_re": s5_lambda_re[l], "s5_lambda_im": s5_lambda_im[l], "s5_log_dt": s5_log_dt[l],
            "s5_b_re": s5_b_re[l], "s5_b_im": s5_b_im[l], "s5_c_re": s5_c_re[l], "s5_c_im": s5_c_im[l],
            "s5_d": s5_d[l], "s5_w_glu": s5_w_glu[l], "attn_sink": attn_sink[l],
            "conv_dw": conv_dw[l], "conv_db": conv_db[l], "conv_ln_g": conv_ln_g[l], "conv_ln_b": conv_ln_b[l],
            "w_br_pool": w_br_pool[l], "w_br_s5": w_br_s5[l], "w_br_attn": w_br_attn[l], "w_br_conv": w_br_conv[l],
            "w_out": w_out[l], "ln1_g": ln1_g[l], "ln1_b": ln1_b[l],
            "ffn_w_up": ffn_w_up[l], "ffn_dw": ffn_dw[l], "ffn_db": ffn_db[l], "ffn_w_down": ffn_w_down[l],
            "ln2_g": ln2_g[l], "ln2_b": ln2_b[l],
        }
        mod_ctx = jax.nn.silu(c_ctx) @ w_ada[l] + b_ada[l]
        mod_lat = (jax.nn.silu(c) @ w_ada[l] + b_ada[l])[:, None, :]
        y_prompt, (k_new, v_new, s5_new) = trunk_layer(y_prompt, mod_ctx, p, None)
        ks_out.append(k_new)
        vs_out.append(v_new)
        ss_out.append(jnp.stack([jnp.real(s5_new), jnp.imag(s5_new)], axis=2))
        st = state_s5[:, l]
        h0 = lax.complex(st[:, :, 0].astype(jnp.float32), st[:, :, 1].astype(jnp.float32))
        y_sample, _ = trunk_layer(y_sample, mod_lat, p, (cache_k[:, l], cache_v[:, l], h0))
    new_cache_k = jnp.stack(ks_out, axis=1)
    new_cache_v = jnp.stack(vs_out, axis=1)
    new_state_s5 = jnp.stack(ss_out, axis=1)
    return (y_prompt, y_sample, new_cache_k, new_cache_v, new_state_s5)
```

```python
import functools
import math

import jax
import jax.numpy as jnp
from jax import lax
from jax.experimental import pallas as pl
from jax.experimental.pallas import tpu as pltpu

F32 = jnp.float32
BF16 = jnp.bfloat16

D_MODEL = 2048
BATCH = 16
SEQ = 256
DEPTH = 2
DEC_BATCH = 4
DEC_SEQ = 1024
PAST_LEN = 512
GRID_W = 64
POOL_WIDTH = 512
POOL_GROUPS = 4
POOL_CH = POOL_WIDTH // POOL_GROUPS
POOL_WINDOWS = (2, 4, 8, 16)
S5_WIDTH = 512
S5_CH = 16
S5_GROUPS = S5_WIDTH // S5_CH
S5_STATE = 64
N_HEADS = 8
N_KV_HEADS = 2
HEAD_DIM = 128
REP = N_HEADS // N_KV_HEADS
ATTN_WIDTH = N_HEADS * HEAD_DIM
KV_WIDTH = N_KV_HEADS * HEAD_DIM
WINDOW = 128
ROPE_BASE = 10000.0
NEG_INF = -1e30
CONV_WIDTH = 512
CONV_K = 31
N_BRANCH = 4
D_FF = 5632
FFN_CONV_K = 3
ALPHA = (2.0 * DEPTH) ** 0.25
LN_EPS = 1e-5
MIX_COLS = POOL_WIDTH + S5_WIDTH + ATTN_WIDTH + 2 * KV_WIDTH + 2 * CONV_WIDTH
N_IN = MIX_COLS + N_BRANCH * D_MODEL

COL_POOL = 0
COL_S5 = POOL_WIDTH
COL_Q = COL_S5 + S5_WIDTH
COL_K = COL_Q + ATTN_WIDTH
COL_V = COL_K + KV_WIDTH
COL_CV = COL_V + KV_WIDTH
COL_GATE = MIX_COLS

T_CTX = BATCH * SEQ
T_LAT = DEC_BATCH * DEC_SEQ
T_ALL = T_CTX + T_LAT
ROW_BLK = 1024
N_ROW_BLK = T_ALL // ROW_BLK
N_CTX_BLK = T_CTX // ROW_BLK
MOD_ROWS = 8

S5_Q = 4
S5_GQ = S5_GROUPS // S5_Q
S5_QU = S5_GQ * S5_CH
S5_QS = S5_GQ * S5_STATE
S5_LANES = 2 * S5_GROUPS * S5_STATE
S5_LAT_B = 8

VMEM_LIMIT = 56 * 1024 * 1024


def _cp(sem):
    return pltpu.CompilerParams(dimension_semantics=sem, vmem_limit_bytes=VMEM_LIMIT)


def _mod_row(row_block, rows_per_block):
    lat_batch = (row_block * rows_per_block) // DEC_SEQ - T_CTX // DEC_SEQ
    return jnp.maximum(lat_batch + 1, 0)


def _seq_pos(row_block, rows):
    r = lax.broadcasted_iota(jnp.int32, (rows, 1), 0)
    seq_len = jnp.where(row_block < N_CTX_BLK, SEQ, DEC_SEQ)
    return r % seq_len, seq_len


def _layer_norm(x):
    mu = jnp.mean(x, axis=-1, keepdims=True)
    xc = x - mu
    var = jnp.mean(xc * xc, axis=-1, keepdims=True)
    return xc * lax.rsqrt(var + LN_EPS)


def _shift_rows(x, d):
    n = x.shape[0]
    return pltpu.roll(x, (-d) % n, axis=0)


def _mod_kernel(c_ref, w_ref, b_ref, o_ref):
    c = c_ref[...]
    a = c * jax.nn.sigmoid(c)
    o_ref[0] = jnp.dot(a, w_ref[0], preferred_element_type=F32,
                       precision=lax.Precision.HIGHEST) + b_ref[0]


def _mod_table(cvec, w_ada, b_ada):
    tn = 1024
    n6 = 6 * D_MODEL
    return pl.pallas_call(
        _mod_kernel,
        grid=(DEPTH, n6 // tn),
        in_specs=[
            pl.BlockSpec((MOD_ROWS, D_MODEL), lambda l, j: (0, 0)),
            pl.BlockSpec((1, D_MODEL, tn), lambda l, j: (l, 0, j)),
            pl.BlockSpec((1, 1, tn), lambda l, j: (l, 0, j)),
        ],
        out_specs=pl.BlockSpec((1, MOD_ROWS, tn), lambda l, j: (l, 0, j)),
        out_shape=jax.ShapeDtypeStruct((DEPTH, MOD_ROWS, n6), F32),
        compiler_params=_cp(("arbitrary", "arbitrary")),
        name="mod_table",
    )(cvec, w_ada, b_ada.reshape(DEPTH, 1, n6))


def _ln_mod_kernel(x_ref, mod_ref, h_ref, *, shift_i, scale_i):
    y = _layer_norm(x_ref[...])
    scale = mod_ref[0, scale_i:scale_i + 1, :]
    shift = mod_ref[0, shift_i:shift_i + 1, :]
    h_ref[...] = (y * (1.0 + scale) + shift).astype(BF16)


def _ln_mod(x, mod_l, shift_i, scale_i):
    tm = 512
    return pl.pallas_call(
        functools.partial(_ln_mod_kernel, shift_i=shift_i, scale_i=scale_i),
        grid=(T_ALL // tm,),
        in_specs=[
            pl.BlockSpec((tm, D_MODEL), lambda i: (i, 0)),
            pl.BlockSpec((1, 6, D_MODEL), lambda i: (_mod_row(i, tm), 0, 0)),
        ],
        out_specs=pl.BlockSpec((tm, D_MODEL), lambda i: (i, 0)),
        out_shape=jax.ShapeDtypeStruct((T_ALL, D_MODEL), BF16),
        compiler_params=_cp(("arbitrary",)),
        name="ln_mod",
    )(x, mod_l)


def _res_ln_kernel(x_ref, y_ref, mod_ref, g_ref, b_ref, *rest, gate_i, next_shift_i, next_scale_i):
    if next_shift_i is None:
        (xo_ref,) = rest
    else:
        nmod_ref, xo_ref, h_ref = rest
    gate = mod_ref[0, gate_i:gate_i + 1, :]
    z = ALPHA * x_ref[...] + gate * y_ref[...]
    xn = _layer_norm(z) * g_ref[...] + b_ref[...]
    xo_ref[...] = xn
    if next_shift_i is not None:
        scale = nmod_ref[0, next_scale_i:next_scale_i + 1, :]
        shift = nmod_ref[0, next_shift_i:next_shift_i + 1, :]
        h_ref[...] = (_layer_norm(xn) * (1.0 + scale) + shift).astype(BF16)


def _res_ln(x, y, mod_l, gate_i, g, b, next_mod=None, next_shift_i=None, next_scale_i=None):
    tm = 512
    row = pl.BlockSpec((tm, D_MODEL), lambda i: (i, 0))
    mod_spec = pl.BlockSpec((1, 6, D_MODEL), lambda i: (_mod_row(i, tm), 0, 0))
    vec = pl.BlockSpec((1, D_MODEL), lambda i: (0, 0))
    in_specs = [row, row, mod_spec, vec, vec]
    args = [x, y, mod_l, g.reshape(1, D_MODEL), b.reshape(1, D_MODEL)]
    out_specs = [row]
    out_shape = [jax.ShapeDtypeStruct((T_ALL, D_MODEL), F32)]
    if next_mod is not None:
        in_specs.append(mod_spec)
        args.append(next_mod)
        out_specs.append(row)
        out_shape.append(jax.ShapeDtypeStruct((T_ALL, D_MODEL), BF16))
    out = pl.pallas_call(
        functools.partial(_res_ln_kernel, gate_i=gate_i, next_shift_i=next_shift_i,
                          next_scale_i=next_scale_i),
        grid=(T_ALL // tm,),
        in_specs=in_specs,
        out_specs=out_specs,
        out_shape=out_shape,
        compiler_params=_cp(("arbitrary",)),
        name="res_ln",
    )(*args)
    return out if next_mod is not None else (out[0], None)


def _matmul_kernel(a_ref, w_ref, o_ref, wb_ref):
    @pl.when(pl.program_id(1) == 0)
    def _():
        wb_ref[...] = w_ref[...].astype(BF16)

    o_ref[...] = jnp.dot(a_ref[...], wb_ref[...], preferred_element_type=F32).astype(o_ref.dtype)


def _matmul(a, w, out_dtype=F32, tm=1024, tn=512):
    m, k = a.shape
    n = w.shape[1]
    return pl.pallas_call(
        _matmul_kernel,
        grid=(n // tn, m // tm),
        in_specs=[
            pl.BlockSpec((tm, k), lambda j, i: (i, 0)),
            pl.BlockSpec((k, tn), lambda j, i: (0, j)),
        ],
        out_specs=pl.BlockSpec((tm, tn), lambda j, i: (i, j)),
        out_shape=jax.ShapeDtypeStruct((m, n), out_dtype),
        scratch_shapes=[pltpu.VMEM((k, tn), BF16)],
        compiler_params=_cp(("arbitrary", "arbitrary")),
        name="matmul",
    )(a, w)


def _pool_kernel(a_ref, w_ref, s_ref, o_ref):
    i = pl.program_id(0)
    pos, seq_len = _seq_pos(i, ROW_BLK)
    for g, win in enumerate(POOL_WINDOWS):
        left = win // 2
        right = win - 1 - left
        x = a_ref[:, g * POOL_CH:(g + 1) * POOL_CH]
        acc = x
        for d in range(-left, right + 1):
            if d == 0:
                continue
            valid = jnp.logical_and(pos + d >= 0, pos + d < seq_len)
            acc = acc + jnp.where(valid, _shift_rows(x, d), 0.0)
        lo = jnp.maximum(pos - left, 0)
        hi = jnp.minimum(pos + right + 1, seq_len)
        pooled = acc / (hi - lo).astype(F32) - x
        mixed = jnp.dot(pooled.astype(BF16), w_ref[g].astype(BF16), preferred_element_type=F32)
        o_ref[:, g * POOL_CH:(g + 1) * POOL_CH] = (
            mixed * s_ref[:, g * POOL_CH:(g + 1) * POOL_CH]).astype(BF16)


def _pool_mixer(proj, pool_w, pool_scale):
    return pl.pallas_call(
        _pool_kernel,
        grid=(N_ROW_BLK,),
        in_specs=[
            pl.BlockSpec((ROW_BLK, POOL_WIDTH), lambda i: (i, COL_POOL // POOL_WIDTH)),
            pl.BlockSpec((POOL_GROUPS, POOL_CH, POOL_CH), lambda i: (0, 0, 0)),
            pl.BlockSpec((1, POOL_WIDTH), lambda i: (0, 0)),
        ],
        out_specs=pl.BlockSpec((ROW_BLK, POOL_WIDTH), lambda i: (i, 0)),
        out_shape=jax.ShapeDtypeStruct((T_ALL, POOL_WIDTH), BF16),
        compiler_params=_cp(("arbitrary",)),
        name="pool_mixer",
    )(proj, pool_w, pool_scale.reshape(1, POOL_WIDTH))


def _conv_kernel(a_ref, g_ref, dw_ref, db_ref, lg_ref, lb_ref, o_ref):
    i = pl.program_id(0)
    pos, seq_len = _seq_pos(i, ROW_BLK)
    x = a_ref[...] * jax.nn.sigmoid(g_ref[...])
    half = CONV_K // 2
    acc = x * dw_ref[half:half + 1, :]
    for k in range(CONV_K):
        d = k - half
        if d == 0:
            continue
        valid = jnp.logical_and(pos + d >= 0, pos + d < seq_len)
        acc = acc + jnp.where(valid, _shift_rows(x, d), 0.0) * dw_ref[k:k + 1, :]
    y = _layer_norm(acc + db_ref[...]) * lg_ref[...] + lb_ref[...]
    o_ref[...] = (y * jax.nn.sigmoid(y)).astype(BF16)


def _conv_mixer(proj, dw, db, ln_g, ln_b):
    cb = COL_CV // CONV_WIDTH
    vec = pl.BlockSpec((1, CONV_WIDTH), lambda i: (0, 0))
    return pl.pallas_call(
        _conv_kernel,
        grid=(N_ROW_BLK,),
        in_specs=[
            pl.BlockSpec((ROW_BLK, CONV_WIDTH), lambda i: (i, cb)),
            pl.BlockSpec((ROW_BLK, CONV_WIDTH), lambda i: (i, cb + 1)),
            pl.BlockSpec((CONV_K, CONV_WIDTH), lambda i: (0, 0)),
            vec, vec, vec,
        ],
        out_specs=pl.BlockSpec((ROW_BLK, CONV_WIDTH), lambda i: (i, 0)),
        out_shape=jax.ShapeDtypeStruct((T_ALL, CONV_WIDTH), BF16),
        compiler_params=_cp(("arbitrary",)),
        name="conv_mixer",
    )(proj, proj, dw, db.reshape(1, -1), ln_g.reshape(1, -1), ln_b.reshape(1, -1))


def _s5_scan_kernel(u_ref, bq_ref, cq_ref, a_ref, h0_ref, y_ref, hf_ref, hs_ref, st_ref, *, tc, nb):
    d = pl.program_id(0)
    j = pl.program_id(1)

    @pl.when(j == 0)
    def _():
        st_ref[...] = h0_ref[0]

    rows = tc * nb
    u2 = u_ref[...].reshape(rows, S5_WIDTH).astype(BF16)
    for q in range(S5_Q):
        hs_ref[:, q * 2 * S5_QS:(q + 1) * 2 * S5_QS] = jnp.dot(
            u2[:, q * S5_QU:(q + 1) * S5_QU], bq_ref[0, q], preferred_element_type=F32)

    lane_w = 256
    for q in range(S5_Q):
        for w in range(S5_QS // lane_w):
            re0 = q * 2 * S5_QS + w * lane_w
            im0 = re0 + S5_QS
            a_re = jnp.broadcast_to(a_ref[0, :, re0:re0 + lane_w], (nb, lane_w))
            a_im = jnp.broadcast_to(a_ref[0, :, im0:im0 + lane_w], (nb, lane_w))

            def step(t, carry, re0=re0, im0=im0, a_re=a_re, a_im=a_im):
                h_re, h_im = carry
                te = jnp.where(d == 0, t, tc - 1 - t)
                r0 = pl.multiple_of(te * nb, nb)
                n_re = a_re * h_re - a_im * h_im + hs_ref[pl.ds(r0, nb), re0:re0 + lane_w]
                n_im = a_re * h_im + a_im * h_re + hs_ref[pl.ds(r0, nb), im0:im0 + lane_w]
                hs_ref[pl.ds(r0, nb), re0:re0 + lane_w] = n_re
                hs_ref[pl.ds(r0, nb), im0:im0 + lane_w] = n_im
                return n_re, n_im

            h_re, h_im = lax.fori_loop(
                0, tc, step, (st_ref[:, re0:re0 + lane_w], st_ref[:, im0:im0 + lane_w]), unroll=4)
            st_ref[:, re0:re0 + lane_w] = h_re
            st_ref[:, im0:im0 + lane_w] = h_im

    for q in range(S5_Q):
        yq = jnp.dot(hs_ref[:, q * 2 * S5_QS:(q + 1) * 2 * S5_QS].astype(BF16), cq_ref[0, q],
                     preferred_element_type=F32)
        y_ref[0, :, :, q * S5_QU:(q + 1) * S5_QU] = yq.reshape(tc, nb, S5_QU)

    @pl.when(j == pl.num_programs(1) - 1)
    def _():
        hf_ref[0] = st_ref[...]


def _s5_scan(u_tb, bq, cq, a_vec, h0, tc):
    n, nb, _ = u_tb.shape
    nc = n // tc

    def chunk(d, j):
        return j + d * (nc - 1 - 2 * j)

    return pl.pallas_call(
        functools.partial(_s5_scan_kernel, tc=tc, nb=nb),
        grid=(2, nc),
        in_specs=[
            pl.BlockSpec((tc, nb, S5_WIDTH), lambda d, j: (chunk(d, j), 0, 0)),
            pl.BlockSpec((1, S5_Q, S5_QU, 2 * S5_QS), lambda d, j: (d, 0, 0, 0)),
            pl.BlockSpec((1, S5_Q, 2 * S5_QS, S5_QU), lambda d, j: (d, 0, 0, 0)),
            pl.BlockSpec((1, 1, S5_LANES), lambda d, j: (d, 0, 0)),
            pl.BlockSpec((1, nb, S5_LANES), lambda d, j: (d, 0, 0)),
        ],
        out_specs=[
            pl.BlockSpec((1, tc, nb, S5_WIDTH), lambda d, j: (d, chunk(d, j), 0, 0)),
            pl.BlockSpec((1, nb, S5_LANES), lambda d, j: (d, 0, 0)),
        ],
        out_shape=[
            jax.ShapeDtypeStruct((2, n, nb, S5_WIDTH), F32),
            jax.ShapeDtypeStruct((2, nb, S5_LANES), F32),
        ],
        scratch_shapes=[
            pltpu.VMEM((tc * nb, S5_LANES), F32),
            pltpu.VMEM((nb, S5_LANES), F32),
        ],
        compiler_params=_cp(("arbitrary", "arbitrary")),
        name="s5_scan",
    )(u_tb, bq, cq, a_vec, h0)


def _s5_post_kernel(yf_ref, yb_ref, u_ref, d_ref, w_ref, o_ref):
    y = yf_ref[0] + yb_ref[0] + d_ref[...] * u_ref[...]
    y = jax.nn.gelu(y)
    gate = jnp.dot(y.astype(BF16), w_ref[...].astype(BF16), preferred_element_type=F32)
    o_ref[...] = (y * jax.nn.sigmoid(gate)).astype(BF16)


def _s5_post(y_dirs, u_tb, d_skip, w_glu):
    _, n, nb, _ = y_dirs.shape
    rows = n * nb
    tr = 1024
    y2 = y_dirs.reshape(2, rows, S5_WIDTH)
    return pl.pallas_call(
        _s5_post_kernel,
        grid=(rows // tr,),
        in_specs=[
            pl.BlockSpec((1, tr, S5_WIDTH), lambda i: (0, i, 0)),
            pl.BlockSpec((1, tr, S5_WIDTH), lambda i: (1, i, 0)),
            pl.BlockSpec((tr, S5_WIDTH), lambda i: (i, 0)),
            pl.BlockSpec((1, S5_WIDTH), lambda i: (0, 0)),
            pl.BlockSpec((S5_WIDTH, S5_WIDTH), lambda i: (0, 0)),
        ],
        out_specs=pl.BlockSpec((tr, S5_WIDTH), lambda i: (i, 0)),
        out_shape=jax.ShapeDtypeStruct((rows, S5_WIDTH), BF16),
        compiler_params=_cp(("arbitrary",)),
        name="s5_post",
    )(y2, y2, u_tb.reshape(rows, S5_WIDTH), d_skip.reshape(1, S5_WIDTH), w_glu)


def _time_to_token_major(y_tm, n, nb, n_batch):
    y = y_tm.reshape(n, nb, S5_WIDTH)[:, :n_batch]
    return y.transpose(1, 0, 2).reshape(n_batch * n, S5_WIDTH)


def _s5_params(lam_re, lam_im, log_dt, b_re, b_im, c_re, c_im):
    dt = jnp.exp(log_dt)[..., None]
    x = lam_re * dt
    y = lam_im * dt
    ex = jnp.exp(x)
    abar_re = ex * jnp.cos(y)
    abar_im = ex * jnp.sin(y)
    num_re = jnp.expm1(x) * jnp.cos(y) - 2.0 * jnp.square(jnp.sin(0.5 * y))
    num_im = abar_im
    den = lam_re * lam_re + lam_im * lam_im
    coef_re = (num_re * lam_re + num_im * lam_im) / den
    coef_im = (num_im * lam_re - num_re * lam_im) / den
    bbar_re = coef_re[..., None] * b_re - coef_im[..., None] * b_im
    bbar_im = coef_re[..., None] * b_im + coef_im[..., None] * b_re
    eye = jnp.eye(S5_GQ, dtype=F32)
    bb = jnp.stack([bbar_re, bbar_im]).reshape(2, 2, S5_Q, S5_GQ, S5_STATE, S5_CH)
    bq = jnp.einsum("rdqgpc,gh->dqgcrhp", bb, eye).reshape(2, S5_Q, S5_QU, 2 * S5_QS)
    cc = jnp.stack([c_re, -c_im]).reshape(2, 2, S5_Q, S5_GQ, S5_CH, S5_STATE)
    cq = jnp.einsum("rdqgcp,gh->dqrhpgc", cc, eye).reshape(2, S5_Q, 2 * S5_QS, S5_QU)
    ab = jnp.stack([abar_re, abar_im]).reshape(2, 2, S5_Q, S5_GQ, S5_STATE)
    a_vec = ab.transpose(1, 2, 0, 3, 4).reshape(2, 1, S5_LANES)
    return bq.astype(BF16), cq.astype(BF16), a_vec


def _s5_state_to_lanes(st):
    b = st.shape[0]
    s = st.reshape(b, 2, 2, S5_Q, S5_GQ, S5_STATE).transpose(1, 0, 3, 2, 4, 5)
    return s.reshape(2, b, S5_LANES)


def _s5_lanes_to_state(hl):
    b = hl.shape[1]
    s = hl.reshape(2, b, S5_Q, 2, S5_GQ, S5_STATE).transpose(1, 0, 3, 2, 4, 5)
    return s.reshape(b, 2, 2, S5_GROUPS, S5_STATE)


def _rope(x, cos, sin_signed):
    lane = lax.broadcasted_iota(jnp.int32, x.shape, 1)
    quarter = HEAD_DIM // 4
    partner = jnp.where((lane & quarter) == 0,
                        pltpu.roll(x, HEAD_DIM - quarter, axis=1),
                        pltpu.roll(x, quarter, axis=1))
    return x * cos + partner * sin_signed


def _nt_dot(a, b):
    return lax.dot_general(a, b, (((1,), (1,)), ((), ())), preferred_element_type=F32)


def _ctx_attn_kernel(q_ref, k_ref, v_ref, sink_ref, o_ref):
    h0 = pl.program_id(1) * REP
    kb = k_ref[...].astype(BF16)
    vb = v_ref[...].astype(BF16)
    for r in range(REP):
        qb = q_ref[:, r * HEAD_DIM:(r + 1) * HEAD_DIM].astype(BF16)
        s = _nt_dot(qb, kb) * (HEAD_DIM ** -0.5)
        sink = sink_ref[pl.ds(h0 + r, 1), 0:1]
        m = jnp.maximum(jnp.max(s, axis=-1, keepdims=True), sink)
        p = jnp.exp(s - m)
        denom = jnp.sum(p, axis=-1, keepdims=True) + jnp.exp(sink - m)
        o = jnp.dot(p.astype(BF16), vb, preferred_element_type=F32) / denom
        o_ref[:, r * HEAD_DIM:(r + 1) * HEAD_DIM] = o.astype(BF16)


def _ctx_attention(proj, sink_tile):
    qw = REP * HEAD_DIM
    return pl.pallas_call(
        _ctx_attn_kernel,
        grid=(BATCH, N_KV_HEADS),
        in_specs=[
            pl.BlockSpec((SEQ, qw), lambda b, h: (b, COL_Q // qw + h)),
            pl.BlockSpec((SEQ, HEAD_DIM), lambda b, h: (b, COL_K // HEAD_DIM + h)),
            pl.BlockSpec((SEQ, HEAD_DIM), lambda b, h: (b, COL_V // HEAD_DIM + h)),
            pl.BlockSpec((N_HEADS, HEAD_DIM), lambda b, h: (0, 0)),
        ],
        out_specs=pl.BlockSpec((SEQ, qw), lambda b, h: (b, h)),
        out_shape=jax.ShapeDtypeStruct((T_CTX, ATTN_WIDTH), BF16),
        compiler_params=_cp(("arbitrary", "arbitrary")),
        name="ctx_attention",
    )(proj, proj, proj, sink_tile)


def _lat_attn_kernel(q_ref, k_ref, v_ref, kc_ref, vc_ref, cq_ref, sq_ref, ck_ref, sk_ref,
                     sink_ref, o_ref, *, tq):
    h0 = pl.program_id(1) * REP
    q0 = pl.program_id(2) * tq
    kb = _rope(k_ref[...], ck_ref[...], sk_ref[...]).astype(BF16)
    vb = v_ref[...].astype(BF16)
    kcb = kc_ref[...].astype(BF16)
    vcb = vc_ref[...].astype(BF16)
    qpos = q0 + lax.broadcasted_iota(jnp.int32, (tq, DEC_SEQ), 0)
    kpos = lax.broadcasted_iota(jnp.int32, (tq, DEC_SEQ), 1)
    mask = jnp.abs(qpos - kpos) <= WINDOW
    scale = HEAD_DIM ** -0.5
    for r in range(REP):
        q = _rope(q_ref[:, r * HEAD_DIM:(r + 1) * HEAD_DIM], cq_ref[...], sq_ref[...]).astype(BF16)
        s_lat = jnp.where(mask, _nt_dot(q, kb) * scale, NEG_INF)
        s_ctx = _nt_dot(q, kcb) * scale
        sink = sink_ref[pl.ds(h0 + r, 1), 0:1]
        m = jnp.maximum(jnp.maximum(jnp.max(s_lat, axis=-1, keepdims=True),
                                    jnp.max(s_ctx, axis=-1, keepdims=True)), sink)
        p_lat = jnp.exp(s_lat - m)
        p_ctx = jnp.exp(s_ctx - m)
        denom = (jnp.sum(p_lat, axis=-1, keepdims=True) + jnp.sum(p_ctx, axis=-1, keepdims=True)
                 + jnp.exp(sink - m))
        o = (jnp.dot(p_lat.astype(BF16), vb, preferred_element_type=F32)
             + jnp.dot(p_ctx.astype(BF16), vcb, preferred_element_type=F32)) / denom
        o_ref[:, r * HEAD_DIM:(r + 1) * HEAD_DIM] = o.astype(BF16)


def _lat_attention(proj, cache_k_l, cache_v_l, rope_cos, rope_sin, sink_tile):
    tq = 256
    nq = DEC_SEQ // tq
    qw = REP * HEAD_DIM
    row0_q = T_CTX // tq
    row0_k = T_CTX // DEC_SEQ
    tab_q = pl.BlockSpec((tq, HEAD_DIM), lambda b, h, i: (i, 0))
    tab_k = pl.BlockSpec((DEC_SEQ, HEAD_DIM), lambda b, h, i: (0, 0))
    return pl.pallas_call(
        functools.partial(_lat_attn_kernel, tq=tq),
        grid=(DEC_BATCH, N_KV_HEADS, nq),
        in_specs=[
            pl.BlockSpec((tq, qw), lambda b, h, i: (row0_q + b * nq + i, COL_Q // qw + h)),
            pl.BlockSpec((DEC_SEQ, HEAD_DIM), lambda b, h, i: (row0_k + b, COL_K // HEAD_DIM + h)),
            pl.BlockSpec((DEC_SEQ, HEAD_DIM), lambda b, h, i: (row0_k + b, COL_V // HEAD_DIM + h)),
            pl.BlockSpec((None, PAST_LEN, HEAD_DIM), lambda b, h, i: (b, 0, h)),
            pl.BlockSpec((None, PAST_LEN, HEAD_DIM), lambda b, h, i: (b, 0, h)),
            tab_q, tab_q, tab_k, tab_k,
            pl.BlockSpec((N_HEADS, HEAD_DIM), lambda b, h, i: (0, 0)),
        ],
        out_specs=pl.BlockSpec((tq, qw), lambda b, h, i: (b * nq + i, h)),
        out_shape=jax.ShapeDtypeStruct((T_LAT, ATTN_WIDTH), BF16),
        compiler_params=_cp(("arbitrary", "arbitrary", "arbitrary")),
        name="lat_attention",
    )(proj, proj, proj, cache_k_l, cache_v_l, rope_cos, rope_sin, rope_cos, rope_sin, sink_tile)


def _rope_tables():
    t = jnp.arange(DEC_SEQ)
    row, col = t // GRID_W, t % GRID_W
    quarter = HEAD_DIM // 4
    inv = ROPE_BASE ** (-jnp.arange(quarter, dtype=F32) / quarter)
    lane = jnp.arange(HEAD_DIM)
    pos = jnp.where(lane[None, :] < HEAD_DIM // 2, row[:, None], col[:, None]).astype(F32)
    ang = pos * inv[lane % quarter][None, :]
    sign = jnp.where((lane & quarter) == 0, -1.0, 1.0).astype(F32)
    return jnp.cos(ang), jnp.sin(ang) * sign[None, :]


def _merge_kernel(yp_ref, ys_ref, ya_ref, yc_ref, g0_ref, g1_ref, g2_ref, g3_ref,
                  wp_ref, ws_ref, wa_ref, wc_ref, o_ref):
    acc = None
    for y_ref, g_ref, w_ref in ((yp_ref, g0_ref, wp_ref), (ys_ref, g1_ref, ws_ref),
                                (ya_ref, g2_ref, wa_ref), (yc_ref, g3_ref, wc_ref)):
        t = jax.nn.sigmoid(g_ref[...]) * jnp.dot(
            y_ref[...], w_ref[...].astype(BF16), preferred_element_type=F32)
        acc = t if acc is None else acc + t
    o_ref[...] = acc.astype(BF16)


def _merge(proj, y_pool, y_s5, attn, y_conv, w_pool, w_s5, w_attn, w_conv):
    tm, tn = 1024, 512
    gb = COL_GATE // tn
    per = D_MODEL // tn

    def act(width):
        return pl.BlockSpec((tm, width), lambda j, i: (i, 0))

    def gate(branch):
        return pl.BlockSpec((tm, tn), lambda j, i: (i, gb + branch * per + j))

    def wt(width):
        return pl.BlockSpec((width, tn), lambda j, i: (0, j))

    return pl.pallas_call(
        _merge_kernel,
        grid=(D_MODEL // tn, T_ALL // tm),
        in_specs=[act(POOL_WIDTH), act(S5_WIDTH), act(ATTN_WIDTH), act(CONV_WIDTH),
                  gate(0), gate(1), gate(2), gate(3),
                  wt(POOL_WIDTH), wt(S5_WIDTH), wt(ATTN_WIDTH), wt(CONV_WIDTH)],
        out_specs=pl.BlockSpec((tm, tn), lambda j, i: (i, j)),
        out_shape=jax.ShapeDtypeStruct((T_ALL, D_MODEL), BF16),
        compiler_params=_cp(("arbitrary", "arbitrary")),
        name="merge",
    )(y_pool, y_s5, attn, y_conv, proj, proj, proj, proj, w_pool, w_s5, w_attn, w_conv)


def _ffn_kernel(h_ref, wg_ref, wv_ref, dwg_ref, dwv_ref, dbg_ref, dbv_ref, wd_ref, o_ref):
    i = pl.program_id(0)
    f = pl.program_id(1)
    pos, seq_len = _seq_pos(i, ROW_BLK)
    first = pos == 0
    last = pos == seq_len - 1
    h = h_ref[...]

    def conv3(w_ref, dw_ref, db_ref):
        u = jnp.dot(h, w_ref[...].astype(BF16), preferred_element_type=F32)
        prev = jnp.where(first, 0.0, _shift_rows(u, -1))
        nxt = jnp.where(last, 0.0, _shift_rows(u, 1))
        return prev * dw_ref[0:1, :] + u * dw_ref[1:2, :] + nxt * dw_ref[2:3, :] + db_ref[...]

    gt = conv3(wg_ref, dwg_ref, dbg_ref)
    val = conv3(wv_ref, dwv_ref, dbv_ref)
    act = (gt * jax.nn.sigmoid(gt) * val).astype(BF16)
    part = jnp.dot(act, wd_ref[...].astype(BF16), preferred_element_type=F32)

    @pl.when(f == 0)
    def _():
        o_ref[...] = part

    @pl.when(f > 0)
    def _():
        o_ref[...] += part


def _conv_ffn(h, w_up, dw, db, w_down):
    tf = 256
    nf = D_FF // tf
    db2 = db.reshape(1, 2 * D_FF)
    return pl.pallas_call(
        _ffn_kernel,
        grid=(N_ROW_BLK, nf),
        in_specs=[
            pl.BlockSpec((ROW_BLK, D_MODEL), lambda i, f: (i, 0)),
            pl.BlockSpec((D_MODEL, tf), lambda i, f: (0, f)),
            pl.BlockSpec((D_MODEL, tf), lambda i, f: (0, nf + f)),
            pl.BlockSpec((FFN_CONV_K, tf), lambda i, f: (0, f)),
            pl.BlockSpec((FFN_CONV_K, tf), lambda i, f: (0, nf + f)),
            pl.BlockSpec((1, tf), lambda i, f: (0, f)),
            pl.BlockSpec((1, tf), lambda i, f: (0, nf + f)),
            pl.BlockSpec((tf, D_MODEL), lambda i, f: (f, 0)),
        ],
        out_specs=pl.BlockSpec((ROW_BLK, D_MODEL), lambda i, f: (i, 0)),
        out_shape=jax.ShapeDtypeStruct((T_ALL, D_MODEL), F32),
        compiler_params=_cp(("arbitrary", "arbitrary")),
        name="conv_ffn",
    )(h, w_up, w_up, dw, dw, db2, db2, w_down)


def kernel(x_prompt, x_sample, cache_k, cache_v, state_s5, c, c_ctx, w_ada, b_ada, w_in, pool_w, pool_scale, s5_lambda_re, s5_lambda_im, s5_log_dt, s5_b_re, s5_b_im, s5_c_re, s5_c_im, s5_d, s5_w_glu, attn_sink, conv_dw, conv_db, conv_ln_g, conv_ln_b, w_br_pool, w_br_s5, w_br_attn, w_br_conv, w_out, ln1_g, ln1_b, ffn_w_up, ffn_dw, ffn_db, ffn_w_down, ln2_g, ln2_b):
    x = jnp.concatenate([x_prompt.reshape(T_CTX, D_MODEL), x_sample.reshape(T_LAT, D_MODEL)], axis=0)
    cvec = jnp.concatenate(
        [c_ctx[None, :], c, jnp.zeros((MOD_ROWS - 1 - DEC_BATCH, D_MODEL), F32)], axis=0)
    mod = _mod_table(cvec, w_ada, b_ada).reshape(DEPTH, MOD_ROWS, 6, D_MODEL)
    rope_cos, rope_sin = _rope_tables()

    ks_out, vs_out, ss_out = [], [], []
    h = _ln_mod(x, mod[0], 0, 1)
    for l in range(DEPTH):
        proj = _matmul(h, w_in[l])

        y_pool = _pool_mixer(proj, pool_w[l], pool_scale[l])
        y_conv = _conv_mixer(proj, conv_dw[l], conv_db[l], conv_ln_g[l], conv_ln_b[l])

        bq, cq, a_vec = _s5_params(s5_lambda_re[l], s5_lambda_im[l], s5_log_dt[l], s5_b_re[l],
                                   s5_b_im[l], s5_c_re[l], s5_c_im[l])
        u = proj[:, COL_S5:COL_S5 + S5_WIDTH]
        u_ctx = u[:T_CTX].reshape(BATCH, SEQ, S5_WIDTH).transpose(1, 0, 2)
        u_lat = u[T_CTX:].reshape(DEC_BATCH, DEC_SEQ, S5_WIDTH).transpose(1, 0, 2)
        u_lat = jnp.pad(u_lat, ((0, 0), (0, S5_LAT_B - DEC_BATCH), (0, 0)))
        h0_ctx = jnp.zeros((2, BATCH, S5_LANES), F32)
        h0_lat = jnp.pad(_s5_state_to_lanes(state_s5[:, l]), ((0, 0), (0, S5_LAT_B - DEC_BATCH), (0, 0)))
        y_ctx_dirs, hf_ctx = _s5_scan(u_ctx, bq, cq, a_vec, h0_ctx, tc=32)
        y_lat_dirs, _ = _s5_scan(u_lat, bq, cq, a_vec, h0_lat, tc=64)
        y_s5 = jnp.concatenate([
            _time_to_token_major(_s5_post(y_ctx_dirs, u_ctx, s5_d[l], s5_w_glu[l]),
                                 SEQ, BATCH, BATCH),
            _time_to_token_major(_s5_post(y_lat_dirs, u_lat, s5_d[l], s5_w_glu[l]),
                                 DEC_SEQ, S5_LAT_B, DEC_BATCH)], axis=0)

        sink_tile = jnp.broadcast_to(attn_sink[l][:, None], (N_HEADS, HEAD_DIM))
        attn = jnp.concatenate([
            _ctx_attention(proj, sink_tile),
            _lat_attention(proj, cache_k[:, l].reshape(DEC_BATCH, PAST_LEN, KV_WIDTH),
                           cache_v[:, l].reshape(DEC_BATCH, PAST_LEN, KV_WIDTH),
                           rope_cos, rope_sin, sink_tile)], axis=0)

        merged = _merge(proj, y_pool, y_s5, attn, y_conv,
                        w_br_pool[l], w_br_s5[l], w_br_attn[l], w_br_conv[l])
        mixed = _matmul(merged, w_out[l])
        x, h2 = _res_ln(x, mixed, mod[l], 2, ln1_g[l], ln1_b[l], mod[l], 3, 4)
        f = _conv_ffn(h2, ffn_w_up[l], ffn_dw[l], ffn_db[l], ffn_w_down[l])
        if l + 1 < DEPTH:
            x, h = _res_ln(x, f, mod[l], 5, ln2_g[l], ln2_b[l], mod[l + 1], 0, 1)
        else:
            x, _ = _res_ln(x, f, mod[l], 5, ln2_g[l], ln2_b[l])

        ks_out.append(proj[:T_CTX, COL_K:COL_K + KV_WIDTH].reshape(BATCH, SEQ, N_KV_HEADS, HEAD_DIM))
        vs_out.append(proj[:T_CTX, COL_V:COL_V + KV_WIDTH].reshape(BATCH, SEQ, N_KV_HEADS, HEAD_DIM))
        ss_out.append(_s5_lanes_to_state(hf_ctx))

    y_prompt = x[:T_CTX].reshape(BATCH, SEQ, D_MODEL)
    y_sample = x[T_CTX:].reshape(DEC_BATCH, DEC_SEQ, D_MODEL)
    return (y_prompt, y_sample, jnp.stack(ks_out, axis=1), jnp.stack(vs_out, axis=1),
            jnp.stack(ss_out, axis=1))
```

```python
import functools

import jax
import jax.numpy as jnp
from jax import lax
from jax.experimental import pallas as pl
from jax.experimental.pallas import tpu as pltpu

F32 = jnp.float32
BF16 = jnp.bfloat16

D_MODEL = 2048
BATCH = 16
SEQ = 256
DEPTH = 2
DEC_BATCH = 4
DEC_SEQ = 1024
PAST_LEN = 512
GRID_W = 64
POOL_WIDTH = 512
POOL_GROUPS = 4
POOL_CH = POOL_WIDTH // POOL_GROUPS
POOL_WINDOWS = (2, 4, 8, 16)
S5_WIDTH = 512
S5_CH = 16
S5_GROUPS = S5_WIDTH // S5_CH
S5_STATE = 64
N_HEADS = 8
N_KV_HEADS = 2
HEAD_DIM = 128
REP = N_HEADS // N_KV_HEADS
ATTN_WIDTH = N_HEADS * HEAD_DIM
KV_WIDTH = N_KV_HEADS * HEAD_DIM
WINDOW = 128
ROPE_BASE = 10000.0
NEG_INF = -1e30
CONV_WIDTH = 512
CONV_K = 31
N_BRANCH = 4
D_FF = 5632
FFN_CONV_K = 3
ALPHA = (2.0 * DEPTH) ** 0.25
LN_EPS = 1e-5
MIX_COLS = POOL_WIDTH + S5_WIDTH + ATTN_WIDTH + 2 * KV_WIDTH + 2 * CONV_WIDTH
N_IN = MIX_COLS + N_BRANCH * D_MODEL

COL_POOL = 0
COL_S5 = POOL_WIDTH
COL_Q = COL_S5 + S5_WIDTH
COL_K = COL_Q + ATTN_WIDTH
COL_V = COL_K + KV_WIDTH
COL_CV = COL_V + KV_WIDTH
COL_GATE = MIX_COLS

T_CTX = BATCH * SEQ
T_LAT = DEC_BATCH * DEC_SEQ
T_ALL = T_CTX + T_LAT
ROW_BLK = 1024
N_ROW_BLK = T_ALL // ROW_BLK
N_CTX_BLK = T_CTX // ROW_BLK
MOD_ROWS = 8
LN_ROWS = 512

S5_Q = 4
S5_GQ = S5_GROUPS // S5_Q
S5_QU = S5_GQ * S5_CH
S5_QS = S5_GQ * S5_STATE
S5_LANES = 2 * S5_GROUPS * S5_STATE
S5_LAT_B = 8

VMEM_LIMIT = 56 * 1024 * 1024


def _cp(sem):
    return pltpu.CompilerParams(dimension_semantics=sem, vmem_limit_bytes=VMEM_LIMIT)


def _mod_row(row_block, rows_per_block):
    lat_batch = (row_block * rows_per_block) // DEC_SEQ - T_CTX // DEC_SEQ
    return jnp.maximum(lat_batch + 1, 0)


def _seq_pos(row_block, rows):
    r = lax.broadcasted_iota(jnp.int32, (rows, 1), 0)
    seq_len = jnp.where(row_block < N_CTX_BLK, SEQ, DEC_SEQ)
    return r & (seq_len - 1), seq_len


def _layer_norm(x):
    mu = jnp.mean(x, axis=-1, keepdims=True)
    xc = x - mu
    var = jnp.mean(xc * xc, axis=-1, keepdims=True)
    return xc * lax.rsqrt(var + LN_EPS)


def _shift_rows(x, d):
    n = x.shape[0]
    return pltpu.roll(x, (-d) % n, axis=0)


def _mod_kernel(c_ref, w_ref, b_ref, o_ref):
    c = c_ref[...]
    a = c * jax.nn.sigmoid(c)
    o_ref[0] = jnp.dot(a, w_ref[0], preferred_element_type=F32,
                       precision=lax.Precision.HIGHEST) + b_ref[0]


def _mod_table(cvec, w_ada, b_ada):
    tn = 1024
    n6 = 6 * D_MODEL
    return pl.pallas_call(
        _mod_kernel,
        grid=(DEPTH, n6 // tn),
        in_specs=[
            pl.BlockSpec((MOD_ROWS, D_MODEL), lambda l, j: (0, 0)),
            pl.BlockSpec((1, D_MODEL, tn), lambda l, j: (l, 0, j)),
            pl.BlockSpec((1, 1, tn), lambda l, j: (l, 0, j)),
        ],
        out_specs=pl.BlockSpec((1, MOD_ROWS, tn), lambda l, j: (l, 0, j)),
        out_shape=jax.ShapeDtypeStruct((DEPTH, MOD_ROWS, n6), F32),
        compiler_params=_cp(("arbitrary", "arbitrary")),
        name="mod_table",
    )(cvec, w_ada, b_ada.reshape(DEPTH, 1, n6))


N_CTX_LN = T_CTX // LN_ROWS


def _row_spec():
    return pl.BlockSpec((LN_ROWS, D_MODEL), lambda i: (i, 0))


def _ctx_part_spec():
    return pl.BlockSpec((LN_ROWS, D_MODEL), lambda i: (jnp.minimum(i, N_CTX_LN - 1), 0))


def _lat_part_spec():
    return pl.BlockSpec((LN_ROWS, D_MODEL), lambda i: (jnp.maximum(i - N_CTX_LN, 0), 0))


def _mod_spec():
    return pl.BlockSpec((1, 6, D_MODEL), lambda i: (_mod_row(i, LN_ROWS), 0, 0))


def _read_rows(refs):
    if len(refs) == 1:
        return refs[0][...]
    return jnp.where(pl.program_id(0) < N_CTX_LN, refs[0][...], refs[1][...])


def _ln_mod_kernel(xc_ref, xl_ref, mod_ref, h_ref, *, shift_i, scale_i):
    y = _layer_norm(_read_rows((xc_ref, xl_ref)))
    scale = mod_ref[0, scale_i:scale_i + 1, :]
    shift = mod_ref[0, shift_i:shift_i + 1, :]
    h_ref[...] = (y * (1.0 + scale) + shift).astype(BF16)


def _ln_mod(x_ctx, x_lat, mod_l, shift_i, scale_i):
    return pl.pallas_call(
        functools.partial(_ln_mod_kernel, shift_i=shift_i, scale_i=scale_i),
        grid=(T_ALL // LN_ROWS,),
        in_specs=[_ctx_part_spec(), _lat_part_spec(), _mod_spec()],
        out_specs=_row_spec(),
        out_shape=jax.ShapeDtypeStruct((T_ALL, D_MODEL), BF16),
        compiler_params=_cp(("arbitrary",)),
        name="ln_mod",
    )(x_ctx, x_lat, mod_l)


def _res_ln_kernel(*refs, n_x, gate_i, next_shift_i, next_scale_i, split_out):
    x_refs = refs[:n_x]
    y_ref, mod_ref, g_ref, b_ref = refs[n_x:n_x + 4]
    rest = refs[n_x + 4:]
    gate = mod_ref[0, gate_i:gate_i + 1, :]
    z = ALPHA * _read_rows(x_refs) + gate * y_ref[...]
    xn = _layer_norm(z) * g_ref[...] + b_ref[...]
    if split_out:
        xc_ref, xl_ref = rest

        @pl.when(pl.program_id(0) < N_CTX_LN)
        def _():
            xc_ref[...] = xn

        @pl.when(pl.program_id(0) >= N_CTX_LN)
        def _():
            xl_ref[...] = xn
    else:
        nmod_ref, xo_ref, h_ref = rest
        xo_ref[...] = xn
        scale = nmod_ref[0, next_scale_i:next_scale_i + 1, :]
        shift = nmod_ref[0, next_shift_i:next_shift_i + 1, :]
        h_ref[...] = (_layer_norm(xn) * (1.0 + scale) + shift).astype(BF16)


def _res_ln(x_parts, y, mod_l, gate_i, g, b, next_mod=None, next_shift_i=None, next_scale_i=None):
    vec = pl.BlockSpec((1, D_MODEL), lambda i: (0, 0))
    x_specs = [_row_spec()] if len(x_parts) == 1 else [_ctx_part_spec(), _lat_part_spec()]
    in_specs = x_specs + [_row_spec(), _mod_spec(), vec, vec]
    args = list(x_parts) + [y, mod_l, g.reshape(1, D_MODEL), b.reshape(1, D_MODEL)]
    split_out = next_mod is None
    if split_out:
        out_specs = [_ctx_part_spec(), _lat_part_spec()]
        out_shape = [jax.ShapeDtypeStruct((T_CTX, D_MODEL), F32),
                     jax.ShapeDtypeStruct((T_LAT, D_MODEL), F32)]
    else:
        in_specs.append(_mod_spec())
        args.append(next_mod)
        out_specs = [_row_spec(), _row_spec()]
        out_shape = [jax.ShapeDtypeStruct((T_ALL, D_MODEL), F32),
                     jax.ShapeDtypeStruct((T_ALL, D_MODEL), BF16)]
    return pl.pallas_call(
        functools.partial(_res_ln_kernel, n_x=len(x_parts), gate_i=gate_i,
                          next_shift_i=next_shift_i, next_scale_i=next_scale_i,
                          split_out=split_out),
        grid=(T_ALL // LN_ROWS,),
        in_specs=in_specs,
        out_specs=out_specs,
        out_shape=out_shape,
        compiler_params=_cp(("arbitrary",)),
        name="res_ln",
    )(*args)


def _matmul_kernel(a_ref, w_ref, o_ref, *, gate):
    y = jnp.dot(a_ref[...], w_ref[...].astype(BF16), preferred_element_type=F32)
    if gate:
        y = jax.nn.sigmoid(y)
    o_ref[...] = y.astype(o_ref.dtype)


def _matmul(a, w, layer, col0=0, ncols=None, out_dtype=F32, gate=False):
    tm, tn = 2048, 512
    m, k = a.shape
    ncols = w.shape[2] - col0 if ncols is None else ncols
    assert col0 % tn == 0 and ncols % tn == 0 and m % tm == 0
    jb = col0 // tn
    return pl.pallas_call(
        functools.partial(_matmul_kernel, gate=gate),
        grid=(m // tm, ncols // tn),
        in_specs=[
            pl.BlockSpec((tm, k), lambda i, j: (i, 0)),
            pl.BlockSpec((None, k, tn), lambda i, j: (layer, 0, jb + j)),
        ],
        out_specs=pl.BlockSpec((tm, tn), lambda i, j: (i, j)),
        out_shape=jax.ShapeDtypeStruct((m, ncols), out_dtype),
        compiler_params=_cp(("arbitrary", "arbitrary")),
        name="matmul",
    )(a, w)


PAD_LO = 16
PAD_SEQ = 32
STENCIL_ROWS = PAD_LO + (ROW_BLK // SEQ) * (SEQ + PAD_SEQ)


def _stencil_layout(seq_len):
    nseq = ROW_BLK // seq_len
    stride = seq_len + PAD_SEQ
    return nseq, stride, PAD_LO + nseq * stride


def _fill_shifted(xr_ref, pieces, seq_len, residues):
    nseq, stride, rows = _stencil_layout(seq_len)
    width = xr_ref.shape[-1]
    xr_ref[0, 0:PAD_LO, :] = jnp.zeros((PAD_LO, width), F32)
    for s in range(nseq):
        b0 = PAD_LO + s * stride
        xr_ref[0, b0:b0 + seq_len, :] = pieces[s]
        xr_ref[0, b0 + seq_len:b0 + stride, :] = jnp.zeros((PAD_SEQ, width), F32)
    x0 = xr_ref[0, 0:rows, :]
    for r in residues:
        if r:
            xr_ref[r, 0:rows, :] = pltpu.roll(x0, rows - r, axis=0)


def _tap(xr_ref, r0, d, rows):
    r = d % 8
    return xr_ref[r, pl.ds(r0 + (d - r), rows), :]


def _per_path(i, body):
    @pl.when(i < N_CTX_BLK)
    def _():
        body(SEQ)

    @pl.when(i >= N_CTX_BLK)
    def _():
        body(DEC_SEQ)


POOL_CHUNK = 64


def _pool_kernel(a_ref, w_ref, s_ref, o_ref, xr_ref, pooled_ref):
    def body(seq_len):
        nseq, stride, _ = _stencil_layout(seq_len)
        for g, win in enumerate(POOL_WINDOWS):
            left = win // 2
            right = win - 1 - left
            cols = slice(g * POOL_CH, (g + 1) * POOL_CH)
            offsets = range(-left, right + 1)
            _fill_shifted(xr_ref, [a_ref[s * seq_len:(s + 1) * seq_len, cols] for s in range(nseq)],
                          seq_len, sorted({d % 8 for d in offsets}))
            for s in range(nseq):
                def chunk(c, carry, s=s, offsets=offsets, left=left, right=right):
                    t0 = c * POOL_CHUNK
                    r0 = pl.multiple_of(PAD_LO + s * stride + t0, 8)
                    x = _tap(xr_ref, r0, 0, POOL_CHUNK)
                    acc = x
                    for d in offsets:
                        if d:
                            acc = acc + _tap(xr_ref, r0, d, POOL_CHUNK)
                    pos = t0 + lax.broadcasted_iota(jnp.int32, (POOL_CHUNK, 1), 0)
                    cnt = jnp.minimum(pos + right + 1, seq_len) - jnp.maximum(pos - left, 0)
                    pooled_ref[pl.ds(pl.multiple_of(s * seq_len + t0, 8), POOL_CHUNK), :] = (
                        acc / cnt.astype(F32) - x)
                    return carry
                lax.fori_loop(0, seq_len // POOL_CHUNK, chunk, 0)
            mixed = jnp.dot(pooled_ref[...].astype(BF16), w_ref[g].astype(BF16),
                            preferred_element_type=F32)
            o_ref[:, cols] = (mixed * s_ref[:, cols]).astype(BF16)

    _per_path(pl.program_id(0), body)


def _pool_mixer(proj, pool_w, pool_scale):
    return pl.pallas_call(
        _pool_kernel,
        grid=(N_ROW_BLK,),
        in_specs=[
            pl.BlockSpec((ROW_BLK, POOL_WIDTH), lambda i: (i, COL_POOL // POOL_WIDTH)),
            pl.BlockSpec((POOL_GROUPS, POOL_CH, POOL_CH), lambda i: (0, 0, 0)),
            pl.BlockSpec((1, POOL_WIDTH), lambda i: (0, 0)),
        ],
        out_specs=pl.BlockSpec((ROW_BLK, POOL_WIDTH), lambda i: (i, 0)),
        out_shape=jax.ShapeDtypeStruct((T_ALL, POOL_WIDTH), BF16),
        scratch_shapes=[pltpu.VMEM((8, STENCIL_ROWS, POOL_CH), F32),
                        pltpu.VMEM((ROW_BLK, POOL_CH), F32)],
        compiler_params=_cp(("arbitrary",)),
        name="pool_mixer",
    )(proj, pool_w, pool_scale.reshape(1, POOL_WIDTH))


CONV_CHUNK = 32


def _conv_kernel(a_ref, g_ref, dw_ref, db_ref, lg_ref, lb_ref, o_ref, xr_ref, y_ref):
    half = CONV_K // 2

    def body(seq_len):
        nseq, stride, _ = _stencil_layout(seq_len)
        pieces = []
        for s in range(nseq):
            rows = slice(s * seq_len, (s + 1) * seq_len)
            pieces.append(a_ref[rows, :] * jax.nn.sigmoid(g_ref[rows, :]))
        _fill_shifted(xr_ref, pieces, seq_len, range(8))
        for s in range(nseq):
            def chunk(c, carry, s=s):
                t0 = c * CONV_CHUNK
                r0 = pl.multiple_of(PAD_LO + s * stride + t0, 8)
                acc = jnp.broadcast_to(db_ref[...], (CONV_CHUNK, CONV_WIDTH))
                for k in range(CONV_K):
                    acc = acc + _tap(xr_ref, r0, k - half, CONV_CHUNK) * dw_ref[k:k + 1, :]
                y_ref[pl.ds(pl.multiple_of(s * seq_len + t0, CONV_CHUNK), CONV_CHUNK), :] = acc
                return carry
            lax.fori_loop(0, seq_len // CONV_CHUNK, chunk, 0, unroll=2)

    _per_path(pl.program_id(0), body)
    y = _layer_norm(y_ref[...]) * lg_ref[...] + lb_ref[...]
    o_ref[...] = (y * jax.nn.sigmoid(y)).astype(BF16)


def _conv_mixer(proj, dw, db, ln_g, ln_b):
    cb = COL_CV // CONV_WIDTH
    vec = pl.BlockSpec((1, CONV_WIDTH), lambda i: (0, 0))
    return pl.pallas_call(
        _conv_kernel,
        grid=(N_ROW_BLK,),
        in_specs=[
            pl.BlockSpec((ROW_BLK, CONV_WIDTH), lambda i: (i, cb)),
            pl.BlockSpec((ROW_BLK, CONV_WIDTH), lambda i: (i, cb + 1)),
            pl.BlockSpec((CONV_K, CONV_WIDTH), lambda i: (0, 0)),
            vec, vec, vec,
        ],
        out_specs=pl.BlockSpec((ROW_BLK, CONV_WIDTH), lambda i: (i, 0)),
        out_shape=jax.ShapeDtypeStruct((T_ALL, CONV_WIDTH), BF16),
        scratch_shapes=[pltpu.VMEM((8, STENCIL_ROWS, CONV_WIDTH), F32),
                        pltpu.VMEM((ROW_BLK, CONV_WIDTH), F32)],
        compiler_params=_cp(("arbitrary",)),
        name="conv_mixer",
    )(proj, proj, dw, db.reshape(1, -1), ln_g.reshape(1, -1), ln_b.reshape(1, -1))


def _s5_scan_kernel(u_ref, bq_ref, cq_ref, a_ref, h0_ref, y_ref, hf_ref, hs_ref, st_ref, *, tc, nb):
    d = pl.program_id(0)
    j = pl.program_id(1)

    @pl.when(j == 0)
    def _():
        st_ref[...] = h0_ref[0]

    rows = tc * nb
    u2 = u_ref[...].reshape(rows, S5_WIDTH).astype(BF16)
    for q in range(S5_Q):
        hs_ref[:, q * 2 * S5_QS:(q + 1) * 2 * S5_QS] = jnp.dot(
            u2[:, q * S5_QU:(q + 1) * S5_QU], bq_ref[0, q], preferred_element_type=F32)

    lane_w = 256
    for q in range(S5_Q):
        for w in range(S5_QS // lane_w):
            re0 = q * 2 * S5_QS + w * lane_w
            im0 = re0 + S5_QS
            a_re = jnp.broadcast_to(a_ref[0, :, re0:re0 + lane_w], (nb, lane_w))
            a_im = jnp.broadcast_to(a_ref[0, :, im0:im0 + lane_w], (nb, lane_w))

            def step(t, carry, re0=re0, im0=im0, a_re=a_re, a_im=a_im):
                h_re, h_im = carry
                te = jnp.where(d == 0, t, tc - 1 - t)
                r0 = pl.multiple_of(te * nb, nb)
                n_re = a_re * h_re - a_im * h_im + hs_ref[pl.ds(r0, nb), re0:re0 + lane_w]
                n_im = a_re * h_im + a_im * h_re + hs_ref[pl.ds(r0, nb), im0:im0 + lane_w]
                hs_ref[pl.ds(r0, nb), re0:re0 + lane_w] = n_re
                hs_ref[pl.ds(r0, nb), im0:im0 + lane_w] = n_im
                return n_re, n_im

            h_re, h_im = lax.fori_loop(
                0, tc, step, (st_ref[:, re0:re0 + lane_w], st_ref[:, im0:im0 + lane_w]), unroll=4)
            st_ref[:, re0:re0 + lane_w] = h_re
            st_ref[:, im0:im0 + lane_w] = h_im

    for q in range(S5_Q):
        yq = jnp.dot(hs_ref[:, q * 2 * S5_QS:(q + 1) * 2 * S5_QS].astype(BF16), cq_ref[0, q],
                     preferred_element_type=F32)
        y_ref[0, :, :, q * S5_QU:(q + 1) * S5_QU] = yq.reshape(tc, nb, S5_QU)

    @pl.when(j == pl.num_programs(1) - 1)
    def _():
        hf_ref[0] = st_ref[...]


def _s5_scan(u_tb, bq, cq, a_vec, h0, tc):
    n, nb, _ = u_tb.shape
    nc = n // tc

    def chunk(d, j):
        return j + d * (nc - 1 - 2 * j)

    return pl.pallas_call(
        functools.partial(_s5_scan_kernel, tc=tc, nb=nb),
        grid=(2, nc),
        in_specs=[
            pl.BlockSpec((tc, nb, S5_WIDTH), lambda d, j: (chunk(d, j), 0, 0)),
            pl.BlockSpec((1, S5_Q, S5_QU, 2 * S5_QS), lambda d, j: (d, 0, 0, 0)),
            pl.BlockSpec((1, S5_Q, 2 * S5_QS, S5_QU), lambda d, j: (d, 0, 0, 0)),
            pl.BlockSpec((1, 1, S5_LANES), lambda d, j: (d, 0, 0)),
            pl.BlockSpec((1, nb, S5_LANES), lambda d, j: (d, 0, 0)),
        ],
        out_specs=[
            pl.BlockSpec((1, tc, nb, S5_WIDTH), lambda d, j: (d, chunk(d, j), 0, 0)),
            pl.BlockSpec((1, nb, S5_LANES), lambda d, j: (d, 0, 0)),
        ],
        out_shape=[
            jax.ShapeDtypeStruct((2, n, nb, S5_WIDTH), F32),
            jax.ShapeDtypeStruct((2, nb, S5_LANES), F32),
        ],
        scratch_shapes=[
            pltpu.VMEM((tc * nb, S5_LANES), F32),
            pltpu.VMEM((nb, S5_LANES), F32),
        ],
        compiler_params=_cp(("arbitrary", "arbitrary")),
        name="s5_scan",
    )(u_tb, bq, cq, a_vec, h0)


def _s5_post_kernel(yf_ref, yb_ref, u_ref, d_ref, w_ref, o_ref):
    y = yf_ref[0] + yb_ref[0] + d_ref[...] * u_ref[...]
    y = jax.nn.gelu(y)
    gate = jnp.dot(y.astype(BF16), w_ref[...].astype(BF16), preferred_element_type=F32)
    o_ref[...] = (y * jax.nn.sigmoid(gate)).astype(BF16)


def _s5_post(y_dirs, u_tb, d_skip, w_glu):
    _, n, nb, _ = y_dirs.shape
    rows = n * nb
    tr = 1024
    y2 = y_dirs.reshape(2, rows, S5_WIDTH)
    return pl.pallas_call(
        _s5_post_kernel,
        grid=(rows // tr,),
        in_specs=[
            pl.BlockSpec((1, tr, S5_WIDTH), lambda i: (0, i, 0)),
            pl.BlockSpec((1, tr, S5_WIDTH), lambda i: (1, i, 0)),
            pl.BlockSpec((tr, S5_WIDTH), lambda i: (i, 0)),
            pl.BlockSpec((1, S5_WIDTH), lambda i: (0, 0)),
            pl.BlockSpec((S5_WIDTH, S5_WIDTH), lambda i: (0, 0)),
        ],
        out_specs=pl.BlockSpec((tr, S5_WIDTH), lambda i: (i, 0)),
        out_shape=jax.ShapeDtypeStruct((rows, S5_WIDTH), BF16),
        compiler_params=_cp(("arbitrary",)),
        name="s5_post",
    )(y2, y2, u_tb.reshape(rows, S5_WIDTH), d_skip.reshape(1, S5_WIDTH), w_glu)


def _time_to_token_major(y_tm, n, nb, n_batch):
    y = y_tm.reshape(n, nb, S5_WIDTH)[:, :n_batch]
    return y.transpose(1, 0, 2).reshape(n_batch * n, S5_WIDTH)


def _s5_params(lam_re, lam_im, log_dt, b_re, b_im, c_re, c_im):
    dt = jnp.exp(log_dt)[..., None]
    x = lam_re * dt
    y = lam_im * dt
    ex = jnp.exp(x)
    abar_re = ex * jnp.cos(y)
    abar_im = ex * jnp.sin(y)
    num_re = jnp.expm1(x) * jnp.cos(y) - 2.0 * jnp.square(jnp.sin(0.5 * y))
    num_im = abar_im
    den = lam_re * lam_re + lam_im * lam_im
    coef_re = (num_re * lam_re + num_im * lam_im) / den
    coef_im = (num_im * lam_re - num_re * lam_im) / den
    bbar_re = coef_re[..., None] * b_re - coef_im[..., None] * b_im
    bbar_im = coef_re[..., None] * b_im + coef_im[..., None] * b_re
    eye = jnp.eye(S5_GQ, dtype=F32)
    bb = jnp.stack([bbar_re, bbar_im]).reshape(2, 2, S5_Q, S5_GQ, S5_STATE, S5_CH)
    bq = jnp.einsum("rdqgpc,gh->dqgcrhp", bb, eye).reshape(2, S5_Q, S5_QU, 2 * S5_QS)
    cc = jnp.stack([c_re, -c_im]).reshape(2, 2, S5_Q, S5_GQ, S5_CH, S5_STATE)
    cq = jnp.einsum("rdqgcp,gh->dqrhpgc", cc, eye).reshape(2, S5_Q, 2 * S5_QS, S5_QU)
    ab = jnp.stack([abar_re, abar_im]).reshape(2, 2, S5_Q, S5_GQ, S5_STATE)
    a_vec = ab.transpose(1, 2, 0, 3, 4).reshape(2, 1, S5_LANES)
    return bq.astype(BF16), cq.astype(BF16), a_vec


def _s5_state_to_lanes(st):
    b = st.shape[0]
    s = st.reshape(b, 2, 2, S5_Q, S5_GQ, S5_STATE).transpose(1, 0, 3, 2, 4, 5)
    return s.reshape(2, b, S5_LANES)


def _s5_lanes_to_state(hl):
    b = hl.shape[1]
    s = hl.reshape(2, b, S5_Q, 2, S5_GQ, S5_STATE).transpose(1, 0, 3, 2, 4, 5)
    return s.reshape(b, 2, 2, S5_GROUPS, S5_STATE)


def _rope(x, cos, sin_signed):
    lane = lax.broadcasted_iota(jnp.int32, x.shape, 1)
    quarter = HEAD_DIM // 4
    partner = jnp.where((lane & quarter) == 0,
                        pltpu.roll(x, HEAD_DIM - quarter, axis=1),
                        pltpu.roll(x, quarter, axis=1))
    return x * cos + partner * sin_signed


def _nt_dot(a, b):
    return lax.dot_general(a, b, (((1,), (1,)), ((), ())), preferred_element_type=F32)


def _ctx_attn_kernel(q_ref, k_ref, v_ref, sink_ref, o_ref, ko_ref, vo_ref):
    h0 = pl.program_id(1) * REP
    k = k_ref[...]
    v = v_ref[...]
    ko_ref[...] = k
    vo_ref[...] = v
    kb = k.astype(BF16)
    vb = v.astype(BF16)
    for r in range(REP):
        qb = q_ref[:, r * HEAD_DIM:(r + 1) * HEAD_DIM].astype(BF16)
        s = _nt_dot(qb, kb) * (HEAD_DIM ** -0.5)
        sink = sink_ref[pl.ds(h0 + r, 1), 0:1]
        m = jnp.maximum(jnp.max(s, axis=-1, keepdims=True), sink)
        p = jnp.exp(s - m)
        denom = jnp.sum(p, axis=-1, keepdims=True) + jnp.exp(sink - m)
        o = jnp.dot(p.astype(BF16), vb, preferred_element_type=F32) / denom
        o_ref[:, r * HEAD_DIM:(r + 1) * HEAD_DIM] = o.astype(BF16)


def _ctx_attention(proj, sink_tile):
    qw = REP * HEAD_DIM
    kv_out = pl.BlockSpec((SEQ, HEAD_DIM), lambda b, h: (b, h))
    return pl.pallas_call(
        _ctx_attn_kernel,
        grid=(BATCH, N_KV_HEADS),
        in_specs=[
            pl.BlockSpec((SEQ, qw), lambda b, h: (b, COL_Q // qw + h)),
            pl.BlockSpec((SEQ, HEAD_DIM), lambda b, h: (b, COL_K // HEAD_DIM + h)),
            pl.BlockSpec((SEQ, HEAD_DIM), lambda b, h: (b, COL_V // HEAD_DIM + h)),
            pl.BlockSpec((N_HEADS, HEAD_DIM), lambda b, h: (0, 0)),
        ],
        out_specs=[pl.BlockSpec((SEQ, qw), lambda b, h: (b, h)), kv_out, kv_out],
        out_shape=[jax.ShapeDtypeStruct((T_CTX, ATTN_WIDTH), BF16),
                   jax.ShapeDtypeStruct((T_CTX, KV_WIDTH), F32),
                   jax.ShapeDtypeStruct((T_CTX, KV_WIDTH), F32)],
        compiler_params=_cp(("arbitrary", "arbitrary")),
        name="ctx_attention",
    )(proj, proj, proj, sink_tile)


def _lat_attn_kernel(q_ref, k_ref, v_ref, kc_ref, vc_ref, cq_ref, sq_ref, ck_ref, sk_ref,
                     sink_ref, o_ref, *, tq):
    h0 = pl.program_id(1) * REP
    q0 = pl.program_id(2) * tq
    kb = _rope(k_ref[...], ck_ref[...], sk_ref[...]).astype(BF16)
    vb = v_ref[...].astype(BF16)
    kcb = kc_ref[...].astype(BF16)
    vcb = vc_ref[...].astype(BF16)
    qpos = q0 + lax.broadcasted_iota(jnp.int32, (tq, DEC_SEQ), 0)
    kpos = lax.broadcasted_iota(jnp.int32, (tq, DEC_SEQ), 1)
    mask = jnp.abs(qpos - kpos) <= WINDOW
    scale = HEAD_DIM ** -0.5
    for r in range(REP):
        q = _rope(q_ref[:, r * HEAD_DIM:(r + 1) * HEAD_DIM], cq_ref[...], sq_ref[...]).astype(BF16)
        s_lat = jnp.where(mask, _nt_dot(q, kb) * scale, NEG_INF)
        s_ctx = _nt_dot(q, kcb) * scale
        sink = sink_ref[pl.ds(h0 + r, 1), 0:1]
        m = jnp.maximum(jnp.maximum(jnp.max(s_lat, axis=-1, keepdims=True),
                                    jnp.max(s_ctx, axis=-1, keepdims=True)), sink)
        p_lat = jnp.exp(s_lat - m)
        p_ctx = jnp.exp(s_ctx - m)
        denom = (jnp.sum(p_lat, axis=-1, keepdims=True) + jnp.sum(p_ctx, axis=-1, keepdims=True)
                 + jnp.exp(sink - m))
        o = (jnp.dot(p_lat.astype(BF16), vb, preferred_element_type=F32)
             + jnp.dot(p_ctx.astype(BF16), vcb, preferred_element_type=F32)) / denom
        o_ref[:, r * HEAD_DIM:(r + 1) * HEAD_DIM] = o.astype(BF16)


def _lat_attention(proj, cache_k_l, cache_v_l, rope_cos, rope_sin, sink_tile):
    tq = 256
    nq = DEC_SEQ // tq
    qw = REP * HEAD_DIM
    row0_q = T_CTX // tq
    row0_k = T_CTX // DEC_SEQ
    tab_q = pl.BlockSpec((tq, HEAD_DIM), lambda b, h, i: (i, 0))
    tab_k = pl.BlockSpec((DEC_SEQ, HEAD_DIM), lambda b, h, i: (0, 0))
    return pl.pallas_call(
        functools.partial(_lat_attn_kernel, tq=tq),
        grid=(DEC_BATCH, N_KV_HEADS, nq),
        in_specs=[
            pl.BlockSpec((tq, qw), lambda b, h, i: (row0_q + b * nq + i, COL_Q // qw + h)),
            pl.BlockSpec((DEC_SEQ, HEAD_DIM), lambda b, h, i: (row0_k + b, COL_K // HEAD_DIM + h)),
            pl.BlockSpec((DEC_SEQ, HEAD_DIM), lambda b, h, i: (row0_k + b, COL_V // HEAD_DIM + h)),
            pl.BlockSpec((None, PAST_LEN, HEAD_DIM), lambda b, h, i: (b, 0, h)),
            pl.BlockSpec((None, PAST_LEN, HEAD_DIM), lambda b, h, i: (b, 0, h)),
            tab_q, tab_q, tab_k, tab_k,
            pl.BlockSpec((N_HEADS, HEAD_DIM), lambda b, h, i: (0, 0)),
        ],
        out_specs=pl.BlockSpec((tq, qw), lambda b, h, i: (b * nq + i, h)),
        out_shape=jax.ShapeDtypeStruct((T_LAT, ATTN_WIDTH), BF16),
        compiler_params=_cp(("arbitrary", "arbitrary", "arbitrary")),
        name="lat_attention",
    )(proj, proj, proj, cache_k_l, cache_v_l, rope_cos, rope_sin, rope_cos, rope_sin, sink_tile)


def _rope_tables():
    t = jnp.arange(DEC_SEQ)
    row, col = t // GRID_W, t % GRID_W
    quarter = HEAD_DIM // 4
    inv = ROPE_BASE ** (-jnp.arange(quarter, dtype=F32) / quarter)
    lane = jnp.arange(HEAD_DIM)
    pos = jnp.where(lane[None, :] < HEAD_DIM // 2, row[:, None], col[:, None]).astype(F32)
    ang = pos * inv[lane % quarter][None, :]
    sign = jnp.where((lane & quarter) == 0, -1.0, 1.0).astype(F32)
    return jnp.cos(ang), jnp.sin(ang) * sign[None, :]


def _merge_kernel(yp_ref, ys_ref, yac_ref, yal_ref, yc_ref, g0_ref, g1_ref, g2_ref, g3_ref,
                  wp_ref, ws_ref, wa_ref, wc_ref, o_ref):
    attn = jnp.where(pl.program_id(1) < N_CTX_BLK, yac_ref[...], yal_ref[...])
    acc = None
    for y, g_ref, w_ref in ((yp_ref[...], g0_ref, wp_ref), (ys_ref[...], g1_ref, ws_ref),
                            (attn, g2_ref, wa_ref), (yc_ref[...], g3_ref, wc_ref)):
        t = g_ref[...].astype(F32) * jnp.dot(y, w_ref[...].astype(BF16), preferred_element_type=F32)
        acc = t if acc is None else acc + t
    o_ref[...] = acc.astype(BF16)


def _merge(gates, y_pool, y_s5, attn_ctx, attn_lat, y_conv, layer, w_pool, w_s5, w_attn, w_conv):
    tm, tn = ROW_BLK, 512
    per = D_MODEL // tn

    def act(width):
        return pl.BlockSpec((tm, width), lambda j, i: (i, 0))

    def gate(branch):
        return pl.BlockSpec((tm, tn), lambda j, i: (i, branch * per + j))

    def wt(width):
        return pl.BlockSpec((None, width, tn), lambda j, i: (layer, 0, j))

    attn_c = pl.BlockSpec((tm, ATTN_WIDTH), lambda j, i: (jnp.minimum(i, N_CTX_BLK - 1), 0))
    attn_l = pl.BlockSpec((tm, ATTN_WIDTH), lambda j, i: (jnp.maximum(i - N_CTX_BLK, 0), 0))
    return pl.pallas_call(
        _merge_kernel,
        grid=(D_MODEL // tn, T_ALL // tm),
        in_specs=[act(POOL_WIDTH), act(S5_WIDTH), attn_c, attn_l, act(CONV_WIDTH),
                  gate(0), gate(1), gate(2), gate(3),
                  wt(POOL_WIDTH), wt(S5_WIDTH), wt(ATTN_WIDTH), wt(CONV_WIDTH)],
        out_specs=pl.BlockSpec((tm, tn), lambda j, i: (i, j)),
        out_shape=jax.ShapeDtypeStruct((T_ALL, D_MODEL), BF16),
        compiler_params=_cp(("arbitrary", "arbitrary")),
        name="merge",
    )(y_pool, y_s5, attn_ctx, attn_lat, y_conv, gates, gates, gates, gates,
      w_pool, w_s5, w_attn, w_conv)


FFN_TILE = 256
N_FFN_TILE = D_FF // FFN_TILE


def _ffn_kernel(h_ref, wg_ref, wv_ref, dwg_ref, dwv_ref, dbg_ref, dbv_ref, wd_ref, o_ref,
                ug_ref, uv_ref):
    i = pl.program_id(0)
    f = pl.program_id(1)

    def up(slot):
        h = h_ref[...]
        ug_ref[slot] = jnp.dot(h, wg_ref[...].astype(BF16), preferred_element_type=F32)
        uv_ref[slot] = jnp.dot(h, wv_ref[...].astype(BF16), preferred_element_type=F32)

    def down(slot):
        pos, seq_len = _seq_pos(i, ROW_BLK)
        first = pos == 0
        last = pos == seq_len - 1

        def conv3(u, dw_ref, db_ref):
            prev = jnp.where(first, 0.0, _shift_rows(u, -1))
            nxt = jnp.where(last, 0.0, _shift_rows(u, 1))
            return prev * dw_ref[0:1, :] + u * dw_ref[1:2, :] + nxt * dw_ref[2:3, :] + db_ref[...]

        gt = conv3(ug_ref[slot], dwg_ref, dbg_ref)
        val = conv3(uv_ref[slot], dwv_ref, dbv_ref)
        act = (gt * jax.nn.sigmoid(gt) * val).astype(BF16)
        o_ref[...] += jnp.dot(act, wd_ref[...].astype(BF16), preferred_element_type=F32)

    @pl.when(f == 0)
    def _():
        o_ref[...] = jnp.zeros_like(o_ref)
        up(0)

    for par in (0, 1):
        @pl.when(jnp.logical_and(jnp.logical_and(f > 0, f < N_FFN_TILE), f % 2 == par))
        def _(par=par):
            up(par)
            down(1 - par)

    @pl.when(f == N_FFN_TILE)
    def _():
        down((N_FFN_TILE - 1) % 2)


def _conv_ffn(h, layer, w_up, dw, db, w_down):
    tf, nf = FFN_TILE, N_FFN_TILE
    db3 = db.reshape(DEPTH, 1, 2 * D_FF)

    def cur(f):
        return jnp.minimum(f, nf - 1)

    def prv(f):
        return jnp.maximum(f - 1, 0)

    return pl.pallas_call(
        _ffn_kernel,
        grid=(N_ROW_BLK, nf + 1),
        in_specs=[
            pl.BlockSpec((ROW_BLK, D_MODEL), lambda i, f: (i, 0)),
            pl.BlockSpec((None, D_MODEL, tf), lambda i, f: (layer, 0, cur(f))),
            pl.BlockSpec((None, D_MODEL, tf), lambda i, f: (layer, 0, nf + cur(f))),
            pl.BlockSpec((None, FFN_CONV_K, tf), lambda i, f: (layer, 0, prv(f))),
            pl.BlockSpec((None, FFN_CONV_K, tf), lambda i, f: (layer, 0, nf + prv(f))),
            pl.BlockSpec((None, 1, tf), lambda i, f: (layer, 0, prv(f))),
            pl.BlockSpec((None, 1, tf), lambda i, f: (layer, 0, nf + prv(f))),
            pl.BlockSpec((None, tf, D_MODEL), lambda i, f: (layer, prv(f), 0)),
        ],
        out_specs=pl.BlockSpec((ROW_BLK, D_MODEL), lambda i, f: (i, 0)),
        out_shape=jax.ShapeDtypeStruct((T_ALL, D_MODEL), F32),
        scratch_shapes=[pltpu.VMEM((2, ROW_BLK, tf), F32), pltpu.VMEM((2, ROW_BLK, tf), F32)],
        compiler_params=_cp(("arbitrary", "arbitrary")),
        name="conv_ffn",
    )(h, w_up, w_up, dw, dw, db3, db3, w_down)


def kernel(x_prompt, x_sample, cache_k, cache_v, state_s5, c, c_ctx, w_ada, b_ada, w_in, pool_w, pool_scale, s5_lambda_re, s5_lambda_im, s5_log_dt, s5_b_re, s5_b_im, s5_c_re, s5_c_im, s5_d, s5_w_glu, attn_sink, conv_dw, conv_db, conv_ln_g, conv_ln_b, w_br_pool, w_br_s5, w_br_attn, w_br_conv, w_out, ln1_g, ln1_b, ffn_w_up, ffn_dw, ffn_db, ffn_w_down, ln2_g, ln2_b):
    x_parts = (x_prompt.reshape(T_CTX, D_MODEL), x_sample.reshape(T_LAT, D_MODEL))
    cvec = jnp.concatenate(
        [c_ctx[None, :], c, jnp.zeros((MOD_ROWS - 1 - DEC_BATCH, D_MODEL), F32)], axis=0)
    mod = _mod_table(cvec, w_ada, b_ada).reshape(DEPTH, MOD_ROWS, 6, D_MODEL)
    rope_cos, rope_sin = _rope_tables()

    ks_out, vs_out, ss_out = [], [], []
    h = _ln_mod(x_parts[0], x_parts[1], mod[0], 0, 1)
    for l in range(DEPTH):
        proj = _matmul(h, w_in, l, 0, MIX_COLS)
        gates = _matmul(h, w_in, l, COL_GATE, N_BRANCH * D_MODEL, out_dtype=BF16, gate=True)

        y_pool = _pool_mixer(proj, pool_w[l], pool_scale[l])
        y_conv = _conv_mixer(proj, conv_dw[l], conv_db[l], conv_ln_g[l], conv_ln_b[l])

        bq, cq, a_vec = _s5_params(s5_lambda_re[l], s5_lambda_im[l], s5_log_dt[l], s5_b_re[l],
                                   s5_b_im[l], s5_c_re[l], s5_c_im[l])
        u = proj[:, COL_S5:COL_S5 + S5_WIDTH]
        u_ctx = u[:T_CTX].reshape(BATCH, SEQ, S5_WIDTH).transpose(1, 0, 2)
        u_lat = u[T_CTX:].reshape(DEC_BATCH, DEC_SEQ, S5_WIDTH).transpose(1, 0, 2)
        u_lat = jnp.pad(u_lat, ((0, 0), (0, S5_LAT_B - DEC_BATCH), (0, 0)))
        h0_ctx = jnp.zeros((2, BATCH, S5_LANES), F32)
        h0_lat = jnp.pad(_s5_state_to_lanes(state_s5[:, l]), ((0, 0), (0, S5_LAT_B - DEC_BATCH), (0, 0)))
        y_ctx_dirs, hf_ctx = _s5_scan(u_ctx, bq, cq, a_vec, h0_ctx, tc=32)
        y_lat_dirs, _ = _s5_scan(u_lat, bq, cq, a_vec, h0_lat, tc=64)
        y_s5 = jnp.concatenate([
            _time_to_token_major(_s5_post(y_ctx_dirs, u_ctx, s5_d[l], s5_w_glu[l]),
                                 SEQ, BATCH, BATCH),
            _time_to_token_major(_s5_post(y_lat_dirs, u_lat, s5_d[l], s5_w_glu[l]),
                                 DEC_SEQ, S5_LAT_B, DEC_BATCH)], axis=0)

        sink_tile = jnp.broadcast_to(attn_sink[l][:, None], (N_HEADS, HEAD_DIM))
        attn_ctx, k_new, v_new = _ctx_attention(proj, sink_tile)
        attn_lat = _lat_attention(proj, cache_k[:, l].reshape(DEC_BATCH, PAST_LEN, KV_WIDTH),
                                  cache_v[:, l].reshape(DEC_BATCH, PAST_LEN, KV_WIDTH),
                                  rope_cos, rope_sin, sink_tile)

        merged = _merge(gates, y_pool, y_s5, attn_ctx, attn_lat, y_conv, l,
                        w_br_pool, w_br_s5, w_br_attn, w_br_conv)
        mixed = _matmul(merged, w_out, l)
        x, h2 = _res_ln(x_parts, mixed, mod[l], 2, ln1_g[l], ln1_b[l], mod[l], 3, 4)
        f = _conv_ffn(h2, l, ffn_w_up, ffn_dw, ffn_db, ffn_w_down)
        if l + 1 < DEPTH:
            x, h = _res_ln((x,), f, mod[l], 5, ln2_g[l], ln2_b[l], mod[l + 1], 0, 1)
            x_parts = (x,)
        else:
            y_prompt, y_sample = _res_ln((x,), f, mod[l], 5, ln2_g[l], ln2_b[l])

        ks_out.append(k_new.reshape(BATCH, SEQ, N_KV_HEADS, HEAD_DIM))
        vs_out.append(v_new.reshape(BATCH, SEQ, N_KV_HEADS, HEAD_DIM))
        ss_out.append(_s5_lanes_to_state(hf_ctx))

    return (y_prompt.reshape(BATCH, SEQ, D_MODEL), y_sample.reshape(DEC_BATCH, DEC_SEQ, D_MODEL),
            jnp.stack(ks_out, axis=1), jnp.stack(vs_out, axis=1), jnp.stack(ss_out, axis=1))
```

```python
import functools

import jax
import jax.numpy as jnp
from jax import lax
from jax.experimental import pallas as pl
from jax.experimental.pallas import tpu as pltpu

F32 = jnp.float32
BF16 = jnp.bfloat16

D_MODEL = 2048
BATCH = 16
SEQ = 256
DEPTH = 2
DEC_BATCH = 4
DEC_SEQ = 1024
PAST_LEN = 512
GRID_W = 64
POOL_WIDTH = 512
POOL_GROUPS = 4
POOL_CH = POOL_WIDTH // POOL_GROUPS
POOL_WINDOWS = (2, 4, 8, 16)
S5_WIDTH = 512
S5_CH = 16
S5_GROUPS = S5_WIDTH // S5_CH
S5_STATE = 64
N_HEADS = 8
N_KV_HEADS = 2
HEAD_DIM = 128
REP = N_HEADS // N_KV_HEADS
ATTN_WIDTH = N_HEADS * HEAD_DIM
KV_WIDTH = N_KV_HEADS * HEAD_DIM
WINDOW = 128
ROPE_BASE = 10000.0
NEG_INF = -1e30
CONV_WIDTH = 512
CONV_K = 31
N_BRANCH = 4
D_FF = 5632
FFN_CONV_K = 3
ALPHA = (2.0 * DEPTH) ** 0.25
LN_EPS = 1e-5
MIX_COLS = POOL_WIDTH + S5_WIDTH + ATTN_WIDTH + 2 * KV_WIDTH + 2 * CONV_WIDTH
N_IN = MIX_COLS + N_BRANCH * D_MODEL

COL_POOL = 0
COL_S5 = POOL_WIDTH
COL_Q = COL_S5 + S5_WIDTH
COL_K = COL_Q + ATTN_WIDTH
COL_V = COL_K + KV_WIDTH
COL_CV = COL_V + KV_WIDTH
COL_GATE = MIX_COLS

T_CTX = BATCH * SEQ
T_LAT = DEC_BATCH * DEC_SEQ
T_ALL = T_CTX + T_LAT
ROW_BLK = 1024
N_ROW_BLK = T_ALL // ROW_BLK
N_CTX_BLK = T_CTX // ROW_BLK
MOD_ROWS = 8
LN_ROWS = 512

S5_Q = 4
S5_GQ = S5_GROUPS // S5_Q
S5_QU = S5_GQ * S5_CH
S5_QS = S5_GQ * S5_STATE
S5_LANES = 2 * S5_GROUPS * S5_STATE
S5_LAT_B = 8

VMEM_LIMIT = 56 * 1024 * 1024


def _cp(sem):
    return pltpu.CompilerParams(dimension_semantics=sem, vmem_limit_bytes=VMEM_LIMIT)


def _mod_row(row_block, rows_per_block):
    lat_batch = (row_block * rows_per_block) // DEC_SEQ - T_CTX // DEC_SEQ
    return jnp.maximum(lat_batch + 1, 0)


def _seq_pos(row_block, rows):
    r = lax.broadcasted_iota(jnp.int32, (rows, 1), 0)
    seq_len = jnp.where(row_block < N_CTX_BLK, SEQ, DEC_SEQ)
    return r & (seq_len - 1), seq_len


def _layer_norm(x):
    mu = jnp.mean(x, axis=-1, keepdims=True)
    xc = x - mu
    var = jnp.mean(xc * xc, axis=-1, keepdims=True)
    return xc * lax.rsqrt(var + LN_EPS)


def _shift_rows(x, d):
    n = x.shape[0]
    return pltpu.roll(x, (-d) % n, axis=0)


def _split_bf16(x):
    hi = x.astype(BF16)
    return hi, (x - hi.astype(F32)).astype(BF16)


def _mod_kernel(c_ref, w_ref, b_ref, o_ref):
    c = c_ref[...]
    a_hi, a_lo = _split_bf16(c * jax.nn.sigmoid(c))
    w_hi, w_lo = _split_bf16(w_ref[0])
    acc = jnp.dot(a_hi, w_hi, preferred_element_type=F32)
    acc = acc + jnp.dot(a_lo, w_hi, preferred_element_type=F32)
    acc = acc + jnp.dot(a_hi, w_lo, preferred_element_type=F32)
    o_ref[0] = acc + b_ref[0]


def _mod_table(cvec, w_ada, b_ada):
    tn = 1024
    n6 = 6 * D_MODEL
    return pl.pallas_call(
        _mod_kernel,
        grid=(DEPTH, n6 // tn),
        in_specs=[
            pl.BlockSpec((MOD_ROWS, D_MODEL), lambda l, j: (0, 0)),
            pl.BlockSpec((1, D_MODEL, tn), lambda l, j: (l, 0, j)),
            pl.BlockSpec((1, 1, tn), lambda l, j: (l, 0, j)),
        ],
        out_specs=pl.BlockSpec((1, MOD_ROWS, tn), lambda l, j: (l, 0, j)),
        out_shape=jax.ShapeDtypeStruct((DEPTH, MOD_ROWS, n6), F32),
        compiler_params=_cp(("arbitrary", "arbitrary")),
        name="mod_table",
    )(cvec, w_ada, b_ada.reshape(DEPTH, 1, n6))


N_CTX_LN = T_CTX // LN_ROWS


def _row_spec():
    return pl.BlockSpec((LN_ROWS, D_MODEL), lambda i: (i, 0))


def _ctx_part_spec():
    return pl.BlockSpec((LN_ROWS, D_MODEL), lambda i: (jnp.minimum(i, N_CTX_LN - 1), 0))


def _lat_part_spec():
    return pl.BlockSpec((LN_ROWS, D_MODEL), lambda i: (jnp.maximum(i - N_CTX_LN, 0), 0))


def _mod_spec():
    return pl.BlockSpec((1, 6, D_MODEL), lambda i: (_mod_row(i, LN_ROWS), 0, 0))


def _read_rows(refs):
    if len(refs) == 1:
        return refs[0][...]
    return jnp.where(pl.program_id(0) < N_CTX_LN, refs[0][...], refs[1][...])


def _ln_mod_kernel(xc_ref, xl_ref, mod_ref, h_ref, *, shift_i, scale_i):
    y = _layer_norm(_read_rows((xc_ref, xl_ref)))
    scale = mod_ref[0, scale_i:scale_i + 1, :]
    shift = mod_ref[0, shift_i:shift_i + 1, :]
    h_ref[...] = (y * (1.0 + scale) + shift).astype(BF16)


def _ln_mod(x_ctx, x_lat, mod_l, shift_i, scale_i):
    return pl.pallas_call(
        functools.partial(_ln_mod_kernel, shift_i=shift_i, scale_i=scale_i),
        grid=(T_ALL // LN_ROWS,),
        in_specs=[_ctx_part_spec(), _lat_part_spec(), _mod_spec()],
        out_specs=_row_spec(),
        out_shape=jax.ShapeDtypeStruct((T_ALL, D_MODEL), BF16),
        compiler_params=_cp(("arbitrary",)),
        name="ln_mod",
    )(x_ctx, x_lat, mod_l)


def _res_ln_kernel(*refs, n_x, gate_i, next_shift_i, next_scale_i, split_out):
    x_refs = refs[:n_x]
    y_ref, mod_ref, g_ref, b_ref = refs[n_x:n_x + 4]
    rest = refs[n_x + 4:]
    gate = mod_ref[0, gate_i:gate_i + 1, :]
    z = ALPHA * _read_rows(x_refs) + gate * y_ref[...]
    xn = _layer_norm(z) * g_ref[...] + b_ref[...]
    if split_out:
        xc_ref, xl_ref = rest

        @pl.when(pl.program_id(0) < N_CTX_LN)
        def _():
            xc_ref[...] = xn

        @pl.when(pl.program_id(0) >= N_CTX_LN)
        def _():
            xl_ref[...] = xn
    else:
        nmod_ref, xo_ref, h_ref = rest
        xo_ref[...] = xn
        scale = nmod_ref[0, next_scale_i:next_scale_i + 1, :]
        shift = nmod_ref[0, next_shift_i:next_shift_i + 1, :]
        h_ref[...] = (_layer_norm(xn) * (1.0 + scale) + shift).astype(BF16)


def _res_ln(x_parts, y, mod_l, gate_i, g, b, next_mod=None, next_shift_i=None, next_scale_i=None):
    vec = pl.BlockSpec((1, D_MODEL), lambda i: (0, 0))
    x_specs = [_row_spec()] if len(x_parts) == 1 else [_ctx_part_spec(), _lat_part_spec()]
    in_specs = x_specs + [_row_spec(), _mod_spec(), vec, vec]
    args = list(x_parts) + [y, mod_l, g.reshape(1, D_MODEL), b.reshape(1, D_MODEL)]
    split_out = next_mod is None
    if split_out:
        out_specs = [_ctx_part_spec(), _lat_part_spec()]
        out_shape = [jax.ShapeDtypeStruct((T_CTX, D_MODEL), F32),
                     jax.ShapeDtypeStruct((T_LAT, D_MODEL), F32)]
    else:
        in_specs.append(_mod_spec())
        args.append(next_mod)
        out_specs = [_row_spec(), _row_spec()]
        out_shape = [jax.ShapeDtypeStruct((T_ALL, D_MODEL), F32),
                     jax.ShapeDtypeStruct((T_ALL, D_MODEL), BF16)]
    return pl.pallas_call(
        functools.partial(_res_ln_kernel, n_x=len(x_parts), gate_i=gate_i,
                          next_shift_i=next_shift_i, next_scale_i=next_scale_i,
                          split_out=split_out),
        grid=(T_ALL // LN_ROWS,),
        in_specs=in_specs,
        out_specs=out_specs,
        out_shape=out_shape,
        compiler_params=_cp(("arbitrary",)),
        name="res_ln",
    )(*args)


def _matmul_kernel(a_ref, w_ref, o_ref, *, gate):
    y = jnp.dot(a_ref[...], w_ref[...].astype(BF16), preferred_element_type=F32)
    if gate:
        y = jax.nn.sigmoid(y)
    o_ref[...] = y.astype(o_ref.dtype)


def _matmul(a, w, layer, col0=0, ncols=None, out_dtype=F32, gate=False):
    tm, tn = 2048, 512
    m, k = a.shape
    ncols = w.shape[2] - col0 if ncols is None else ncols
    assert col0 % tn == 0 and ncols % tn == 0 and m % tm == 0
    jb = col0 // tn
    return pl.pallas_call(
        functools.partial(_matmul_kernel, gate=gate),
        grid=(m // tm, ncols // tn),
        in_specs=[
            pl.BlockSpec((tm, k), lambda i, j: (i, 0)),
            pl.BlockSpec((None, k, tn), lambda i, j: (layer, 0, jb + j)),
        ],
        out_specs=pl.BlockSpec((tm, tn), lambda i, j: (i, j)),
        out_shape=jax.ShapeDtypeStruct((m, ncols), out_dtype),
        compiler_params=_cp(("arbitrary", "arbitrary")),
        name="matmul",
    )(a, w)


PAD_LO = 16
PAD_SEQ = 32
STENCIL_ROWS = PAD_LO + (ROW_BLK // SEQ) * (SEQ + PAD_SEQ)


def _stencil_layout(seq_len):
    nseq = ROW_BLK // seq_len
    stride = seq_len + PAD_SEQ
    return nseq, stride, PAD_LO + nseq * stride


def _fill_shifted(xr_ref, pieces, seq_len, residues):
    nseq, stride, rows = _stencil_layout(seq_len)
    width = xr_ref.shape[-1]
    xr_ref[0, 0:PAD_LO, :] = jnp.zeros((PAD_LO, width), F32)
    for s in range(nseq):
        b0 = PAD_LO + s * stride
        xr_ref[0, b0:b0 + seq_len, :] = pieces[s]
        xr_ref[0, b0 + seq_len:b0 + stride, :] = jnp.zeros((PAD_SEQ, width), F32)
    x0 = xr_ref[0, 0:rows, :]
    for r in residues:
        if r:
            xr_ref[r, 0:rows, :] = pltpu.roll(x0, rows - r, axis=0)


def _tap(xr_ref, r0, d, rows):
    r = d % 8
    return xr_ref[r, pl.ds(r0 + (d - r), rows), :]


def _per_path(i, body):
    @pl.when(i < N_CTX_BLK)
    def _():
        body(SEQ)

    @pl.when(i >= N_CTX_BLK)
    def _():
        body(DEC_SEQ)


POOL_CHUNK = 64


def _pool_kernel(a_ref, w_ref, s_ref, o_ref, xr_ref, pooled_ref):
    def body(seq_len):
        nseq, stride, _ = _stencil_layout(seq_len)
        for g, win in enumerate(POOL_WINDOWS):
            left = win // 2
            right = win - 1 - left
            cols = slice(g * POOL_CH, (g + 1) * POOL_CH)
            offsets = range(-left, right + 1)
            _fill_shifted(xr_ref, [a_ref[s * seq_len:(s + 1) * seq_len, cols] for s in range(nseq)],
                          seq_len, sorted({d % 8 for d in offsets}))
            for s in range(nseq):
                def chunk(c, carry, s=s, offsets=offsets, left=left, right=right):
                    t0 = c * POOL_CHUNK
                    r0 = pl.multiple_of(PAD_LO + s * stride + t0, 8)
                    x = _tap(xr_ref, r0, 0, POOL_CHUNK)
                    acc = x
                    for d in offsets:
                        if d:
                            acc = acc + _tap(xr_ref, r0, d, POOL_CHUNK)
                    pos = t0 + lax.broadcasted_iota(jnp.int32, (POOL_CHUNK, 1), 0)
                    cnt = jnp.minimum(pos + right + 1, seq_len) - jnp.maximum(pos - left, 0)
                    pooled_ref[pl.ds(pl.multiple_of(s * seq_len + t0, 8), POOL_CHUNK), :] = (
                        acc / cnt.astype(F32) - x)
                    return carry
                lax.fori_loop(0, seq_len // POOL_CHUNK, chunk, 0)
            mixed = jnp.dot(pooled_ref[...].astype(BF16), w_ref[g].astype(BF16),
                            preferred_element_type=F32)
            o_ref[:, cols] = (mixed * s_ref[:, cols]).astype(BF16)

    _per_path(pl.program_id(0), body)


def _pool_mixer(proj, pool_w, pool_scale):
    return pl.pallas_call(
        _pool_kernel,
        grid=(N_ROW_BLK,),
        in_specs=[
            pl.BlockSpec((ROW_BLK, POOL_WIDTH), lambda i: (i, COL_POOL // POOL_WIDTH)),
            pl.BlockSpec((POOL_GROUPS, POOL_CH, POOL_CH), lambda i: (0, 0, 0)),
            pl.BlockSpec((1, POOL_WIDTH), lambda i: (0, 0)),
        ],
        out_specs=pl.BlockSpec((ROW_BLK, POOL_WIDTH), lambda i: (i, 0)),
        out_shape=jax.ShapeDtypeStruct((T_ALL, POOL_WIDTH), BF16),
        scratch_shapes=[pltpu.VMEM((8, STENCIL_ROWS, POOL_CH), F32),
                        pltpu.VMEM((ROW_BLK, POOL_CH), F32)],
        compiler_params=_cp(("arbitrary",)),
        name="pool_mixer",
    )(proj, pool_w, pool_scale.reshape(1, POOL_WIDTH))


CONV_CHUNK = 32


def _conv_kernel(a_ref, g_ref, dw_ref, db_ref, lg_ref, lb_ref, o_ref, xr_ref, y_ref):
    half = CONV_K // 2

    def body(seq_len):
        nseq, stride, _ = _stencil_layout(seq_len)
        pieces = []
        for s in range(nseq):
            rows = slice(s * seq_len, (s + 1) * seq_len)
            pieces.append(a_ref[rows, :] * jax.nn.sigmoid(g_ref[rows, :]))
        _fill_shifted(xr_ref, pieces, seq_len, range(8))
        for s in range(nseq):
            def chunk(c, carry, s=s):
                t0 = c * CONV_CHUNK
                r0 = pl.multiple_of(PAD_LO + s * stride + t0, 8)
                acc = jnp.broadcast_to(db_ref[...], (CONV_CHUNK, CONV_WIDTH))
                for k in range(CONV_K):
                    acc = acc + _tap(xr_ref, r0, k - half, CONV_CHUNK) * dw_ref[k:k + 1, :]
                y_ref[pl.ds(pl.multiple_of(s * seq_len + t0, CONV_CHUNK), CONV_CHUNK), :] = acc
                return carry
            lax.fori_loop(0, seq_len // CONV_CHUNK, chunk, 0, unroll=2)

    _per_path(pl.program_id(0), body)
    y = _layer_norm(y_ref[...]) * lg_ref[...] + lb_ref[...]
    o_ref[...] = (y * jax.nn.sigmoid(y)).astype(BF16)


def _conv_mixer(proj, dw, db, ln_g, ln_b):
    cb = COL_CV // CONV_WIDTH
    vec = pl.BlockSpec((1, CONV_WIDTH), lambda i: (0, 0))
    return pl.pallas_call(
        _conv_kernel,
        grid=(N_ROW_BLK,),
        in_specs=[
            pl.BlockSpec((ROW_BLK, CONV_WIDTH), lambda i: (i, cb)),
            pl.BlockSpec((ROW_BLK, CONV_WIDTH), lambda i: (i, cb + 1)),
            pl.BlockSpec((CONV_K, CONV_WIDTH), lambda i: (0, 0)),
            vec, vec, vec,
        ],
        out_specs=pl.BlockSpec((ROW_BLK, CONV_WIDTH), lambda i: (i, 0)),
        out_shape=jax.ShapeDtypeStruct((T_ALL, CONV_WIDTH), BF16),
        scratch_shapes=[pltpu.VMEM((8, STENCIL_ROWS, CONV_WIDTH), F32),
                        pltpu.VMEM((ROW_BLK, CONV_WIDTH), F32)],
        compiler_params=_cp(("arbitrary",)),
        name="conv_mixer",
    )(proj, proj, dw, db.reshape(1, -1), ln_g.reshape(1, -1), ln_b.reshape(1, -1))


S5_ROWS = 512


def _s5_kernel(*refs, tc, nb, n_u, reverse, post):
    u_refs = refs[:n_u]
    bq_ref, cq_ref, a_ref, h0_ref = refs[n_u:n_u + 4]
    k = n_u + 4
    if post:
        yf_ref, dskip_ref, wglu_ref = refs[k:k + 3]
        k += 3
    y_ref, hf_ref = refs[k:k + 2]
    u_scr, y_scr, hs_ref, st_ref = refs[k + 2:]
    j = pl.program_id(0)
    n_real = u_refs[0].shape[0] if n_u == 1 else n_u

    @pl.when(j == 0)
    def _():
        st_ref[...] = h0_ref[...]

    if n_u == 1:
        u_bt = u_refs[0][...]
    else:
        u_bt = jnp.stack([r[...] for r in u_refs], axis=0)
    u_scr[:, 0:n_real, :] = pltpu.einshape("btc->tbc", u_bt)
    if n_real < nb:
        u_scr[:, n_real:nb, :] = jnp.zeros((tc, nb - n_real, S5_WIDTH), F32)

    rows = tc * nb
    u_tm = u_scr[...].reshape(rows, S5_WIDTH)
    u2 = u_tm.astype(BF16)
    for q in range(S5_Q):
        hs_ref[:, q * 2 * S5_QS:(q + 1) * 2 * S5_QS] = jnp.dot(
            u2[:, q * S5_QU:(q + 1) * S5_QU], bq_ref[q], preferred_element_type=F32)

    lane_w = 4096 // nb
    for q in range(S5_Q):
        for w in range(S5_QS // lane_w):
            re0 = q * 2 * S5_QS + w * lane_w
            im0 = re0 + S5_QS
            a_re = jnp.broadcast_to(a_ref[:, re0:re0 + lane_w], (nb, lane_w))
            a_im = jnp.broadcast_to(a_ref[:, im0:im0 + lane_w], (nb, lane_w))

            def step(t, carry, re0=re0, im0=im0, a_re=a_re, a_im=a_im):
                h_re, h_im = carry
                te = tc - 1 - t if reverse else t
                r0 = pl.multiple_of(te * nb, nb)
                n_re = a_re * h_re - a_im * h_im + hs_ref[pl.ds(r0, nb), re0:re0 + lane_w]
                n_im = a_re * h_im + a_im * h_re + hs_ref[pl.ds(r0, nb), im0:im0 + lane_w]
                hs_ref[pl.ds(r0, nb), re0:re0 + lane_w] = n_re
                hs_ref[pl.ds(r0, nb), im0:im0 + lane_w] = n_im
                return n_re, n_im

            h_re, h_im = lax.fori_loop(
                0, tc, step, (st_ref[:, re0:re0 + lane_w], st_ref[:, im0:im0 + lane_w]), unroll=4)
            st_ref[:, re0:re0 + lane_w] = h_re
            st_ref[:, im0:im0 + lane_w] = h_im

    for q in range(S5_Q):
        yq = jnp.dot(hs_ref[:, q * 2 * S5_QS:(q + 1) * 2 * S5_QS].astype(BF16), cq_ref[q],
                     preferred_element_type=F32)
        y_scr[:, :, q * S5_QU:(q + 1) * S5_QU] = yq.reshape(tc, nb, S5_QU)

    if post:
        y_scr[...] = y_scr[...] + (dskip_ref[...] * u_tm).reshape(tc, nb, S5_WIDTH)
    y_bt = pltpu.einshape("tbc->btc", y_scr[:, 0:n_real, :])
    if post:
        y = jax.nn.gelu(y_bt + yf_ref[...]).reshape(n_real * tc, S5_WIDTH)
        gate = jnp.dot(y.astype(BF16), wglu_ref[...].astype(BF16), preferred_element_type=F32)
        y_ref[...] = (y * jax.nn.sigmoid(gate)).astype(BF16).reshape(n_real, tc, S5_WIDTH)
    else:
        y_ref[...] = y_bt

    @pl.when(j == pl.num_programs(0) - 1)
    def _():
        hf_ref[...] = st_ref[...]


def _s5_pass(proj3, path, direction, bq, cq, a_vec, h0, yf=None, d_skip=None, w_glu=None):
    post = yf is not None
    reverse = direction == 1
    cb = COL_S5 // S5_WIDTH
    if path == "ctx":
        n_real, nb, n = BATCH, BATCH, SEQ
    else:
        n_real, nb, n = DEC_BATCH, S5_LAT_B, DEC_SEQ
    tc = S5_ROWS // nb
    nc = n // tc

    def chunk(j):
        return nc - 1 - j if reverse else j

    if path == "ctx":
        u_specs = [pl.BlockSpec((BATCH, tc, S5_WIDTH), lambda j: (0, chunk(j), cb))]
    else:
        per = SEQ // tc
        u_specs = [
            pl.BlockSpec((None, tc, S5_WIDTH),
                         lambda j, b=b: (T_CTX // SEQ + b * (DEC_SEQ // SEQ) + chunk(j) // per,
                                         chunk(j) % per, cb))
            for b in range(DEC_BATCH)]
    seq_blk = pl.BlockSpec((n_real, tc, S5_WIDTH), lambda j: (0, chunk(j), 0))
    in_specs = u_specs + [
        pl.BlockSpec((None, S5_Q, S5_QU, 2 * S5_QS), lambda j: (direction, 0, 0, 0)),
        pl.BlockSpec((None, S5_Q, 2 * S5_QS, S5_QU), lambda j: (direction, 0, 0, 0)),
        pl.BlockSpec((None, 1, S5_LANES), lambda j: (direction, 0, 0)),
        pl.BlockSpec((None, nb, S5_LANES), lambda j: (direction, 0, 0)),
    ]
    args = [proj3] * len(u_specs) + [bq, cq, a_vec, h0]
    if post:
        in_specs += [seq_blk, pl.BlockSpec((1, S5_WIDTH), lambda j: (0, 0)),
                     pl.BlockSpec((S5_WIDTH, S5_WIDTH), lambda j: (0, 0))]
        args += [yf, d_skip.reshape(1, S5_WIDTH), w_glu]
    return pl.pallas_call(
        functools.partial(_s5_kernel, tc=tc, nb=nb, n_u=len(u_specs), reverse=reverse, post=post),
        grid=(nc,),
        in_specs=in_specs,
        out_specs=[seq_blk, pl.BlockSpec((nb, S5_LANES), lambda j: (0, 0))],
        out_shape=[jax.ShapeDtypeStruct((n_real, n, S5_WIDTH), BF16 if post else F32),
                   jax.ShapeDtypeStruct((nb, S5_LANES), F32)],
        scratch_shapes=[
            pltpu.VMEM((tc, nb, S5_WIDTH), F32),
            pltpu.VMEM((tc, nb, S5_WIDTH), F32),
            pltpu.VMEM((tc * nb, S5_LANES), F32),
            pltpu.VMEM((nb, S5_LANES), F32),
        ],
        compiler_params=_cp(("arbitrary",)),
        name="s5_pass",
    )(*args)


def _s5_mixer(proj3, path, bq, cq, a_vec, h0, d_skip, w_glu):
    yf, hf_f = _s5_pass(proj3, path, 0, bq, cq, a_vec, h0)
    y, hf_b = _s5_pass(proj3, path, 1, bq, cq, a_vec, h0, yf, d_skip, w_glu)
    return y.reshape(-1, S5_WIDTH), jnp.stack([hf_f, hf_b])


def _s5_params(lam_re, lam_im, log_dt, b_re, b_im, c_re, c_im):
    dt = jnp.exp(log_dt)[..., None]
    x = lam_re * dt
    y = lam_im * dt
    ex = jnp.exp(x)
    abar_re = ex * jnp.cos(y)
    abar_im = ex * jnp.sin(y)
    num_re = jnp.expm1(x) * jnp.cos(y) - 2.0 * jnp.square(jnp.sin(0.5 * y))
    num_im = abar_im
    den = lam_re * lam_re + lam_im * lam_im
    coef_re = (num_re * lam_re + num_im * lam_im) / den
    coef_im = (num_im * lam_re - num_re * lam_im) / den
    bbar_re = coef_re[..., None] * b_re - coef_im[..., None] * b_im
    bbar_im = coef_re[..., None] * b_im + coef_im[..., None] * b_re
    eye = jnp.eye(S5_GQ, dtype=F32)
    bb = jnp.stack([bbar_re, bbar_im]).reshape(2, 2, S5_Q, S5_GQ, S5_STATE, S5_CH)
    bq = jnp.einsum("rdqgpc,gh->dqgcrhp", bb, eye).reshape(2, S5_Q, S5_QU, 2 * S5_QS)
    cc = jnp.stack([c_re, -c_im]).reshape(2, 2, S5_Q, S5_GQ, S5_CH, S5_STATE)
    cq = jnp.einsum("rdqgcp,gh->dqrhpgc", cc, eye).reshape(2, S5_Q, 2 * S5_QS, S5_QU)
    ab = jnp.stack([abar_re, abar_im]).reshape(2, 2, S5_Q, S5_GQ, S5_STATE)
    a_vec = ab.transpose(1, 2, 0, 3, 4).reshape(2, 1, S5_LANES)
    return bq.astype(BF16), cq.astype(BF16), a_vec


def _s5_state_to_lanes(st):
    b = st.shape[0]
    s = st.reshape(b, 2, 2, S5_Q, S5_GQ, S5_STATE).transpose(1, 0, 3, 2, 4, 5)
    return s.reshape(2, b, S5_LANES)


def _s5_lanes_to_state(hl):
    b = hl.shape[1]
    s = hl.reshape(2, b, S5_Q, 2, S5_GQ, S5_STATE).transpose(1, 0, 3, 2, 4, 5)
    return s.reshape(b, 2, 2, S5_GROUPS, S5_STATE)


def _rope(x, cos, sin_signed):
    lane = lax.broadcasted_iota(jnp.int32, x.shape, 1)
    quarter = HEAD_DIM // 4
    partner = jnp.where((lane & quarter) == 0,
                        pltpu.roll(x, HEAD_DIM - quarter, axis=1),
                        pltpu.roll(x, quarter, axis=1))
    return x * cos + partner * sin_signed


def _nt_dot(a, b):
    return lax.dot_general(a, b, (((1,), (1,)), ((), ())), preferred_element_type=F32)


def _stack_heads(q_ref, rope=None):
    parts = []
    for r in range(REP):
        q = q_ref[:, r * HEAD_DIM:(r + 1) * HEAD_DIM]
        if rope is not None:
            q = _rope(q, *rope)
        parts.append(q.astype(BF16))
    return jnp.concatenate(parts, axis=0)


def _with_ones(v):
    return jnp.concatenate([v.astype(BF16), jnp.ones(v.shape, BF16)], axis=1)


def _sink_column(sink_ref, h0, rows):
    return jnp.concatenate(
        [jnp.broadcast_to(sink_ref[pl.ds(h0 + r, 1), 0:1], (rows, 1)) for r in range(REP)], axis=0)


def _ctx_attn_kernel(q_ref, k_ref, v_ref, sink_ref, o_ref, ko_ref, vo_ref):
    h0 = pl.program_id(1) * REP
    k = k_ref[...]
    v = v_ref[...]
    ko_ref[...] = k
    vo_ref[...] = v
    q = _stack_heads(q_ref)
    s = _nt_dot(q, k.astype(BF16)) * (HEAD_DIM ** -0.5)
    sink = _sink_column(sink_ref, h0, SEQ)
    m = jnp.maximum(jnp.max(s, axis=-1, keepdims=True), sink)
    p = jnp.exp(s - m)
    acc = jnp.dot(p.astype(BF16), _with_ones(v), preferred_element_type=F32)
    o = acc[:, :HEAD_DIM] / (acc[:, HEAD_DIM:HEAD_DIM + 1] + jnp.exp(sink - m))
    for r in range(REP):
        o_ref[:, r * HEAD_DIM:(r + 1) * HEAD_DIM] = o[r * SEQ:(r + 1) * SEQ].astype(BF16)


def _ctx_attention(proj, sink_tile):
    qw = REP * HEAD_DIM
    kv_out = pl.BlockSpec((SEQ, HEAD_DIM), lambda b, h: (b, h))
    return pl.pallas_call(
        _ctx_attn_kernel,
        grid=(BATCH, N_KV_HEADS),
        in_specs=[
            pl.BlockSpec((SEQ, qw), lambda b, h: (b, COL_Q // qw + h)),
            pl.BlockSpec((SEQ, HEAD_DIM), lambda b, h: (b, COL_K // HEAD_DIM + h)),
            pl.BlockSpec((SEQ, HEAD_DIM), lambda b, h: (b, COL_V // HEAD_DIM + h)),
            pl.BlockSpec((N_HEADS, HEAD_DIM), lambda b, h: (0, 0)),
        ],
        out_specs=[pl.BlockSpec((SEQ, qw), lambda b, h: (b, h)), kv_out, kv_out],
        out_shape=[jax.ShapeDtypeStruct((T_CTX, ATTN_WIDTH), BF16),
                   jax.ShapeDtypeStruct((T_CTX, KV_WIDTH), F32),
                   jax.ShapeDtypeStruct((T_CTX, KV_WIDTH), F32)],
        compiler_params=_cp(("arbitrary", "arbitrary")),
        name="ctx_attention",
    )(proj, proj, proj, sink_tile)


LAT_TQ = 256
LAT_KWIN = LAT_TQ + 2 * WINDOW


def _lat_attn_kernel(q_ref, k_ref, v_ref, kc_ref, vc_ref, cq_ref, sq_ref, ck_ref, sk_ref,
                     sink_ref, o_ref):
    h0 = pl.program_id(1) * REP
    q0 = pl.program_id(2) * LAT_TQ
    k0 = pl.multiple_of(jnp.clip(q0 - WINDOW, 0, DEC_SEQ - LAT_KWIN), WINDOW)
    win = pl.ds(k0, LAT_KWIN)
    kb = _rope(k_ref[win, :], ck_ref[win, :], sk_ref[win, :]).astype(BF16)
    rows = REP * LAT_TQ
    row = lax.broadcasted_iota(jnp.int32, (rows, LAT_KWIN), 0)
    qpos = q0 + (row & (LAT_TQ - 1))
    kpos = k0 + lax.broadcasted_iota(jnp.int32, (rows, LAT_KWIN), 1)
    mask = jnp.abs(qpos - kpos) <= WINDOW
    scale = HEAD_DIM ** -0.5
    q = _stack_heads(q_ref, (cq_ref[...], sq_ref[...]))
    s_lat = jnp.where(mask, _nt_dot(q, kb) * scale, NEG_INF)
    s_ctx = _nt_dot(q, kc_ref[...].astype(BF16)) * scale
    sink = _sink_column(sink_ref, h0, LAT_TQ)
    m = jnp.maximum(jnp.maximum(jnp.max(s_lat, axis=-1, keepdims=True),
                                jnp.max(s_ctx, axis=-1, keepdims=True)), sink)
    p_lat = jnp.exp(s_lat - m).astype(BF16)
    p_ctx = jnp.exp(s_ctx - m).astype(BF16)
    acc = (jnp.dot(p_lat, _with_ones(v_ref[win, :]), preferred_element_type=F32)
           + jnp.dot(p_ctx, _with_ones(vc_ref[...]), preferred_element_type=F32))
    o = acc[:, :HEAD_DIM] / (acc[:, HEAD_DIM:HEAD_DIM + 1] + jnp.exp(sink - m))
    for r in range(REP):
        o_ref[:, r * HEAD_DIM:(r + 1) * HEAD_DIM] = o[r * LAT_TQ:(r + 1) * LAT_TQ].astype(BF16)


def _lat_attention(proj, cache_k_l, cache_v_l, rope_cos, rope_sin, sink_tile):
    tq = LAT_TQ
    nq = DEC_SEQ // tq
    qw = REP * HEAD_DIM
    row0_q = T_CTX // tq
    row0_k = T_CTX // DEC_SEQ
    tab_q = pl.BlockSpec((tq, HEAD_DIM), lambda b, h, i: (i, 0))
    tab_k = pl.BlockSpec((DEC_SEQ, HEAD_DIM), lambda b, h, i: (0, 0))
    return pl.pallas_call(
        _lat_attn_kernel,
        grid=(DEC_BATCH, N_KV_HEADS, nq),
        in_specs=[
            pl.BlockSpec((tq, qw), lambda b, h, i: (row0_q + b * nq + i, COL_Q // qw + h)),
            pl.BlockSpec((DEC_SEQ, HEAD_DIM), lambda b, h, i: (row0_k + b, COL_K // HEAD_DIM + h)),
            pl.BlockSpec((DEC_SEQ, HEAD_DIM), lambda b, h, i: (row0_k + b, COL_V // HEAD_DIM + h)),
            pl.BlockSpec((None, PAST_LEN, HEAD_DIM), lambda b, h, i: (b, 0, h)),
            pl.BlockSpec((None, PAST_LEN, HEAD_DIM), lambda b, h, i: (b, 0, h)),
            tab_q, tab_q, tab_k, tab_k,
            pl.BlockSpec((N_HEADS, HEAD_DIM), lambda b, h, i: (0, 0)),
        ],
        out_specs=pl.BlockSpec((tq, qw), lambda b, h, i: (b * nq + i, h)),
        out_shape=jax.ShapeDtypeStruct((T_LAT, ATTN_WIDTH), BF16),
        compiler_params=_cp(("arbitrary", "arbitrary", "arbitrary")),
        name="lat_attention",
    )(proj, proj, proj, cache_k_l, cache_v_l, rope_cos, rope_sin, rope_cos, rope_sin, sink_tile)


def _rope_tables():
    t = jnp.arange(DEC_SEQ)
    row, col = t // GRID_W, t % GRID_W
    quarter = HEAD_DIM // 4
    inv = ROPE_BASE ** (-jnp.arange(quarter, dtype=F32) / quarter)
    lane = jnp.arange(HEAD_DIM)
    pos = jnp.where(lane[None, :] < HEAD_DIM // 2, row[:, None], col[:, None]).astype(F32)
    ang = pos * inv[lane % quarter][None, :]
    sign = jnp.where((lane & quarter) == 0, -1.0, 1.0).astype(F32)
    return jnp.cos(ang), jnp.sin(ang) * sign[None, :]


def _merge_kernel(yp_ref, ysc_ref, ysl_ref, yac_ref, yal_ref, yc_ref, g0_ref, g1_ref, g2_ref,
                  g3_ref, wp_ref, ws_ref, wa_ref, wc_ref, o_ref):
    is_ctx = pl.program_id(1) < N_CTX_BLK
    y_s5 = jnp.where(is_ctx, ysc_ref[...], ysl_ref[...])
    attn = jnp.where(is_ctx, yac_ref[...], yal_ref[...])
    acc = None
    for y, g_ref, w_ref in ((yp_ref[...], g0_ref, wp_ref), (y_s5, g1_ref, ws_ref),
                            (attn, g2_ref, wa_ref), (yc_ref[...], g3_ref, wc_ref)):
        t = g_ref[...].astype(F32) * jnp.dot(y, w_ref[...].astype(BF16), preferred_element_type=F32)
        acc = t if acc is None else acc + t
    o_ref[...] = acc.astype(BF16)


def _merge(gates, y_pool, y_s5_ctx, y_s5_lat, attn_ctx, attn_lat, y_conv, layer,
           w_pool, w_s5, w_attn, w_conv):
    tm, tn = ROW_BLK, 512
    per = D_MODEL // tn

    def act(width):
        return pl.BlockSpec((tm, width), lambda j, i: (i, 0))

    def ctx_part(width):
        return pl.BlockSpec((tm, width), lambda j, i: (jnp.minimum(i, N_CTX_BLK - 1), 0))

    def lat_part(width):
        return pl.BlockSpec((tm, width), lambda j, i: (jnp.maximum(i - N_CTX_BLK, 0), 0))

    def gate(branch):
        return pl.BlockSpec((tm, tn), lambda j, i: (i, branch * per + j))

    def wt(width):
        return pl.BlockSpec((None, width, tn), lambda j, i: (layer, 0, j))

    return pl.pallas_call(
        _merge_kernel,
        grid=(D_MODEL // tn, T_ALL // tm),
        in_specs=[act(POOL_WIDTH), ctx_part(S5_WIDTH), lat_part(S5_WIDTH),
                  ctx_part(ATTN_WIDTH), lat_part(ATTN_WIDTH), act(CONV_WIDTH),
                  gate(0), gate(1), gate(2), gate(3),
                  wt(POOL_WIDTH), wt(S5_WIDTH), wt(ATTN_WIDTH), wt(CONV_WIDTH)],
        out_specs=pl.BlockSpec((tm, tn), lambda j, i: (i, j)),
        out_shape=jax.ShapeDtypeStruct((T_ALL, D_MODEL), BF16),
        compiler_params=_cp(("arbitrary", "arbitrary")),
        name="merge",
    )(y_pool, y_s5_ctx, y_s5_lat, attn_ctx, attn_lat, y_conv, gates, gates, gates, gates,
      w_pool, w_s5, w_attn, w_conv)


FFN_TILE = 256
N_FFN_TILE = D_FF // FFN_TILE


def _ffn_kernel(h_ref, wg_ref, wv_ref, dwg_ref, dwv_ref, dbg_ref, dbv_ref, wd_ref, o_ref,
                ug_ref, uv_ref):
    i = pl.program_id(0)
    f = pl.program_id(1)

    def up(slot):
        h = h_ref[...]
        ug_ref[slot] = jnp.dot(h, wg_ref[...].astype(BF16), preferred_element_type=F32)
        uv_ref[slot] = jnp.dot(h, wv_ref[...].astype(BF16), preferred_element_type=F32)

    def down(slot):
        pos, seq_len = _seq_pos(i, ROW_BLK)
        first = pos == 0
        last = pos == seq_len - 1

        def conv3(u, dw_ref, db_ref):
            prev = jnp.where(first, 0.0, _shift_rows(u, -1))
            nxt = jnp.where(last, 0.0, _shift_rows(u, 1))
            return prev * dw_ref[0:1, :] + u * dw_ref[1:2, :] + nxt * dw_ref[2:3, :] + db_ref[...]

        gt = conv3(ug_ref[slot], dwg_ref, dbg_ref)
        val = conv3(uv_ref[slot], dwv_ref, dbv_ref)
        act = (gt * jax.nn.sigmoid(gt) * val).astype(BF16)
        o_ref[...] += jnp.dot(act, wd_ref[...].astype(BF16), preferred_element_type=F32)

    @pl.when(f == 0)
    def _():
        o_ref[...] = jnp.zeros_like(o_ref)
        up(0)

    for par in (0, 1):
        @pl.when(jnp.logical_and(jnp.logical_and(f > 0, f < N_FFN_TILE), f % 2 == par))
        def _(par=par):
            up(par)
            down(1 - par)

    @pl.when(f == N_FFN_TILE)
    def _():
        down((N_FFN_TILE - 1) % 2)


def _conv_ffn(h, layer, w_up, dw, db, w_down):
    tf, nf = FFN_TILE, N_FFN_TILE
    db3 = db.reshape(DEPTH, 1, 2 * D_FF)

    def cur(f):
        return jnp.minimum(f, nf - 1)

    def prv(f):
        return jnp.maximum(f - 1, 0)

    return pl.pallas_call(
        _ffn_kernel,
        grid=(N_ROW_BLK, nf + 1),
        in_specs=[
            pl.BlockSpec((ROW_BLK, D_MODEL), lambda i, f: (i, 0)),
            pl.BlockSpec((None, D_MODEL, tf), lambda i, f: (layer, 0, cur(f))),
            pl.BlockSpec((None, D_MODEL, tf), lambda i, f: (layer, 0, nf + cur(f))),
            pl.BlockSpec((None, FFN_CONV_K, tf), lambda i, f: (layer, 0, prv(f))),
            pl.BlockSpec((None, FFN_CONV_K, tf), lambda i, f: (layer, 0, nf + prv(f))),
            pl.BlockSpec((None, 1, tf), lambda i, f: (layer, 0, prv(f))),
            pl.BlockSpec((None, 1, tf), lambda i, f: (layer, 0, nf + prv(f))),
            pl.BlockSpec((None, tf, D_MODEL), lambda i, f: (layer, prv(f), 0)),
        ],
        out_specs=pl.BlockSpec((ROW_BLK, D_MODEL), lambda i, f: (i, 0)),
        out_shape=jax.ShapeDtypeStruct((T_ALL, D_MODEL), F32),
        scratch_shapes=[pltpu.VMEM((2, ROW_BLK, tf), F32), pltpu.VMEM((2, ROW_BLK, tf), F32)],
        compiler_params=_cp(("arbitrary", "arbitrary")),
        name="conv_ffn",
    )(h, w_up, w_up, dw, dw, db3, db3, w_down)


def kernel(x_prompt, x_sample, cache_k, cache_v, state_s5, c, c_ctx, w_ada, b_ada, w_in, pool_w, pool_scale, s5_lambda_re, s5_lambda_im, s5_log_dt, s5_b_re, s5_b_im, s5_c_re, s5_c_im, s5_d, s5_w_glu, attn_sink, conv_dw, conv_db, conv_ln_g, conv_ln_b, w_br_pool, w_br_s5, w_br_attn, w_br_conv, w_out, ln1_g, ln1_b, ffn_w_up, ffn_dw, ffn_db, ffn_w_down, ln2_g, ln2_b):
    x_parts = (x_prompt.reshape(T_CTX, D_MODEL), x_sample.reshape(T_LAT, D_MODEL))
    cvec = jnp.concatenate(
        [c_ctx[None, :], c, jnp.zeros((MOD_ROWS - 1 - DEC_BATCH, D_MODEL), F32)], axis=0)
    mod = _mod_table(cvec, w_ada, b_ada).reshape(DEPTH, MOD_ROWS, 6, D_MODEL)
    rope_cos, rope_sin = _rope_tables()

    ks_out, vs_out, ss_out = [], [], []
    h = _ln_mod(x_parts[0], x_parts[1], mod[0], 0, 1)
    for l in range(DEPTH):
        proj = _matmul(h, w_in, l, 0, MIX_COLS)
        gates = _matmul(h, w_in, l, COL_GATE, N_BRANCH * D_MODEL, out_dtype=BF16, gate=True)

        y_pool = _pool_mixer(proj, pool_w[l], pool_scale[l])
        y_conv = _conv_mixer(proj, conv_dw[l], conv_db[l], conv_ln_g[l], conv_ln_b[l])

        bq, cq, a_vec = _s5_params(s5_lambda_re[l], s5_lambda_im[l], s5_log_dt[l], s5_b_re[l],
                                   s5_b_im[l], s5_c_re[l], s5_c_im[l])
        proj3 = proj.reshape(T_ALL // SEQ, SEQ, MIX_COLS)
        h0_ctx = jnp.zeros((2, BATCH, S5_LANES), F32)
        h0_lat = jnp.pad(_s5_state_to_lanes(state_s5[:, l]), ((0, 0), (0, S5_LAT_B - DEC_BATCH), (0, 0)))
        y_s5_ctx, hf_ctx = _s5_mixer(proj3, "ctx", bq, cq, a_vec, h0_ctx, s5_d[l], s5_w_glu[l])
        y_s5_lat, _ = _s5_mixer(proj3, "lat", bq, cq, a_vec, h0_lat, s5_d[l], s5_w_glu[l])

        sink_tile = jnp.broadcast_to(attn_sink[l][:, None], (N_HEADS, HEAD_DIM))
        attn_ctx, k_new, v_new = _ctx_attention(proj, sink_tile)
        attn_lat = _lat_attention(proj, cache_k[:, l].reshape(DEC_BATCH, PAST_LEN, KV_WIDTH),
                                  cache_v[:, l].reshape(DEC_BATCH, PAST_LEN, KV_WIDTH),
                                  rope_cos, rope_sin, sink_tile)

        merged = _merge(gates, y_pool, y_s5_ctx, y_s5_lat, attn_ctx, attn_lat, y_conv, l,
                        w_br_pool, w_br_s5, w_br_attn, w_br_conv)
        mixed = _matmul(merged, w_out, l)
        x, h2 = _res_ln(x_parts, mixed, mod[l], 2, ln1_g[l], ln1_b[l], mod[l], 3, 4)
        f = _conv_ffn(h2, l, ffn_w_up, ffn_dw, ffn_db, ffn_w_down)
        if l + 1 < DEPTH:
            x, h = _res_ln((x,), f, mod[l], 5, ln2_g[l], ln2_b[l], mod[l + 1], 0, 1)
            x_parts = (x,)
        else:
            y_prompt, y_sample = _res_ln((x,), f, mod[l], 5, ln2_g[l], ln2_b[l])

        ks_out.append(k_new.reshape(BATCH, SEQ, N_KV_HEADS, HEAD_DIM))
        vs_out.append(v_new.reshape(BATCH, SEQ, N_KV_HEADS, HEAD_DIM))
        ss_out.append(_s5_lanes_to_state(hf_ctx))

    return (y_prompt.reshape(BATCH, SEQ, D_MODEL), y_sample.reshape(DEC_BATCH, DEC_SEQ, D_MODEL),
            jnp.stack(ks_out, axis=1), jnp.stack(vs_out, axis=1), jnp.stack(ss_out, axis=1))
```

```python
import functools

import jax
import jax.numpy as jnp
from jax import lax
from jax.experimental import pallas as pl
from jax.experimental.pallas import tpu as pltpu

F32 = jnp.float32
BF16 = jnp.bfloat16

D_MODEL = 2048
BATCH = 16
SEQ = 256
DEPTH = 2
DEC_BATCH = 4
DEC_SEQ = 1024
PAST_LEN = 512
GRID_W = 64
POOL_WIDTH = 512
POOL_GROUPS = 4
POOL_CH = POOL_WIDTH // POOL_GROUPS
POOL_WINDOWS = (2, 4, 8, 16)
S5_WIDTH = 512
S5_CH = 16
S5_GROUPS = S5_WIDTH // S5_CH
S5_STATE = 64
N_HEADS = 8
N_KV_HEADS = 2
HEAD_DIM = 128
REP = N_HEADS // N_KV_HEADS
ATTN_WIDTH = N_HEADS * HEAD_DIM
KV_WIDTH = N_KV_HEADS * HEAD_DIM
WINDOW = 128
ROPE_BASE = 10000.0
NEG_INF = -1e30
CONV_WIDTH = 512
CONV_K = 31
N_BRANCH = 4
D_FF = 5632
FFN_CONV_K = 3
ALPHA = (2.0 * DEPTH) ** 0.25
LN_EPS = 1e-5
MIX_COLS = POOL_WIDTH + S5_WIDTH + ATTN_WIDTH + 2 * KV_WIDTH + 2 * CONV_WIDTH
N_IN = MIX_COLS + N_BRANCH * D_MODEL

COL_POOL = 0
COL_S5 = POOL_WIDTH
COL_Q = COL_S5 + S5_WIDTH
COL_K = COL_Q + ATTN_WIDTH
COL_V = COL_K + KV_WIDTH
COL_CV = COL_V + KV_WIDTH
COL_GATE = MIX_COLS

T_CTX = BATCH * SEQ
T_LAT = DEC_BATCH * DEC_SEQ
T_ALL = T_CTX + T_LAT
ROW_BLK = 1024
N_ROW_BLK = T_ALL // ROW_BLK
N_CTX_BLK = T_CTX // ROW_BLK
MOD_ROWS = 8
LN_ROWS = 512

S5_Q = 4
S5_GQ = S5_GROUPS // S5_Q
S5_QU = S5_GQ * S5_CH
S5_QS = S5_GQ * S5_STATE
S5_LANES = 2 * S5_GROUPS * S5_STATE
S5_LAT_B = 8

VMEM_LIMIT = 56 * 1024 * 1024


def _cp(sem):
    return pltpu.CompilerParams(dimension_semantics=sem, vmem_limit_bytes=VMEM_LIMIT)


def _mod_row(row_block, rows_per_block):
    lat_batch = (row_block * rows_per_block) // DEC_SEQ - T_CTX // DEC_SEQ
    return jnp.maximum(lat_batch + 1, 0)


def _seq_pos(row_block, rows):
    r = lax.broadcasted_iota(jnp.int32, (rows, 1), 0)
    seq_len = jnp.where(row_block < N_CTX_BLK, SEQ, DEC_SEQ)
    return r & (seq_len - 1), seq_len


def _layer_norm(x):
    mu = jnp.mean(x, axis=-1, keepdims=True)
    xc = x - mu
    var = jnp.mean(xc * xc, axis=-1, keepdims=True)
    return xc * lax.rsqrt(var + LN_EPS)


def _shift_rows(x, d):
    n = x.shape[0]
    return pltpu.roll(x, (-d) % n, axis=0)


def _split_bf16(x):
    hi = x.astype(BF16)
    return hi, (x - hi.astype(F32)).astype(BF16)


def _mod_kernel(c_ref, w_ref, b_ref, o_ref):
    c = c_ref[...]
    a_hi, a_lo = _split_bf16(c * jax.nn.sigmoid(c))
    w_hi, w_lo = _split_bf16(w_ref[0])
    acc = jnp.dot(a_hi, w_hi, preferred_element_type=F32)
    acc = acc + jnp.dot(a_lo, w_hi, preferred_element_type=F32)
    acc = acc + jnp.dot(a_hi, w_lo, preferred_element_type=F32)
    o_ref[0] = acc + b_ref[0]


def _mod_table(cvec, w_ada, b_ada):
    tn = 1024
    n6 = 6 * D_MODEL
    return pl.pallas_call(
        _mod_kernel,
        grid=(DEPTH, n6 // tn),
        in_specs=[
            pl.BlockSpec((MOD_ROWS, D_MODEL), lambda l, j: (0, 0)),
            pl.BlockSpec((1, D_MODEL, tn), lambda l, j: (l, 0, j)),
            pl.BlockSpec((1, 1, tn), lambda l, j: (l, 0, j)),
        ],
        out_specs=pl.BlockSpec((1, MOD_ROWS, tn), lambda l, j: (l, 0, j)),
        out_shape=jax.ShapeDtypeStruct((DEPTH, MOD_ROWS, n6), F32),
        compiler_params=_cp(("arbitrary", "arbitrary")),
        name="mod_table",
    )(cvec, w_ada, b_ada.reshape(DEPTH, 1, n6))


N_CTX_LN = T_CTX // LN_ROWS


def _row_spec():
    return pl.BlockSpec((LN_ROWS, D_MODEL), lambda i: (i, 0))


def _ctx_part_spec():
    return pl.BlockSpec((LN_ROWS, D_MODEL), lambda i: (jnp.minimum(i, N_CTX_LN - 1), 0))


def _lat_part_spec():
    return pl.BlockSpec((LN_ROWS, D_MODEL), lambda i: (jnp.maximum(i - N_CTX_LN, 0), 0))


def _mod_spec():
    return pl.BlockSpec((1, 6, D_MODEL), lambda i: (_mod_row(i, LN_ROWS), 0, 0))


def _read_rows(refs):
    if len(refs) == 1:
        return refs[0][...]
    return jnp.where(pl.program_id(0) < N_CTX_LN, refs[0][...], refs[1][...])


def _ln_mod_kernel(xc_ref, xl_ref, mod_ref, h_ref, *, shift_i, scale_i):
    y = _layer_norm(_read_rows((xc_ref, xl_ref)))
    scale = mod_ref[0, scale_i:scale_i + 1, :]
    shift = mod_ref[0, shift_i:shift_i + 1, :]
    h_ref[...] = (y * (1.0 + scale) + shift).astype(BF16)


def _ln_mod(x_ctx, x_lat, mod_l, shift_i, scale_i):
    return pl.pallas_call(
        functools.partial(_ln_mod_kernel, shift_i=shift_i, scale_i=scale_i),
        grid=(T_ALL // LN_ROWS,),
        in_specs=[_ctx_part_spec(), _lat_part_spec(), _mod_spec()],
        out_specs=_row_spec(),
        out_shape=jax.ShapeDtypeStruct((T_ALL, D_MODEL), BF16),
        compiler_params=_cp(("arbitrary",)),
        name="ln_mod",
    )(x_ctx, x_lat, mod_l)


def _res_ln_kernel(*refs, n_x, gate_i, next_shift_i, next_scale_i, split_out):
    x_refs = refs[:n_x]
    y_ref, mod_ref, g_ref, b_ref = refs[n_x:n_x + 4]
    rest = refs[n_x + 4:]
    gate = mod_ref[0, gate_i:gate_i + 1, :]
    z = ALPHA * _read_rows(x_refs) + gate * y_ref[...].astype(F32)
    xn = _layer_norm(z) * g_ref[...] + b_ref[...]
    if split_out:
        xc_ref, xl_ref = rest

        @pl.when(pl.program_id(0) < N_CTX_LN)
        def _():
            xc_ref[...] = xn

        @pl.when(pl.program_id(0) >= N_CTX_LN)
        def _():
            xl_ref[...] = xn
    else:
        nmod_ref, xo_ref, h_ref = rest
        xo_ref[...] = xn
        scale = nmod_ref[0, next_scale_i:next_scale_i + 1, :]
        shift = nmod_ref[0, next_shift_i:next_shift_i + 1, :]
        h_ref[...] = (_layer_norm(xn) * (1.0 + scale) + shift).astype(BF16)


def _res_ln(x_parts, y, mod_l, gate_i, g, b, next_mod=None, next_shift_i=None, next_scale_i=None):
    vec = pl.BlockSpec((1, D_MODEL), lambda i: (0, 0))
    x_specs = [_row_spec()] if len(x_parts) == 1 else [_ctx_part_spec(), _lat_part_spec()]
    in_specs = x_specs + [_row_spec(), _mod_spec(), vec, vec]
    args = list(x_parts) + [y, mod_l, g.reshape(1, D_MODEL), b.reshape(1, D_MODEL)]
    split_out = next_mod is None
    if split_out:
        out_specs = [_ctx_part_spec(), _lat_part_spec()]
        out_shape = [jax.ShapeDtypeStruct((T_CTX, D_MODEL), F32),
                     jax.ShapeDtypeStruct((T_LAT, D_MODEL), F32)]
    else:
        in_specs.append(_mod_spec())
        args.append(next_mod)
        out_specs = [_row_spec(), _row_spec()]
        out_shape = [jax.ShapeDtypeStruct((T_ALL, D_MODEL), F32),
                     jax.ShapeDtypeStruct((T_ALL, D_MODEL), BF16)]
    return pl.pallas_call(
        functools.partial(_res_ln_kernel, n_x=len(x_parts), gate_i=gate_i,
                          next_shift_i=next_shift_i, next_scale_i=next_scale_i,
                          split_out=split_out),
        grid=(T_ALL // LN_ROWS,),
        in_specs=in_specs,
        out_specs=out_specs,
        out_shape=out_shape,
        compiler_params=_cp(("arbitrary",)),
        name="res_ln",
    )(*args)


def _matmul_kernel(a_ref, w_ref, o_ref, *, gate):
    y = jnp.dot(a_ref[...], w_ref[...].astype(BF16), preferred_element_type=F32)
    if gate:
        y = jax.nn.sigmoid(y)
    o_ref[...] = y.astype(o_ref.dtype)


def _matmul(a, w, layer, col0=0, ncols=None, out_dtype=F32, gate=False):
    tm, tn = 2048, 512
    m, k = a.shape
    ncols = w.shape[2] - col0 if ncols is None else ncols
    assert col0 % tn == 0 and ncols % tn == 0 and m % tm == 0
    jb = col0 // tn
    return pl.pallas_call(
        functools.partial(_matmul_kernel, gate=gate),
        grid=(m // tm, ncols // tn),
        in_specs=[
            pl.BlockSpec((tm, k), lambda i, j: (i, 0)),
            pl.BlockSpec((None, k, tn), lambda i, j: (layer, 0, jb + j)),
        ],
        out_specs=pl.BlockSpec((tm, tn), lambda i, j: (i, j)),
        out_shape=jax.ShapeDtypeStruct((m, ncols), out_dtype),
        compiler_params=_cp(("arbitrary", "arbitrary")),
        name="matmul",
    )(a, w)


PAD_LO = 16
PAD_SEQ = 32
STENCIL_ROWS = PAD_LO + (ROW_BLK // SEQ) * (SEQ + PAD_SEQ)


def _stencil_layout(seq_len):
    nseq = ROW_BLK // seq_len
    stride = seq_len + PAD_SEQ
    return nseq, stride, PAD_LO + nseq * stride


def _fill_shifted(xr_ref, pieces, seq_len, residues):
    nseq, stride, rows = _stencil_layout(seq_len)
    width = xr_ref.shape[-1]
    xr_ref[0, 0:PAD_LO, :] = jnp.zeros((PAD_LO, width), F32)
    for s in range(nseq):
        b0 = PAD_LO + s * stride
        xr_ref[0, b0:b0 + seq_len, :] = pieces[s]
        xr_ref[0, b0 + seq_len:b0 + stride, :] = jnp.zeros((PAD_SEQ, width), F32)
    x0 = xr_ref[0, 0:rows, :]
    for r in residues:
        if r:
            xr_ref[r, 0:rows, :] = pltpu.roll(x0, rows - r, axis=0)


def _tap(xr_ref, r0, d, rows):
    r = d % 8
    return xr_ref[r, pl.ds(r0 + (d - r), rows), :]


def _per_path(i, body):
    @pl.when(i < N_CTX_BLK)
    def _():
        body(SEQ)

    @pl.when(i >= N_CTX_BLK)
    def _():
        body(DEC_SEQ)


POOL_CHUNK = 64


def _pool_kernel(a_ref, w_ref, s_ref, o_ref, xr_ref, pooled_ref):
    def body(seq_len):
        nseq, stride, _ = _stencil_layout(seq_len)
        for g, win in enumerate(POOL_WINDOWS):
            left = win // 2
            right = win - 1 - left
            cols = slice(g * POOL_CH, (g + 1) * POOL_CH)
            offsets = range(-left, right + 1)
            _fill_shifted(xr_ref, [a_ref[s * seq_len:(s + 1) * seq_len, cols] for s in range(nseq)],
                          seq_len, sorted({d % 8 for d in offsets}))
            for s in range(nseq):
                def chunk(c, carry, s=s, offsets=offsets, left=left, right=right):
                    t0 = c * POOL_CHUNK
                    r0 = pl.multiple_of(PAD_LO + s * stride + t0, 8)
                    x = _tap(xr_ref, r0, 0, POOL_CHUNK)
                    acc = x
                    for d in offsets:
                        if d:
                            acc = acc + _tap(xr_ref, r0, d, POOL_CHUNK)
                    pos = t0 + lax.broadcasted_iota(jnp.int32, (POOL_CHUNK, 1), 0)
                    cnt = jnp.minimum(pos + right + 1, seq_len) - jnp.maximum(pos - left, 0)
                    pooled_ref[pl.ds(pl.multiple_of(s * seq_len + t0, 8), POOL_CHUNK), :] = (
                        acc / cnt.astype(F32) - x)
                    return carry
                lax.fori_loop(0, seq_len // POOL_CHUNK, chunk, 0)
            mixed = jnp.dot(pooled_ref[...].astype(BF16), w_ref[g].astype(BF16),
                            preferred_element_type=F32)
            o_ref[:, cols] = (mixed * s_ref[:, cols]).astype(BF16)

    _per_path(pl.program_id(0), body)


def _pool_mixer(proj, pool_w, pool_scale):
    return pl.pallas_call(
        _pool_kernel,
        grid=(N_ROW_BLK,),
        in_specs=[
            pl.BlockSpec((ROW_BLK, POOL_WIDTH), lambda i: (i, COL_POOL // POOL_WIDTH)),
            pl.BlockSpec((POOL_GROUPS, POOL_CH, POOL_CH), lambda i: (0, 0, 0)),
            pl.BlockSpec((1, POOL_WIDTH), lambda i: (0, 0)),
        ],
        out_specs=pl.BlockSpec((ROW_BLK, POOL_WIDTH), lambda i: (i, 0)),
        out_shape=jax.ShapeDtypeStruct((T_ALL, POOL_WIDTH), BF16),
        scratch_shapes=[pltpu.VMEM((8, STENCIL_ROWS, POOL_CH), F32),
                        pltpu.VMEM((ROW_BLK, POOL_CH), F32)],
        compiler_params=_cp(("arbitrary",)),
        name="pool_mixer",
    )(proj, pool_w, pool_scale.reshape(1, POOL_WIDTH))


CONV_CHUNK = 32


def _conv_kernel(a_ref, g_ref, dw_ref, db_ref, lg_ref, lb_ref, o_ref, xr_ref, y_ref):
    half = CONV_K // 2

    def body(seq_len):
        nseq, stride, _ = _stencil_layout(seq_len)
        pieces = []
        for s in range(nseq):
            rows = slice(s * seq_len, (s + 1) * seq_len)
            pieces.append(a_ref[rows, :] * jax.nn.sigmoid(g_ref[rows, :]))
        _fill_shifted(xr_ref, pieces, seq_len, range(8))
        for s in range(nseq):
            def chunk(c, carry, s=s):
                t0 = c * CONV_CHUNK
                r0 = pl.multiple_of(PAD_LO + s * stride + t0, 8)
                acc = jnp.broadcast_to(db_ref[...], (CONV_CHUNK, CONV_WIDTH))
                for k in range(CONV_K):
                    acc = acc + _tap(xr_ref, r0, k - half, CONV_CHUNK) * dw_ref[k:k + 1, :]
                y_ref[pl.ds(pl.multiple_of(s * seq_len + t0, CONV_CHUNK), CONV_CHUNK), :] = acc
                return carry
            lax.fori_loop(0, seq_len // CONV_CHUNK, chunk, 0, unroll=2)

    _per_path(pl.program_id(0), body)
    y = _layer_norm(y_ref[...]) * lg_ref[...] + lb_ref[...]
    o_ref[...] = (y * jax.nn.sigmoid(y)).astype(BF16)


def _conv_mixer(proj, dw, db, ln_g, ln_b):
    cb = COL_CV // CONV_WIDTH
    vec = pl.BlockSpec((1, CONV_WIDTH), lambda i: (0, 0))
    return pl.pallas_call(
        _conv_kernel,
        grid=(N_ROW_BLK,),
        in_specs=[
            pl.BlockSpec((ROW_BLK, CONV_WIDTH), lambda i: (i, cb)),
            pl.BlockSpec((ROW_BLK, CONV_WIDTH), lambda i: (i, cb + 1)),
            pl.BlockSpec((CONV_K, CONV_WIDTH), lambda i: (0, 0)),
            vec, vec, vec,
        ],
        out_specs=pl.BlockSpec((ROW_BLK, CONV_WIDTH), lambda i: (i, 0)),
        out_shape=jax.ShapeDtypeStruct((T_ALL, CONV_WIDTH), BF16),
        scratch_shapes=[pltpu.VMEM((8, STENCIL_ROWS, CONV_WIDTH), F32),
                        pltpu.VMEM((ROW_BLK, CONV_WIDTH), F32)],
        compiler_params=_cp(("arbitrary",)),
        name="conv_mixer",
    )(proj, proj, dw, db.reshape(1, -1), ln_g.reshape(1, -1), ln_b.reshape(1, -1))


S5_ROWS = 512


def _s5_kernel(*refs, tc, nb, n_u, reverse, post):
    u_refs = refs[:n_u]
    bq_ref, cq_ref, a_ref, h0_ref = refs[n_u:n_u + 4]
    k = n_u + 4
    if post:
        yf_ref, dskip_ref, wglu_ref = refs[k:k + 3]
        k += 3
    y_ref, hf_ref = refs[k:k + 2]
    u_scr, y_scr, hs_ref, st_ref = refs[k + 2:]
    j = pl.program_id(0)
    n_real = u_refs[0].shape[0] if n_u == 1 else n_u

    @pl.when(j == 0)
    def _():
        st_ref[...] = h0_ref[...]

    if n_u == 1:
        u_bt = u_refs[0][...]
    else:
        u_bt = jnp.stack([r[...] for r in u_refs], axis=0)
    u_scr[:, 0:n_real, :] = pltpu.einshape("btc->tbc", u_bt)
    if n_real < nb:
        u_scr[:, n_real:nb, :] = jnp.zeros((tc, nb - n_real, S5_WIDTH), F32)

    rows = tc * nb
    u_tm = u_scr[...].reshape(rows, S5_WIDTH)
    u2 = u_tm.astype(BF16)
    for q in range(S5_Q):
        hs_ref[:, q * 2 * S5_QS:(q + 1) * 2 * S5_QS] = jnp.dot(
            u2[:, q * S5_QU:(q + 1) * S5_QU], bq_ref[q], preferred_element_type=F32)

    lane_w = 4096 // nb
    for q in range(S5_Q):
        for w in range(S5_QS // lane_w):
            re0 = q * 2 * S5_QS + w * lane_w
            im0 = re0 + S5_QS
            a_re = jnp.broadcast_to(a_ref[:, re0:re0 + lane_w], (nb, lane_w))
            a_im = jnp.broadcast_to(a_ref[:, im0:im0 + lane_w], (nb, lane_w))

            def step(t, carry, re0=re0, im0=im0, a_re=a_re, a_im=a_im):
                h_re, h_im = carry
                te = tc - 1 - t if reverse else t
                r0 = pl.multiple_of(te * nb, nb)
                n_re = a_re * h_re - a_im * h_im + hs_ref[pl.ds(r0, nb), re0:re0 + lane_w]
                n_im = a_re * h_im + a_im * h_re + hs_ref[pl.ds(r0, nb), im0:im0 + lane_w]
                hs_ref[pl.ds(r0, nb), re0:re0 + lane_w] = n_re
                hs_ref[pl.ds(r0, nb), im0:im0 + lane_w] = n_im
                return n_re, n_im

            h_re, h_im = lax.fori_loop(
                0, tc, step, (st_ref[:, re0:re0 + lane_w], st_ref[:, im0:im0 + lane_w]), unroll=4)
            st_ref[:, re0:re0 + lane_w] = h_re
            st_ref[:, im0:im0 + lane_w] = h_im

    for q in range(S5_Q):
        yq = jnp.dot(hs_ref[:, q * 2 * S5_QS:(q + 1) * 2 * S5_QS].astype(BF16), cq_ref[q],
                     preferred_element_type=F32)
        y_scr[:, :, q * S5_QU:(q + 1) * S5_QU] = yq.reshape(tc, nb, S5_QU)

    if post:
        y_scr[...] = y_scr[...] + (dskip_ref[...] * u_tm).reshape(tc, nb, S5_WIDTH)
    y_bt = pltpu.einshape("tbc->btc", y_scr[:, 0:n_real, :])
    if post:
        y = jax.nn.gelu(y_bt + yf_ref[...]).reshape(n_real * tc, S5_WIDTH)
        gate = jnp.dot(y.astype(BF16), wglu_ref[...].astype(BF16), preferred_element_type=F32)
        y_ref[...] = (y * jax.nn.sigmoid(gate)).astype(BF16).reshape(n_real, tc, S5_WIDTH)
    else:
        y_ref[...] = y_bt

    @pl.when(j == pl.num_programs(0) - 1)
    def _():
        hf_ref[...] = st_ref[...]


def _s5_pass(proj3, path, direction, bq, cq, a_vec, h0, yf=None, d_skip=None, w_glu=None):
    post = yf is not None
    reverse = direction == 1
    cb = COL_S5 // S5_WIDTH
    if path == "ctx":
        n_real, nb, n = BATCH, BATCH, SEQ
    else:
        n_real, nb, n = DEC_BATCH, S5_LAT_B, DEC_SEQ
    tc = S5_ROWS // nb
    nc = n // tc

    def chunk(j):
        return nc - 1 - j if reverse else j

    if path == "ctx":
        u_specs = [pl.BlockSpec((BATCH, tc, S5_WIDTH), lambda j: (0, chunk(j), cb))]
    else:
        per = SEQ // tc
        u_specs = [
            pl.BlockSpec((None, tc, S5_WIDTH),
                         lambda j, b=b: (T_CTX // SEQ + b * (DEC_SEQ // SEQ) + chunk(j) // per,
                                         chunk(j) % per, cb))
            for b in range(DEC_BATCH)]
    seq_blk = pl.BlockSpec((n_real, tc, S5_WIDTH), lambda j: (0, chunk(j), 0))
    in_specs = u_specs + [
        pl.BlockSpec((None, S5_Q, S5_QU, 2 * S5_QS), lambda j: (direction, 0, 0, 0)),
        pl.BlockSpec((None, S5_Q, 2 * S5_QS, S5_QU), lambda j: (direction, 0, 0, 0)),
        pl.BlockSpec((None, 1, S5_LANES), lambda j: (direction, 0, 0)),
        pl.BlockSpec((None, nb, S5_LANES), lambda j: (direction, 0, 0)),
    ]
    args = [proj3] * len(u_specs) + [bq, cq, a_vec, h0]
    if post:
        in_specs += [seq_blk, pl.BlockSpec((1, S5_WIDTH), lambda j: (0, 0)),
                     pl.BlockSpec((S5_WIDTH, S5_WIDTH), lambda j: (0, 0))]
        args += [yf, d_skip.reshape(1, S5_WIDTH), w_glu]
    return pl.pallas_call(
        functools.partial(_s5_kernel, tc=tc, nb=nb, n_u=len(u_specs), reverse=reverse, post=post),
        grid=(nc,),
        in_specs=in_specs,
        out_specs=[seq_blk, pl.BlockSpec((nb, S5_LANES), lambda j: (0, 0))],
        out_shape=[jax.ShapeDtypeStruct((n_real, n, S5_WIDTH), BF16 if post else F32),
                   jax.ShapeDtypeStruct((nb, S5_LANES), F32)],
        scratch_shapes=[
            pltpu.VMEM((tc, nb, S5_WIDTH), F32),
            pltpu.VMEM((tc, nb, S5_WIDTH), F32),
            pltpu.VMEM((tc * nb, S5_LANES), F32),
            pltpu.VMEM((nb, S5_LANES), F32),
        ],
        compiler_params=_cp(("arbitrary",)),
        name="s5_pass",
    )(*args)


def _s5_mixer(proj3, path, bq, cq, a_vec, h0, d_skip, w_glu):
    yf, hf_f = _s5_pass(proj3, path, 0, bq, cq, a_vec, h0)
    y, hf_b = _s5_pass(proj3, path, 1, bq, cq, a_vec, h0, yf, d_skip, w_glu)
    return y.reshape(-1, S5_WIDTH), jnp.stack([hf_f, hf_b])


def _s5_params(lam_re, lam_im, log_dt, b_re, b_im, c_re, c_im):
    dt = jnp.exp(log_dt)[..., None]
    x = lam_re * dt
    y = lam_im * dt
    ex = jnp.exp(x)
    abar_re = ex * jnp.cos(y)
    abar_im = ex * jnp.sin(y)
    num_re = jnp.expm1(x) * jnp.cos(y) - 2.0 * jnp.square(jnp.sin(0.5 * y))
    num_im = abar_im
    den = lam_re * lam_re + lam_im * lam_im
    coef_re = (num_re * lam_re + num_im * lam_im) / den
    coef_im = (num_im * lam_re - num_re * lam_im) / den
    bbar_re = coef_re[..., None] * b_re - coef_im[..., None] * b_im
    bbar_im = coef_re[..., None] * b_im + coef_im[..., None] * b_re
    eye = jnp.eye(S5_GQ, dtype=F32)
    bb = jnp.stack([bbar_re, bbar_im]).reshape(2, 2, S5_Q, S5_GQ, S5_STATE, S5_CH)
    bq = jnp.einsum("rdqgpc,gh->dqgcrhp", bb, eye).reshape(2, S5_Q, S5_QU, 2 * S5_QS)
    cc = jnp.stack([c_re, -c_im]).reshape(2, 2, S5_Q, S5_GQ, S5_CH, S5_STATE)
    cq = jnp.einsum("rdqgcp,gh->dqrhpgc", cc, eye).reshape(2, S5_Q, 2 * S5_QS, S5_QU)
    ab = jnp.stack([abar_re, abar_im]).reshape(2, 2, S5_Q, S5_GQ, S5_STATE)
    a_vec = ab.transpose(1, 2, 0, 3, 4).reshape(2, 1, S5_LANES)
    return bq.astype(BF16), cq.astype(BF16), a_vec


def _s5_state_to_lanes(st):
    b = st.shape[0]
    s = st.reshape(b, 2, 2, S5_Q, S5_GQ, S5_STATE).transpose(1, 0, 3, 2, 4, 5)
    return s.reshape(2, b, S5_LANES)


def _s5_lanes_to_state(hl):
    b = hl.shape[1]
    s = hl.reshape(2, b, S5_Q, 2, S5_GQ, S5_STATE).transpose(1, 0, 3, 2, 4, 5)
    return s.reshape(b, 2, 2, S5_GROUPS, S5_STATE)


def _rope(x, cos, sin_signed):
    lane = lax.broadcasted_iota(jnp.int32, x.shape, 1)
    quarter = HEAD_DIM // 4
    partner = jnp.where((lane & quarter) == 0,
                        pltpu.roll(x, HEAD_DIM - quarter, axis=1),
                        pltpu.roll(x, quarter, axis=1))
    return x * cos + partner * sin_signed


def _nt_dot(a, b):
    return lax.dot_general(a, b, (((1,), (1,)), ((), ())), preferred_element_type=F32)


def _stack_heads(q_ref, rope=None):
    parts = []
    for r in range(REP):
        q = q_ref[:, r * HEAD_DIM:(r + 1) * HEAD_DIM]
        if rope is not None:
            q = _rope(q, *rope)
        parts.append(q.astype(BF16))
    return jnp.concatenate(parts, axis=0)


def _with_ones(v):
    return jnp.concatenate([v.astype(BF16), jnp.ones(v.shape, BF16)], axis=1)


def _sink_column(sink_ref, h0, rows):
    return jnp.concatenate(
        [jnp.broadcast_to(sink_ref[pl.ds(h0 + r, 1), 0:1], (rows, 1)) for r in range(REP)], axis=0)


def _ctx_attn_kernel(q_ref, k_ref, v_ref, sink_ref, o_ref, ko_ref, vo_ref):
    h0 = pl.program_id(1) * REP
    k = k_ref[...]
    v = v_ref[...]
    ko_ref[...] = k
    vo_ref[...] = v
    q = _stack_heads(q_ref)
    s = _nt_dot(q, k.astype(BF16)) * (HEAD_DIM ** -0.5)
    sink = _sink_column(sink_ref, h0, SEQ)
    m = jnp.maximum(jnp.max(s, axis=-1, keepdims=True), sink)
    p = jnp.exp(s - m)
    acc = jnp.dot(p.astype(BF16), _with_ones(v), preferred_element_type=F32)
    o = acc[:, :HEAD_DIM] / (acc[:, HEAD_DIM:HEAD_DIM + 1] + jnp.exp(sink - m))
    for r in range(REP):
        o_ref[:, r * HEAD_DIM:(r + 1) * HEAD_DIM] = o[r * SEQ:(r + 1) * SEQ].astype(BF16)


def _ctx_attention(proj, sink_tile):
    qw = REP * HEAD_DIM
    kv_out = pl.BlockSpec((SEQ, HEAD_DIM), lambda b, h: (b, h))
    return pl.pallas_call(
        _ctx_attn_kernel,
        grid=(BATCH, N_KV_HEADS),
        in_specs=[
            pl.BlockSpec((SEQ, qw), lambda b, h: (b, COL_Q // qw + h)),
            pl.BlockSpec((SEQ, HEAD_DIM), lambda b, h: (b, COL_K // HEAD_DIM + h)),
            pl.BlockSpec((SEQ, HEAD_DIM), lambda b, h: (b, COL_V // HEAD_DIM + h)),
            pl.BlockSpec((N_HEADS, HEAD_DIM), lambda b, h: (0, 0)),
        ],
        out_specs=[pl.BlockSpec((SEQ, qw), lambda b, h: (b, h)), kv_out, kv_out],
        out_shape=[jax.ShapeDtypeStruct((T_CTX, ATTN_WIDTH), BF16),
                   jax.ShapeDtypeStruct((T_CTX, KV_WIDTH), F32),
                   jax.ShapeDtypeStruct((T_CTX, KV_WIDTH), F32)],
        compiler_params=_cp(("arbitrary", "arbitrary")),
        name="ctx_attention",
    )(proj, proj, proj, sink_tile)


LAT_TQ = 256
LAT_KWIN = LAT_TQ + 2 * WINDOW


def _lat_attn_kernel(q_ref, k_ref, v_ref, kc_ref, vc_ref, cq_ref, sq_ref, ck_ref, sk_ref,
                     sink_ref, o_ref):
    h0 = pl.program_id(1) * REP
    q0 = pl.program_id(2) * LAT_TQ
    k0 = pl.multiple_of(jnp.clip(q0 - WINDOW, 0, DEC_SEQ - LAT_KWIN), WINDOW)
    win = pl.ds(k0, LAT_KWIN)
    kb = _rope(k_ref[win, :], ck_ref[win, :], sk_ref[win, :]).astype(BF16)
    rows = REP * LAT_TQ
    row = lax.broadcasted_iota(jnp.int32, (rows, LAT_KWIN), 0)
    qpos = q0 + (row & (LAT_TQ - 1))
    kpos = k0 + lax.broadcasted_iota(jnp.int32, (rows, LAT_KWIN), 1)
    mask = jnp.abs(qpos - kpos) <= WINDOW
    scale = HEAD_DIM ** -0.5
    q = _stack_heads(q_ref, (cq_ref[...], sq_ref[...]))
    s_lat = jnp.where(mask, _nt_dot(q, kb) * scale, NEG_INF)
    s_ctx = _nt_dot(q, kc_ref[...].astype(BF16)) * scale
    sink = _sink_column(sink_ref, h0, LAT_TQ)
    m = jnp.maximum(jnp.maximum(jnp.max(s_lat, axis=-1, keepdims=True),
                                jnp.max(s_ctx, axis=-1, keepdims=True)), sink)
    p_lat = jnp.exp(s_lat - m).astype(BF16)
    p_ctx = jnp.exp(s_ctx - m).astype(BF16)
    acc = (jnp.dot(p_lat, _with_ones(v_ref[win, :]), preferred_element_type=F32)
           + jnp.dot(p_ctx, _with_ones(vc_ref[...]), preferred_element_type=F32))
    o = acc[:, :HEAD_DIM] / (acc[:, HEAD_DIM:HEAD_DIM + 1] + jnp.exp(sink - m))
    for r in range(REP):
        o_ref[:, r * HEAD_DIM:(r + 1) * HEAD_DIM] = o[r * LAT_TQ:(r + 1) * LAT_TQ].astype(BF16)


def _lat_attention(proj, cache_k_l, cache_v_l, rope_cos, rope_sin, sink_tile):
    tq = LAT_TQ
    nq = DEC_SEQ // tq
    qw = REP * HEAD_DIM
    row0_q = T_CTX // tq
    row0_k = T_CTX // DEC_SEQ
    tab_q = pl.BlockSpec((tq, HEAD_DIM), lambda b, h, i: (i, 0))
    tab_k = pl.BlockSpec((DEC_SEQ, HEAD_DIM), lambda b, h, i: (0, 0))
    return pl.pallas_call(
        _lat_attn_kernel,
        grid=(DEC_BATCH, N_KV_HEADS, nq),
        in_specs=[
            pl.BlockSpec((tq, qw), lambda b, h, i: (row0_q + b * nq + i, COL_Q // qw + h)),
            pl.BlockSpec((DEC_SEQ, HEAD_DIM), lambda b, h, i: (row0_k + b, COL_K // HEAD_DIM + h)),
            pl.BlockSpec((DEC_SEQ, HEAD_DIM), lambda b, h, i: (row0_k + b, COL_V // HEAD_DIM + h)),
            pl.BlockSpec((None, PAST_LEN, HEAD_DIM), lambda b, h, i: (b, 0, h)),
            pl.BlockSpec((None, PAST_LEN, HEAD_DIM), lambda b, h, i: (b, 0, h)),
            tab_q, tab_q, tab_k, tab_k,
            pl.BlockSpec((N_HEADS, HEAD_DIM), lambda b, h, i: (0, 0)),
        ],
        out_specs=pl.BlockSpec((tq, qw), lambda b, h, i: (b * nq + i, h)),
        out_shape=jax.ShapeDtypeStruct((T_LAT, ATTN_WIDTH), BF16),
        compiler_params=_cp(("arbitrary", "arbitrary", "arbitrary")),
        name="lat_attention",
    )(proj, proj, proj, cache_k_l, cache_v_l, rope_cos, rope_sin, rope_cos, rope_sin, sink_tile)


def _rope_tables():
    t = jnp.arange(DEC_SEQ)
    row, col = t // GRID_W, t % GRID_W
    quarter = HEAD_DIM // 4
    inv = ROPE_BASE ** (-jnp.arange(quarter, dtype=F32) / quarter)
    lane = jnp.arange(HEAD_DIM)
    pos = jnp.where(lane[None, :] < HEAD_DIM // 2, row[:, None], col[:, None]).astype(F32)
    ang = pos * inv[lane % quarter][None, :]
    sign = jnp.where((lane & quarter) == 0, -1.0, 1.0).astype(F32)
    return jnp.cos(ang), jnp.sin(ang) * sign[None, :]


def _merge_kernel(yp_ref, ysc_ref, ysl_ref, yac_ref, yal_ref, yc_ref, g0_ref, g1_ref, g2_ref,
                  g3_ref, wp_ref, ws_ref, wa_ref, wc_ref, o_ref):
    is_ctx = pl.program_id(1) < N_CTX_BLK
    y_s5 = jnp.where(is_ctx, ysc_ref[...], ysl_ref[...])
    attn = jnp.where(is_ctx, yac_ref[...], yal_ref[...])
    acc = None
    for y, g_ref, w_ref in ((yp_ref[...], g0_ref, wp_ref), (y_s5, g1_ref, ws_ref),
                            (attn, g2_ref, wa_ref), (yc_ref[...], g3_ref, wc_ref)):
        t = g_ref[...].astype(F32) * jnp.dot(y, w_ref[...].astype(BF16), preferred_element_type=F32)
        acc = t if acc is None else acc + t
    o_ref[...] = acc.astype(BF16)


def _merge(gates, y_pool, y_s5_ctx, y_s5_lat, attn_ctx, attn_lat, y_conv, layer,
           w_pool, w_s5, w_attn, w_conv):
    tm, tn = ROW_BLK, 512
    per = D_MODEL // tn

    def act(width):
        return pl.BlockSpec((tm, width), lambda j, i: (i, 0))

    def ctx_part(width):
        return pl.BlockSpec((tm, width), lambda j, i: (jnp.minimum(i, N_CTX_BLK - 1), 0))

    def lat_part(width):
        return pl.BlockSpec((tm, width), lambda j, i: (jnp.maximum(i - N_CTX_BLK, 0), 0))

    def gate(branch):
        return pl.BlockSpec((tm, tn), lambda j, i: (i, branch * per + j))

    def wt(width):
        return pl.BlockSpec((None, width, tn), lambda j, i: (layer, 0, j))

    return pl.pallas_call(
        _merge_kernel,
        grid=(D_MODEL // tn, T_ALL // tm),
        in_specs=[act(POOL_WIDTH), ctx_part(S5_WIDTH), lat_part(S5_WIDTH),
                  ctx_part(ATTN_WIDTH), lat_part(ATTN_WIDTH), act(CONV_WIDTH),
                  gate(0), gate(1), gate(2), gate(3),
                  wt(POOL_WIDTH), wt(S5_WIDTH), wt(ATTN_WIDTH), wt(CONV_WIDTH)],
        out_specs=pl.BlockSpec((tm, tn), lambda j, i: (i, j)),
        out_shape=jax.ShapeDtypeStruct((T_ALL, D_MODEL), BF16),
        compiler_params=_cp(("arbitrary", "arbitrary")),
        name="merge",
    )(y_pool, y_s5_ctx, y_s5_lat, attn_ctx, attn_lat, y_conv, gates, gates, gates, gates,
      w_pool, w_s5, w_attn, w_conv)


FFN_TILE = 256
N_FFN_TILE = D_FF // FFN_TILE


def _ffn_kernel(h_ref, wg_ref, wv_ref, dwg_ref, dwv_ref, dbg_ref, dbv_ref, wd_ref, o_ref,
                ug_ref, uv_ref, acc_ref):
    i = pl.program_id(0)
    f = pl.program_id(1)

    def up(slot):
        h = h_ref[...]
        ug_ref[slot] = jnp.dot(h, wg_ref[...].astype(BF16), preferred_element_type=F32)
        uv_ref[slot] = jnp.dot(h, wv_ref[...].astype(BF16), preferred_element_type=F32)

    def down(slot):
        pos, seq_len = _seq_pos(i, ROW_BLK)
        first = pos == 0
        last = pos == seq_len - 1

        def conv3(u, dw_ref, db_ref):
            prev = jnp.where(first, 0.0, _shift_rows(u, -1))
            nxt = jnp.where(last, 0.0, _shift_rows(u, 1))
            return prev * dw_ref[0:1, :] + u * dw_ref[1:2, :] + nxt * dw_ref[2:3, :] + db_ref[...]

        gt = conv3(ug_ref[slot], dwg_ref, dbg_ref)
        val = conv3(uv_ref[slot], dwv_ref, dbv_ref)
        act = (gt * jax.nn.sigmoid(gt) * val).astype(BF16)
        acc_ref[...] += jnp.dot(act, wd_ref[...].astype(BF16), preferred_element_type=F32)

    @pl.when(f == 0)
    def _():
        acc_ref[...] = jnp.zeros_like(acc_ref)
        up(0)

    for par in (0, 1):
        @pl.when(jnp.logical_and(jnp.logical_and(f > 0, f < N_FFN_TILE), f % 2 == par))
        def _(par=par):
            up(par)
            down(1 - par)

    @pl.when(f == N_FFN_TILE)
    def _():
        down((N_FFN_TILE - 1) % 2)
        o_ref[...] = acc_ref[...].astype(o_ref.dtype)


def _conv_ffn(h, layer, w_up, dw, db, w_down):
    tf, nf = FFN_TILE, N_FFN_TILE
    db3 = db.reshape(DEPTH, 1, 2 * D_FF)

    def cur(f):
        return jnp.minimum(f, nf - 1)

    def prv(f):
        return jnp.maximum(f - 1, 0)

    return pl.pallas_call(
        _ffn_kernel,
        grid=(N_ROW_BLK, nf + 1),
        in_specs=[
            pl.BlockSpec((ROW_BLK, D_MODEL), lambda i, f: (i, 0)),
            pl.BlockSpec((None, D_MODEL, tf), lambda i, f: (layer, 0, cur(f))),
            pl.BlockSpec((None, D_MODEL, tf), lambda i, f: (layer, 0, nf + cur(f))),
            pl.BlockSpec((None, FFN_CONV_K, tf), lambda i, f: (layer, 0, prv(f))),
            pl.BlockSpec((None, FFN_CONV_K, tf), lambda i, f: (layer, 0, nf + prv(f))),
            pl.BlockSpec((None, 1, tf), lambda i, f: (layer, 0, prv(f))),
            pl.BlockSpec((None, 1, tf), lambda i, f: (layer, 0, nf + prv(f))),
            pl.BlockSpec((None, tf, D_MODEL), lambda i, f: (layer, prv(f), 0)),
        ],
        out_specs=pl.BlockSpec((ROW_BLK, D_MODEL), lambda i, f: (i, 0)),
        out_shape=jax.ShapeDtypeStruct((T_ALL, D_MODEL), BF16),
        scratch_shapes=[pltpu.VMEM((2, ROW_BLK, tf), F32), pltpu.VMEM((2, ROW_BLK, tf), F32),
                        pltpu.VMEM((ROW_BLK, D_MODEL), F32)],
        compiler_params=_cp(("arbitrary", "arbitrary")),
        name="conv_ffn",
    )(h, w_up, w_up, dw, dw, db3, db3, w_down)


def kernel(x_prompt, x_sample, cache_k, cache_v, state_s5, c, c_ctx, w_ada, b_ada, w_in, pool_w, pool_scale, s5_lambda_re, s5_lambda_im, s5_log_dt, s5_b_re, s5_b_im, s5_c_re, s5_c_im, s5_d, s5_w_glu, attn_sink, conv_dw, conv_db, conv_ln_g, conv_ln_b, w_br_pool, w_br_s5, w_br_attn, w_br_conv, w_out, ln1_g, ln1_b, ffn_w_up, ffn_dw, ffn_db, ffn_w_down, ln2_g, ln2_b):
    x_parts = (x_prompt.reshape(T_CTX, D_MODEL), x_sample.reshape(T_LAT, D_MODEL))
    cvec = jnp.concatenate(
        [c_ctx[None, :], c, jnp.zeros((MOD_ROWS - 1 - DEC_BATCH, D_MODEL), F32)], axis=0)
    mod = _mod_table(cvec, w_ada, b_ada).reshape(DEPTH, MOD_ROWS, 6, D_MODEL)
    rope_cos, rope_sin = _rope_tables()

    ks_out, vs_out, ss_out = [], [], []
    h = _ln_mod(x_parts[0], x_parts[1], mod[0], 0, 1)
    for l in range(DEPTH):
        proj = _matmul(h, w_in, l, 0, MIX_COLS)
        gates = _matmul(h, w_in, l, COL_GATE, N_BRANCH * D_MODEL, out_dtype=BF16, gate=True)

        y_pool = _pool_mixer(proj, pool_w[l], pool_scale[l])
        y_conv = _conv_mixer(proj, conv_dw[l], conv_db[l], conv_ln_g[l], conv_ln_b[l])

        bq, cq, a_vec = _s5_params(s5_lambda_re[l], s5_lambda_im[l], s5_log_dt[l], s5_b_re[l],
                                   s5_b_im[l], s5_c_re[l], s5_c_im[l])
        proj3 = proj.reshape(T_ALL // SEQ, SEQ, MIX_COLS)
        h0_ctx = jnp.zeros((2, BATCH, S5_LANES), F32)
        h0_lat = jnp.pad(_s5_state_to_lanes(state_s5[:, l]), ((0, 0), (0, S5_LAT_B - DEC_BATCH), (0, 0)))
        y_s5_ctx, hf_ctx = _s5_mixer(proj3, "ctx", bq, cq, a_vec, h0_ctx, s5_d[l], s5_w_glu[l])
        y_s5_lat, _ = _s5_mixer(proj3, "lat", bq, cq, a_vec, h0_lat, s5_d[l], s5_w_glu[l])

        sink_tile = jnp.broadcast_to(attn_sink[l][:, None], (N_HEADS, HEAD_DIM))
        attn_ctx, k_new, v_new = _ctx_attention(proj, sink_tile)
        attn_lat = _lat_attention(proj, cache_k[:, l].reshape(DEC_BATCH, PAST_LEN, KV_WIDTH),
                                  cache_v[:, l].reshape(DEC_BATCH, PAST_LEN, KV_WIDTH),
                                  rope_cos, rope_sin, sink_tile)

        merged = _merge(gates, y_pool, y_s5_ctx, y_s5_lat, attn_ctx, attn_lat, y_conv, l,
                        w_br_pool, w_br_s5, w_br_attn, w_br_conv)
        mixed = _matmul(merged, w_out, l, out_dtype=BF16)
        x, h2 = _res_ln(x_parts, mixed, mod[l], 2, ln1_g[l], ln1_b[l], mod[l], 3, 4)
        f = _conv_ffn(h2, l, ffn_w_up, ffn_dw, ffn_db, ffn_w_down)
        if l + 1 < DEPTH:
            x, h = _res_ln((x,), f, mod[l], 5, ln2_g[l], ln2_b[l], mod[l + 1], 0, 1)
            x_parts = (x,)
        else:
            y_prompt, y_sample = _res_ln((x,), f, mod[l], 5, ln2_g[l], ln2_b[l])

        ks_out.append(k_new.reshape(BATCH, SEQ, N_KV_HEADS, HEAD_DIM))
        vs_out.append(v_new.reshape(BATCH, SEQ, N_KV_HEADS, HEAD_DIM))
        ss_out.append(_s5_lanes_to_state(hf_ctx))

    return (y_prompt.reshape(BATCH, SEQ, D_MODEL), y_sample.reshape(DEC_BATCH, DEC_SEQ, D_MODEL),
            jnp.stack(ks_out, axis=1), jnp.stack(vs_out, axis=1), jnp.stack(ss_out, axis=1))
```

```python
import functools

import jax
import jax.numpy as jnp
from jax import lax
from jax.experimental import pallas as pl
from jax.experimental.pallas import tpu as pltpu

F32 = jnp.float32
BF16 = jnp.bfloat16

D_MODEL = 2048
BATCH = 16
SEQ = 256
DEPTH = 2
DEC_BATCH = 4
DEC_SEQ = 1024
PAST_LEN = 512
GRID_W = 64
POOL_WIDTH = 512
POOL_GROUPS = 4
POOL_CH = POOL_WIDTH // POOL_GROUPS
POOL_WINDOWS = (2, 4, 8, 16)
S5_WIDTH = 512
S5_CH = 16
S5_GROUPS = S5_WIDTH // S5_CH
S5_STATE = 64
N_HEADS = 8
N_KV_HEADS = 2
HEAD_DIM = 128
REP = N_HEADS // N_KV_HEADS
ATTN_WIDTH = N_HEADS * HEAD_DIM
KV_WIDTH = N_KV_HEADS * HEAD_DIM
WINDOW = 128
ROPE_BASE = 10000.0
NEG_INF = -1e30
CONV_WIDTH = 512
CONV_K = 31
N_BRANCH = 4
D_FF = 5632
FFN_CONV_K = 3
ALPHA = (2.0 * DEPTH) ** 0.25
LN_EPS = 1e-5
MIX_COLS = POOL_WIDTH + S5_WIDTH + ATTN_WIDTH + 2 * KV_WIDTH + 2 * CONV_WIDTH
N_IN = MIX_COLS + N_BRANCH * D_MODEL

COL_POOL = 0
COL_S5 = POOL_WIDTH
COL_Q = COL_S5 + S5_WIDTH
COL_K = COL_Q + ATTN_WIDTH
COL_V = COL_K + KV_WIDTH
COL_CV = COL_V + KV_WIDTH
COL_GATE = MIX_COLS

T_CTX = BATCH * SEQ
T_LAT = DEC_BATCH * DEC_SEQ
T_ALL = T_CTX + T_LAT
ROW_BLK = 1024
N_ROW_BLK = T_ALL // ROW_BLK
N_CTX_BLK = T_CTX // ROW_BLK
MOD_ROWS = 8
LN_ROWS = 512

S5_Q = 4
S5_GQ = S5_GROUPS // S5_Q
S5_QU = S5_GQ * S5_CH
S5_QS = S5_GQ * S5_STATE
S5_LANES = 2 * S5_GROUPS * S5_STATE

VMEM_LIMIT = 56 * 1024 * 1024


def _cp(sem):
    return pltpu.CompilerParams(dimension_semantics=sem, vmem_limit_bytes=VMEM_LIMIT)


def _mod_row(row_block, rows_per_block):
    lat_batch = (row_block * rows_per_block) // DEC_SEQ - T_CTX // DEC_SEQ
    return jnp.maximum(lat_batch + 1, 0)


def _seq_pos(row_block, rows):
    r = lax.broadcasted_iota(jnp.int32, (rows, 1), 0)
    seq_len = jnp.where(row_block < N_CTX_BLK, SEQ, DEC_SEQ)
    return r & (seq_len - 1), seq_len


def _layer_norm(x):
    mu = jnp.mean(x, axis=-1, keepdims=True)
    xc = x - mu
    var = jnp.mean(xc * xc, axis=-1, keepdims=True)
    return xc * lax.rsqrt(var + LN_EPS)


def _shift_rows(x, d):
    n = x.shape[0]
    return pltpu.roll(x, (-d) % n, axis=0)


def _split_bf16(x):
    hi = x.astype(BF16)
    return hi, (x - hi.astype(F32)).astype(BF16)


def _mod_kernel(c_ref, w_ref, b_ref, o_ref):
    c = c_ref[...]
    a_hi, a_lo = _split_bf16(c * jax.nn.sigmoid(c))
    w_hi, w_lo = _split_bf16(w_ref[0])
    acc = jnp.dot(a_hi, w_hi, preferred_element_type=F32)
    acc = acc + jnp.dot(a_lo, w_hi, preferred_element_type=F32)
    acc = acc + jnp.dot(a_hi, w_lo, preferred_element_type=F32)
    o_ref[0] = acc + b_ref[0]


def _mod_table(cvec, w_ada, b_ada):
    tn = 1024
    n6 = 6 * D_MODEL
    return pl.pallas_call(
        _mod_kernel,
        grid=(DEPTH, n6 // tn),
        in_specs=[
            pl.BlockSpec((MOD_ROWS, D_MODEL), lambda l, j: (0, 0)),
            pl.BlockSpec((1, D_MODEL, tn), lambda l, j: (l, 0, j)),
            pl.BlockSpec((1, 1, tn), lambda l, j: (l, 0, j)),
        ],
        out_specs=pl.BlockSpec((1, MOD_ROWS, tn), lambda l, j: (l, 0, j)),
        out_shape=jax.ShapeDtypeStruct((DEPTH, MOD_ROWS, n6), F32),
        compiler_params=_cp(("arbitrary", "arbitrary")),
        name="mod_table",
    )(cvec, w_ada, b_ada.reshape(DEPTH, 1, n6))


N_CTX_LN = T_CTX // LN_ROWS


def _row_spec():
    return pl.BlockSpec((LN_ROWS, D_MODEL), lambda i: (i, 0))


def _ctx_part_spec():
    return pl.BlockSpec((LN_ROWS, D_MODEL), lambda i: (jnp.minimum(i, N_CTX_LN - 1), 0))


def _lat_part_spec():
    return pl.BlockSpec((LN_ROWS, D_MODEL), lambda i: (jnp.maximum(i - N_CTX_LN, 0), 0))


def _mod_spec():
    return pl.BlockSpec((1, 6, D_MODEL), lambda i: (_mod_row(i, LN_ROWS), 0, 0))


def _read_rows(refs):
    if len(refs) == 1:
        return refs[0][...]
    return jnp.where(pl.program_id(0) < N_CTX_LN, refs[0][...], refs[1][...])


def _ln_mod_kernel(xc_ref, xl_ref, mod_ref, h_ref, *, shift_i, scale_i):
    y = _layer_norm(_read_rows((xc_ref, xl_ref)))
    scale = mod_ref[0, scale_i:scale_i + 1, :]
    shift = mod_ref[0, shift_i:shift_i + 1, :]
    h_ref[...] = (y * (1.0 + scale) + shift).astype(BF16)


def _ln_mod(x_ctx, x_lat, mod_l, shift_i, scale_i):
    return pl.pallas_call(
        functools.partial(_ln_mod_kernel, shift_i=shift_i, scale_i=scale_i),
        grid=(T_ALL // LN_ROWS,),
        in_specs=[_ctx_part_spec(), _lat_part_spec(), _mod_spec()],
        out_specs=_row_spec(),
        out_shape=jax.ShapeDtypeStruct((T_ALL, D_MODEL), BF16),
        compiler_params=_cp(("arbitrary",)),
        name="ln_mod",
    )(x_ctx, x_lat, mod_l)


def _res_ln_kernel(*refs, n_x, gate_i, next_shift_i, next_scale_i, split_out):
    x_refs = refs[:n_x]
    y_ref, mod_ref, g_ref, b_ref = refs[n_x:n_x + 4]
    rest = refs[n_x + 4:]
    gate = mod_ref[0, gate_i:gate_i + 1, :]
    z = ALPHA * _read_rows(x_refs) + gate * y_ref[...].astype(F32)
    xn = _layer_norm(z) * g_ref[...] + b_ref[...]
    if split_out:
        xc_ref, xl_ref = rest

        @pl.when(pl.program_id(0) < N_CTX_LN)
        def _():
            xc_ref[...] = xn

        @pl.when(pl.program_id(0) >= N_CTX_LN)
        def _():
            xl_ref[...] = xn
    else:
        nmod_ref, xo_ref, h_ref = rest
        xo_ref[...] = xn
        scale = nmod_ref[0, next_scale_i:next_scale_i + 1, :]
        shift = nmod_ref[0, next_shift_i:next_shift_i + 1, :]
        h_ref[...] = (_layer_norm(xn) * (1.0 + scale) + shift).astype(BF16)


def _res_ln(x_parts, y, mod_l, gate_i, g, b, next_mod=None, next_shift_i=None, next_scale_i=None):
    vec = pl.BlockSpec((1, D_MODEL), lambda i: (0, 0))
    x_specs = [_row_spec()] if len(x_parts) == 1 else [_ctx_part_spec(), _lat_part_spec()]
    in_specs = x_specs + [_row_spec(), _mod_spec(), vec, vec]
    args = list(x_parts) + [y, mod_l, g.reshape(1, D_MODEL), b.reshape(1, D_MODEL)]
    split_out = next_mod is None
    if split_out:
        out_specs = [_ctx_part_spec(), _lat_part_spec()]
        out_shape = [jax.ShapeDtypeStruct((T_CTX, D_MODEL), F32),
                     jax.ShapeDtypeStruct((T_LAT, D_MODEL), F32)]
    else:
        in_specs.append(_mod_spec())
        args.append(next_mod)
        out_specs = [_row_spec(), _row_spec()]
        out_shape = [jax.ShapeDtypeStruct((T_ALL, D_MODEL), F32),
                     jax.ShapeDtypeStruct((T_ALL, D_MODEL), BF16)]
    return pl.pallas_call(
        functools.partial(_res_ln_kernel, n_x=len(x_parts), gate_i=gate_i,
                          next_shift_i=next_shift_i, next_scale_i=next_scale_i,
                          split_out=split_out),
        grid=(T_ALL // LN_ROWS,),
        in_specs=in_specs,
        out_specs=out_specs,
        out_shape=out_shape,
        compiler_params=_cp(("arbitrary",)),
        name="res_ln",
    )(*args)


def _matmul_kernel(a_ref, w_ref, o_ref, *, gate):
    y = jnp.dot(a_ref[...], w_ref[...].astype(BF16), preferred_element_type=F32)
    if gate:
        y = jax.nn.sigmoid(y)
    o_ref[...] = y.astype(o_ref.dtype)


def _matmul(a, w, layer, col0=0, ncols=None, out_dtype=F32, gate=False):
    tm, tn = 2048, 512
    m, k = a.shape
    ncols = w.shape[2] - col0 if ncols is None else ncols
    assert col0 % tn == 0 and ncols % tn == 0 and m % tm == 0
    jb = col0 // tn
    return pl.pallas_call(
        functools.partial(_matmul_kernel, gate=gate),
        grid=(m // tm, ncols // tn),
        in_specs=[
            pl.BlockSpec((tm, k), lambda i, j: (i, 0)),
            pl.BlockSpec((None, k, tn), lambda i, j: (layer, 0, jb + j)),
        ],
        out_specs=pl.BlockSpec((tm, tn), lambda i, j: (i, j)),
        out_shape=jax.ShapeDtypeStruct((m, ncols), out_dtype),
        compiler_params=_cp(("arbitrary", "arbitrary")),
        name="matmul",
    )(a, w)


PAD_LO = 16
PAD_SEQ = 32
STENCIL_ROWS = PAD_LO + (ROW_BLK // SEQ) * (SEQ + PAD_SEQ)


def _stencil_layout(seq_len):
    nseq = ROW_BLK // seq_len
    stride = seq_len + PAD_SEQ
    return nseq, stride, PAD_LO + nseq * stride


def _fill_shifted(xr_ref, pieces, seq_len, residues):
    nseq, stride, rows = _stencil_layout(seq_len)
    width = xr_ref.shape[-1]
    xr_ref[0, 0:PAD_LO, :] = jnp.zeros((PAD_LO, width), F32)
    for s in range(nseq):
        b0 = PAD_LO + s * stride
        xr_ref[0, b0:b0 + seq_len, :] = pieces[s]
        xr_ref[0, b0 + seq_len:b0 + stride, :] = jnp.zeros((PAD_SEQ, width), F32)
    x0 = xr_ref[0, 0:rows, :]
    for r in residues:
        if r:
            xr_ref[r, 0:rows, :] = pltpu.roll(x0, rows - r, axis=0)


def _tap(xr_ref, r0, d, rows):
    r = d % 8
    return xr_ref[r, pl.ds(r0 + (d - r), rows), :]


def _per_path(i, body):
    @pl.when(i < N_CTX_BLK)
    def _():
        body(SEQ)

    @pl.when(i >= N_CTX_BLK)
    def _():
        body(DEC_SEQ)


POOL_CHUNK = 64


def _pool_kernel(a_ref, w_ref, s_ref, o_ref, xr_ref, pooled_ref):
    def body(seq_len):
        nseq, stride, _ = _stencil_layout(seq_len)
        for g, win in enumerate(POOL_WINDOWS):
            left = win // 2
            right = win - 1 - left
            cols = slice(g * POOL_CH, (g + 1) * POOL_CH)
            offsets = range(-left, right + 1)
            _fill_shifted(xr_ref, [a_ref[s * seq_len:(s + 1) * seq_len, cols] for s in range(nseq)],
                          seq_len, sorted({d % 8 for d in offsets}))
            for s in range(nseq):
                def chunk(c, carry, s=s, offsets=offsets, left=left, right=right):
                    t0 = c * POOL_CHUNK
                    r0 = pl.multiple_of(PAD_LO + s * stride + t0, 8)
                    x = _tap(xr_ref, r0, 0, POOL_CHUNK)
                    acc = x
                    for d in offsets:
                        if d:
                            acc = acc + _tap(xr_ref, r0, d, POOL_CHUNK)
                    pos = t0 + lax.broadcasted_iota(jnp.int32, (POOL_CHUNK, 1), 0)
                    cnt = jnp.minimum(pos + right + 1, seq_len) - jnp.maximum(pos - left, 0)
                    pooled_ref[pl.ds(pl.multiple_of(s * seq_len + t0, 8), POOL_CHUNK), :] = (
                        acc / cnt.astype(F32) - x)
                    return carry
                lax.fori_loop(0, seq_len // POOL_CHUNK, chunk, 0)
            mixed = jnp.dot(pooled_ref[...].astype(BF16), w_ref[g].astype(BF16),
                            preferred_element_type=F32)
            o_ref[:, cols] = (mixed * s_ref[:, cols]).astype(BF16)

    _per_path(pl.program_id(0), body)


def _pool_mixer(proj, pool_w, pool_scale):
    return pl.pallas_call(
        _pool_kernel,
        grid=(N_ROW_BLK,),
        in_specs=[
            pl.BlockSpec((ROW_BLK, POOL_WIDTH), lambda i: (i, COL_POOL // POOL_WIDTH)),
            pl.BlockSpec((POOL_GROUPS, POOL_CH, POOL_CH), lambda i: (0, 0, 0)),
            pl.BlockSpec((1, POOL_WIDTH), lambda i: (0, 0)),
        ],
        out_specs=pl.BlockSpec((ROW_BLK, POOL_WIDTH), lambda i: (i, 0)),
        out_shape=jax.ShapeDtypeStruct((T_ALL, POOL_WIDTH), BF16),
        scratch_shapes=[pltpu.VMEM((8, STENCIL_ROWS, POOL_CH), F32),
                        pltpu.VMEM((ROW_BLK, POOL_CH), F32)],
        compiler_params=_cp(("arbitrary",)),
        name="pool_mixer",
    )(proj, pool_w, pool_scale.reshape(1, POOL_WIDTH))


CONV_CHUNK = 32


def _conv_kernel(a_ref, g_ref, dw_ref, db_ref, lg_ref, lb_ref, o_ref, xr_ref, y_ref):
    half = CONV_K // 2

    def body(seq_len):
        nseq, stride, _ = _stencil_layout(seq_len)
        pieces = []
        for s in range(nseq):
            rows = slice(s * seq_len, (s + 1) * seq_len)
            pieces.append(a_ref[rows, :] * jax.nn.sigmoid(g_ref[rows, :]))
        _fill_shifted(xr_ref, pieces, seq_len, range(8))
        for s in range(nseq):
            def chunk(c, carry, s=s):
                t0 = c * CONV_CHUNK
                r0 = pl.multiple_of(PAD_LO + s * stride + t0, 8)
                acc = jnp.broadcast_to(db_ref[...], (CONV_CHUNK, CONV_WIDTH))
                for k in range(CONV_K):
                    acc = acc + _tap(xr_ref, r0, k - half, CONV_CHUNK) * dw_ref[k:k + 1, :]
                y_ref[pl.ds(pl.multiple_of(s * seq_len + t0, CONV_CHUNK), CONV_CHUNK), :] = acc
                return carry
            lax.fori_loop(0, seq_len // CONV_CHUNK, chunk, 0, unroll=2)

    _per_path(pl.program_id(0), body)
    y = _layer_norm(y_ref[...]) * lg_ref[...] + lb_ref[...]
    o_ref[...] = (y * jax.nn.sigmoid(y)).astype(BF16)


def _conv_mixer(proj, dw, db, ln_g, ln_b):
    cb = COL_CV // CONV_WIDTH
    vec = pl.BlockSpec((1, CONV_WIDTH), lambda i: (0, 0))
    return pl.pallas_call(
        _conv_kernel,
        grid=(N_ROW_BLK,),
        in_specs=[
            pl.BlockSpec((ROW_BLK, CONV_WIDTH), lambda i: (i, cb)),
            pl.BlockSpec((ROW_BLK, CONV_WIDTH), lambda i: (i, cb + 1)),
            pl.BlockSpec((CONV_K, CONV_WIDTH), lambda i: (0, 0)),
            vec, vec, vec,
        ],
        out_specs=pl.BlockSpec((ROW_BLK, CONV_WIDTH), lambda i: (i, 0)),
        out_shape=jax.ShapeDtypeStruct((T_ALL, CONV_WIDTH), BF16),
        scratch_shapes=[pltpu.VMEM((8, STENCIL_ROWS, CONV_WIDTH), F32),
                        pltpu.VMEM((ROW_BLK, CONV_WIDTH), F32)],
        compiler_params=_cp(("arbitrary",)),
        name="conv_mixer",
    )(proj, proj, dw, db.reshape(1, -1), ln_g.reshape(1, -1), ln_b.reshape(1, -1))


S5_ROWS = 512


def _s5_kernel(*refs, tc, nb, reverse, post):
    u_ref, bq_ref, cq_ref, a_ref, h0_ref = refs[:5]
    k = 5
    if post:
        yf_ref, dskip_ref, wglu_ref = refs[k:k + 3]
        k += 3
    y_ref, hf_ref = refs[k:k + 2]
    y_scr, hs_ref, st_ref = refs[k + 2:]
    j = pl.program_id(0)

    @pl.when(j == 0)
    def _():
        st_ref[...] = h0_ref[...]

    rows = tc * nb
    u_tm = pltpu.einshape("btc->tbc", u_ref[...]).reshape(rows, S5_WIDTH)
    u2 = u_tm.astype(BF16)
    for q in range(S5_Q):
        hs_ref[:, q * 2 * S5_QS:(q + 1) * 2 * S5_QS] = jnp.dot(
            u2[:, q * S5_QU:(q + 1) * S5_QU], bq_ref[q], preferred_element_type=F32)

    lane_w = 4096 // nb
    for q in range(S5_Q):
        for w in range(S5_QS // lane_w):
            re0 = q * 2 * S5_QS + w * lane_w
            im0 = re0 + S5_QS
            a_re = jnp.broadcast_to(a_ref[:, re0:re0 + lane_w], (nb, lane_w))
            a_im = jnp.broadcast_to(a_ref[:, im0:im0 + lane_w], (nb, lane_w))

            def step(t, carry, re0=re0, im0=im0, a_re=a_re, a_im=a_im):
                h_re, h_im = carry
                te = tc - 1 - t if reverse else t
                r0 = pl.multiple_of(te * nb, nb)
                n_re = a_re * h_re - a_im * h_im + hs_ref[pl.ds(r0, nb), re0:re0 + lane_w]
                n_im = a_re * h_im + a_im * h_re + hs_ref[pl.ds(r0, nb), im0:im0 + lane_w]
                hs_ref[pl.ds(r0, nb), re0:re0 + lane_w] = n_re
                hs_ref[pl.ds(r0, nb), im0:im0 + lane_w] = n_im
                return n_re, n_im

            h_re, h_im = lax.fori_loop(
                0, tc, step, (st_ref[:, re0:re0 + lane_w], st_ref[:, im0:im0 + lane_w]), unroll=4)
            st_ref[:, re0:re0 + lane_w] = h_re
            st_ref[:, im0:im0 + lane_w] = h_im

    for q in range(S5_Q):
        yq = jnp.dot(hs_ref[:, q * 2 * S5_QS:(q + 1) * 2 * S5_QS].astype(BF16), cq_ref[q],
                     preferred_element_type=F32)
        y_scr[:, :, q * S5_QU:(q + 1) * S5_QU] = yq.reshape(tc, nb, S5_QU)

    if post:
        y_scr[...] = y_scr[...] + (dskip_ref[...] * u_tm).reshape(tc, nb, S5_WIDTH)
    y_bt = pltpu.einshape("tbc->btc", y_scr[...])
    if post:
        y = jax.nn.gelu(y_bt + yf_ref[...]).reshape(rows, S5_WIDTH)
        gate = jnp.dot(y.astype(BF16), wglu_ref[...].astype(BF16), preferred_element_type=F32)
        y_ref[...] = (y * jax.nn.sigmoid(gate)).astype(BF16).reshape(nb, tc, S5_WIDTH)
    else:
        y_ref[...] = y_bt

    @pl.when(j == pl.num_programs(0) - 1)
    def _():
        hf_ref[...] = st_ref[...]


def _s5_pass(proj3, direction, bq, cq, a_vec, h0, yf=None, d_skip=None, w_glu=None):
    post = yf is not None
    reverse = direction == 1
    cb = COL_S5 // S5_WIDTH
    nb, n = BATCH, SEQ
    tc = S5_ROWS // nb
    nc = n // tc

    def chunk(j):
        return nc - 1 - j if reverse else j

    seq_blk = pl.BlockSpec((nb, tc, S5_WIDTH), lambda j: (0, chunk(j), 0))
    in_specs = [
        pl.BlockSpec((nb, tc, S5_WIDTH), lambda j: (0, chunk(j), cb)),
        pl.BlockSpec((None, S5_Q, S5_QU, 2 * S5_QS), lambda j: (direction, 0, 0, 0)),
        pl.BlockSpec((None, S5_Q, 2 * S5_QS, S5_QU), lambda j: (direction, 0, 0, 0)),
        pl.BlockSpec((None, 1, S5_LANES), lambda j: (direction, 0, 0)),
        pl.BlockSpec((None, nb, S5_LANES), lambda j: (direction, 0, 0)),
    ]
    args = [proj3, bq, cq, a_vec, h0]
    if post:
        in_specs += [seq_blk, pl.BlockSpec((1, S5_WIDTH), lambda j: (0, 0)),
                     pl.BlockSpec((S5_WIDTH, S5_WIDTH), lambda j: (0, 0))]
        args += [yf, d_skip.reshape(1, S5_WIDTH), w_glu]
    return pl.pallas_call(
        functools.partial(_s5_kernel, tc=tc, nb=nb, reverse=reverse, post=post),
        grid=(nc,),
        in_specs=in_specs,
        out_specs=[seq_blk, pl.BlockSpec((nb, S5_LANES), lambda j: (0, 0))],
        out_shape=[jax.ShapeDtypeStruct((nb, n, S5_WIDTH), BF16 if post else F32),
                   jax.ShapeDtypeStruct((nb, S5_LANES), F32)],
        scratch_shapes=[
            pltpu.VMEM((tc, nb, S5_WIDTH), F32),
            pltpu.VMEM((tc * nb, S5_LANES), F32),
            pltpu.VMEM((nb, S5_LANES), F32),
        ],
        compiler_params=_cp(("arbitrary",)),
        name="s5_pass",
    )(*args)


def _s5_mixer(proj3, bq, cq, a_vec, h0, d_skip, w_glu):
    yf, hf_f = _s5_pass(proj3, 0, bq, cq, a_vec, h0)
    y, hf_b = _s5_pass(proj3, 1, bq, cq, a_vec, h0, yf, d_skip, w_glu)
    return y.reshape(-1, S5_WIDTH), jnp.stack([hf_f, hf_b])


LAT_TC = S5_ROWS // (2 * DEC_BATCH)
LAT_NC = DEC_SEQ // LAT_TC


def _s5_lat_kernel(*refs):
    nbat, tc = DEC_BATCH, LAT_TC
    uf_refs = refs[:nbat]
    ub_refs = refs[nbat:2 * nbat]
    (bq_ref, cq_ref, a_ref, h0_ref, yf_ref, yb_ref, hf_ref,
     u_scr, y_scr, hs_ref, hb_ref, st_ref) = refs[2 * nbat:]
    j = pl.program_id(0)

    @pl.when(j == 0)
    def _():
        st_ref[...] = h0_ref[...]

    u_scr[:, 0:nbat, :] = pltpu.einshape("btc->tbc", jnp.stack([r[...] for r in uf_refs], axis=0))
    u_scr[:, nbat:, :] = pltpu.einshape("btc->tbc", jnp.stack([r[...] for r in ub_refs], axis=0))
    rows = 2 * nbat * tc
    u2 = u_scr[...].reshape(rows, S5_WIDTH).astype(BF16)
    is_fwd = (lax.broadcasted_iota(jnp.int32, (rows, 1), 0) & (2 * nbat - 1)) < nbat
    zero = jnp.zeros((rows, S5_QU), BF16)
    for q in range(S5_Q):
        uq = u2[:, q * S5_QU:(q + 1) * S5_QU]
        lhs = jnp.concatenate([jnp.where(is_fwd, uq, zero), jnp.where(is_fwd, zero, uq)], axis=1)
        bu = jnp.dot(lhs, bq_ref[q], preferred_element_type=F32)
        hs_ref[:, q * 2 * S5_QS:(q + 1) * 2 * S5_QS] = bu
        hb_ref[:, q * 2 * S5_QS:(q + 1) * 2 * S5_QS] = bu

    lane_w = S5_QS
    low = lax.broadcasted_iota(jnp.int32, (2 * nbat, lane_w), 0) < nbat
    for q in range(S5_Q):
        re0 = q * 2 * S5_QS
        im0 = re0 + S5_QS
        a_re = a_ref[:, re0:re0 + lane_w]
        a_im = a_ref[:, im0:im0 + lane_w]

        def step(t, carry, re0=re0, im0=im0, a_re=a_re, a_im=a_im):
            h_re, h_im = carry
            ra = pl.multiple_of(t * 2 * nbat, 2 * nbat)
            rb = pl.multiple_of((tc - 1 - t) * 2 * nbat, 2 * nbat)
            n_re = (a_re * h_re - a_im * h_im
                    + jnp.where(low, hs_ref[pl.ds(ra, 2 * nbat), re0:re0 + lane_w],
                                hb_ref[pl.ds(rb, 2 * nbat), re0:re0 + lane_w]))
            n_im = (a_re * h_im + a_im * h_re
                    + jnp.where(low, hs_ref[pl.ds(ra, 2 * nbat), im0:im0 + lane_w],
                                hb_ref[pl.ds(rb, 2 * nbat), im0:im0 + lane_w]))
            hs_ref[pl.ds(ra, 2 * nbat), re0:re0 + lane_w] = n_re
            hs_ref[pl.ds(ra, 2 * nbat), im0:im0 + lane_w] = n_im
            hb_ref[pl.ds(rb, 2 * nbat), re0:re0 + lane_w] = n_re
            hb_ref[pl.ds(rb, 2 * nbat), im0:im0 + lane_w] = n_im
            return n_re, n_im

        h_re, h_im = lax.fori_loop(
            0, tc, step, (st_ref[:, re0:re0 + lane_w], st_ref[:, im0:im0 + lane_w]), unroll=2)
        st_ref[:, re0:re0 + lane_w] = h_re
        st_ref[:, im0:im0 + lane_w] = h_im

    for q in range(S5_Q):
        cols = slice(q * 2 * S5_QS, (q + 1) * 2 * S5_QS)
        hq = jnp.where(is_fwd, hs_ref[:, cols], hb_ref[:, cols]).astype(BF16)
        y2 = jnp.dot(hq, cq_ref[q], preferred_element_type=F32)
        yq = jnp.where(is_fwd, y2[:, :S5_QU], y2[:, S5_QU:])
        y_scr[:, :, q * S5_QU:(q + 1) * S5_QU] = yq.reshape(tc, 2 * nbat, S5_QU)
    yf_ref[...] = pltpu.einshape("tbc->btc", y_scr[:, 0:nbat, :])
    yb_ref[...] = pltpu.einshape("tbc->btc", y_scr[:, nbat:, :])

    @pl.when(j == pl.num_programs(0) - 1)
    def _():
        hf_ref[...] = st_ref[...]


def _s5_lat_post_kernel(yf_ref, yb_ref, u_ref, d_ref, w_ref, o_ref):
    y = jax.nn.gelu(yf_ref[...] + yb_ref[...] + d_ref[...] * u_ref[...])
    gate = jnp.dot(y.astype(BF16), w_ref[...].astype(BF16), preferred_element_type=F32)
    o_ref[...] = (y * jax.nn.sigmoid(gate)).astype(BF16)


def _s5_lat_mixer(proj, bq, cq, a_vec, h0, d_skip, w_glu):
    nbat, tc, nc = DEC_BATCH, LAT_TC, LAT_NC
    cb = COL_S5 // S5_WIDTH
    proj3 = proj.reshape(T_ALL // SEQ, SEQ, MIX_COLS)
    per = SEQ // tc

    def u_spec(b, mirrored):
        def index(j):
            c = nc - 1 - j if mirrored else j
            return (T_CTX // SEQ + b * (DEC_SEQ // SEQ) + c // per, c % per, cb)
        return pl.BlockSpec((None, tc, S5_WIDTH), index)

    bq2 = jnp.concatenate([bq[0], bq[1]], axis=1)
    cq2 = jnp.concatenate([cq[0], cq[1]], axis=2)
    a8 = jnp.broadcast_to(a_vec, (2, nbat, S5_LANES)).reshape(2 * nbat, S5_LANES)
    h08 = h0.reshape(2 * nbat, S5_LANES)
    yf, yb, _ = pl.pallas_call(
        _s5_lat_kernel,
        grid=(nc,),
        in_specs=[u_spec(b, False) for b in range(nbat)] + [u_spec(b, True) for b in range(nbat)] + [
            pl.BlockSpec((S5_Q, 2 * S5_QU, 2 * S5_QS), lambda j: (0, 0, 0)),
            pl.BlockSpec((S5_Q, 2 * S5_QS, 2 * S5_QU), lambda j: (0, 0, 0)),
            pl.BlockSpec((2 * nbat, S5_LANES), lambda j: (0, 0)),
            pl.BlockSpec((2 * nbat, S5_LANES), lambda j: (0, 0)),
        ],
        out_specs=[pl.BlockSpec((nbat, tc, S5_WIDTH), lambda j: (0, j, 0)),
                   pl.BlockSpec((nbat, tc, S5_WIDTH), lambda j: (0, nc - 1 - j, 0)),
                   pl.BlockSpec((2 * nbat, S5_LANES), lambda j: (0, 0))],
        out_shape=[jax.ShapeDtypeStruct((nbat, DEC_SEQ, S5_WIDTH), F32),
                   jax.ShapeDtypeStruct((nbat, DEC_SEQ, S5_WIDTH), F32),
                   jax.ShapeDtypeStruct((2 * nbat, S5_LANES), F32)],
        scratch_shapes=[
            pltpu.VMEM((tc, 2 * nbat, S5_WIDTH), F32),
            pltpu.VMEM((tc, 2 * nbat, S5_WIDTH), F32),
            pltpu.VMEM((tc * 2 * nbat, S5_LANES), F32),
            pltpu.VMEM((tc * 2 * nbat, S5_LANES), F32),
            pltpu.VMEM((2 * nbat, S5_LANES), F32),
        ],
        compiler_params=_cp(("arbitrary",)),
        name="s5_lat",
    )(*([proj3] * (2 * nbat)), bq2, cq2, a8, h08)
    tr = ROW_BLK
    row = pl.BlockSpec((tr, S5_WIDTH), lambda i: (i, 0))
    return pl.pallas_call(
        _s5_lat_post_kernel,
        grid=(T_LAT // tr,),
        in_specs=[row, row, pl.BlockSpec((tr, S5_WIDTH), lambda i: (T_CTX // tr + i, cb)),
                  pl.BlockSpec((1, S5_WIDTH), lambda i: (0, 0)),
                  pl.BlockSpec((S5_WIDTH, S5_WIDTH), lambda i: (0, 0))],
        out_specs=row,
        out_shape=jax.ShapeDtypeStruct((T_LAT, S5_WIDTH), BF16),
        compiler_params=_cp(("arbitrary",)),
        name="s5_lat_post",
    )(yf.reshape(T_LAT, S5_WIDTH), yb.reshape(T_LAT, S5_WIDTH), proj, d_skip.reshape(1, S5_WIDTH), w_glu)


def _s5_params(lam_re, lam_im, log_dt, b_re, b_im, c_re, c_im):
    dt = jnp.exp(log_dt)[..., None]
    x = lam_re * dt
    y = lam_im * dt
    ex = jnp.exp(x)
    abar_re = ex * jnp.cos(y)
    abar_im = ex * jnp.sin(y)
    num_re = jnp.expm1(x) * jnp.cos(y) - 2.0 * jnp.square(jnp.sin(0.5 * y))
    num_im = abar_im
    den = lam_re * lam_re + lam_im * lam_im
    coef_re = (num_re * lam_re + num_im * lam_im) / den
    coef_im = (num_im * lam_re - num_re * lam_im) / den
    bbar_re = coef_re[..., None] * b_re - coef_im[..., None] * b_im
    bbar_im = coef_re[..., None] * b_im + coef_im[..., None] * b_re
    eye = jnp.eye(S5_GQ, dtype=F32)
    bb = jnp.stack([bbar_re, bbar_im]).reshape(2, 2, S5_Q, S5_GQ, S5_STATE, S5_CH)
    bq = jnp.einsum("rdqgpc,gh->dqgcrhp", bb, eye).reshape(2, S5_Q, S5_QU, 2 * S5_QS)
    cc = jnp.stack([c_re, -c_im]).reshape(2, 2, S5_Q, S5_GQ, S5_CH, S5_STATE)
    cq = jnp.einsum("rdqgcp,gh->dqrhpgc", cc, eye).reshape(2, S5_Q, 2 * S5_QS, S5_QU)
    ab = jnp.stack([abar_re, abar_im]).reshape(2, 2, S5_Q, S5_GQ, S5_STATE)
    a_vec = ab.transpose(1, 2, 0, 3, 4).reshape(2, 1, S5_LANES)
    return bq.astype(BF16), cq.astype(BF16), a_vec


def _s5_state_to_lanes(st):
    b = st.shape[0]
    s = st.reshape(b, 2, 2, S5_Q, S5_GQ, S5_STATE).transpose(1, 0, 3, 2, 4, 5)
    return s.reshape(2, b, S5_LANES)


def _s5_lanes_to_state(hl):
    b = hl.shape[1]
    s = hl.reshape(2, b, S5_Q, 2, S5_GQ, S5_STATE).transpose(1, 0, 3, 2, 4, 5)
    return s.reshape(b, 2, 2, S5_GROUPS, S5_STATE)


def _rope(x, cos, sin_signed):
    lane = lax.broadcasted_iota(jnp.int32, x.shape, 1)
    quarter = HEAD_DIM // 4
    partner = jnp.where((lane & quarter) == 0,
                        pltpu.roll(x, HEAD_DIM - quarter, axis=1),
                        pltpu.roll(x, quarter, axis=1))
    return x * cos + partner * sin_signed


def _nt_dot(a, b):
    return lax.dot_general(a, b, (((1,), (1,)), ((), ())), preferred_element_type=F32)


def _stack_heads(q_ref, rope=None):
    parts = []
    for r in range(REP):
        q = q_ref[:, r * HEAD_DIM:(r + 1) * HEAD_DIM]
        if rope is not None:
            q = _rope(q, *rope)
        parts.append(q.astype(BF16))
    return jnp.concatenate(parts, axis=0)


def _with_ones(v):
    return jnp.concatenate([v.astype(BF16), jnp.ones(v.shape, BF16)], axis=1)


def _sink_column(sink_ref, h0, rows):
    return jnp.concatenate(
        [jnp.broadcast_to(sink_ref[pl.ds(h0 + r, 1), 0:1], (rows, 1)) for r in range(REP)], axis=0)


def _ctx_attn_kernel(q_ref, k_ref, v_ref, sink_ref, o_ref, ko_ref, vo_ref):
    h0 = pl.program_id(1) * REP
    k = k_ref[...]
    v = v_ref[...]
    ko_ref[...] = k
    vo_ref[...] = v
    q = _stack_heads(q_ref)
    s = _nt_dot(q, k.astype(BF16)) * (HEAD_DIM ** -0.5)
    sink = _sink_column(sink_ref, h0, SEQ)
    m = jnp.maximum(jnp.max(s, axis=-1, keepdims=True), sink)
    p = jnp.exp(s - m)
    acc = jnp.dot(p.astype(BF16), _with_ones(v), preferred_element_type=F32)
    o = acc[:, :HEAD_DIM] / (acc[:, HEAD_DIM:HEAD_DIM + 1] + jnp.exp(sink - m))
    for r in range(REP):
        o_ref[:, r * HEAD_DIM:(r + 1) * HEAD_DIM] = o[r * SEQ:(r + 1) * SEQ].astype(BF16)


def _ctx_attention(proj, sink_tile):
    qw = REP * HEAD_DIM
    kv_out = pl.BlockSpec((SEQ, HEAD_DIM), lambda b, h: (b, h))
    return pl.pallas_call(
        _ctx_attn_kernel,
        grid=(BATCH, N_KV_HEADS),
        in_specs=[
            pl.BlockSpec((SEQ, qw), lambda b, h: (b, COL_Q // qw + h)),
            pl.BlockSpec((SEQ, HEAD_DIM), lambda b, h: (b, COL_K // HEAD_DIM + h)),
            pl.BlockSpec((SEQ, HEAD_DIM), lambda b, h: (b, COL_V // HEAD_DIM + h)),
            pl.BlockSpec((N_HEADS, HEAD_DIM), lambda b, h: (0, 0)),
        ],
        out_specs=[pl.BlockSpec((SEQ, qw), lambda b, h: (b, h)), kv_out, kv_out],
        out_shape=[jax.ShapeDtypeStruct((T_CTX, ATTN_WIDTH), BF16),
                   jax.ShapeDtypeStruct((T_CTX, KV_WIDTH), F32),
                   jax.ShapeDtypeStruct((T_CTX, KV_WIDTH), F32)],
        compiler_params=_cp(("arbitrary", "arbitrary")),
        name="ctx_attention",
    )(proj, proj, proj, sink_tile)


LAT_TQ = 256
LAT_KWIN = LAT_TQ + 2 * WINDOW


def _lat_attn_kernel(q_ref, k_ref, v_ref, kc_ref, vc_ref, cq_ref, sq_ref, ck_ref, sk_ref,
                     sink_ref, o_ref):
    h0 = pl.program_id(1) * REP
    q0 = pl.program_id(2) * LAT_TQ
    k0 = pl.multiple_of(jnp.clip(q0 - WINDOW, 0, DEC_SEQ - LAT_KWIN), WINDOW)
    win = pl.ds(k0, LAT_KWIN)
    kb = _rope(k_ref[win, :], ck_ref[win, :], sk_ref[win, :]).astype(BF16)
    rows = REP * LAT_TQ
    row = lax.broadcasted_iota(jnp.int32, (rows, LAT_KWIN), 0)
    qpos = q0 + (row & (LAT_TQ - 1))
    kpos = k0 + lax.broadcasted_iota(jnp.int32, (rows, LAT_KWIN), 1)
    mask = jnp.abs(qpos - kpos) <= WINDOW
    scale = HEAD_DIM ** -0.5
    q = _stack_heads(q_ref, (cq_ref[...], sq_ref[...]))
    s_lat = jnp.where(mask, _nt_dot(q, kb) * scale, NEG_INF)
    s_ctx = _nt_dot(q, kc_ref[...].astype(BF16)) * scale
    sink = _sink_column(sink_ref, h0, LAT_TQ)
    m = jnp.maximum(jnp.maximum(jnp.max(s_lat, axis=-1, keepdims=True),
                                jnp.max(s_ctx, axis=-1, keepdims=True)), sink)
    p_lat = jnp.exp(s_lat - m).astype(BF16)
    p_ctx = jnp.exp(s_ctx - m).astype(BF16)
    acc = (jnp.dot(p_lat, _with_ones(v_ref[win, :]), preferred_element_type=F32)
           + jnp.dot(p_ctx, _with_ones(vc_ref[...]), preferred_element_type=F32))
    o = acc[:, :HEAD_DIM] / (acc[:, HEAD_DIM:HEAD_DIM + 1] + jnp.exp(sink - m))
    for r in range(REP):
        o_ref[:, r * HEAD_DIM:(r + 1) * HEAD_DIM] = o[r * LAT_TQ:(r + 1) * LAT_TQ].astype(BF16)


def _lat_attention(proj, cache_k_l, cache_v_l, rope_cos, rope_sin, sink_tile):
    tq = LAT_TQ
    nq = DEC_SEQ // tq
    qw = REP * HEAD_DIM
    row0_q = T_CTX // tq
    row0_k = T_CTX // DEC_SEQ
    tab_q = pl.BlockSpec((tq, HEAD_DIM), lambda b, h, i: (i, 0))
    tab_k = pl.BlockSpec((DEC_SEQ, HEAD_DIM), lambda b, h, i: (0, 0))
    return pl.pallas_call(
        _lat_attn_kernel,
        grid=(DEC_BATCH, N_KV_HEADS, nq),
        in_specs=[
            pl.BlockSpec((tq, qw), lambda b, h, i: (row0_q + b * nq + i, COL_Q // qw + h)),
            pl.BlockSpec((DEC_SEQ, HEAD_DIM), lambda b, h, i: (row0_k + b, COL_K // HEAD_DIM + h)),
            pl.BlockSpec((DEC_SEQ, HEAD_DIM), lambda b, h, i: (row0_k + b, COL_V // HEAD_DIM + h)),
            pl.BlockSpec((None, PAST_LEN, HEAD_DIM), lambda b, h, i: (b, 0, h)),
            pl.BlockSpec((None, PAST_LEN, HEAD_DIM), lambda b, h, i: (b, 0, h)),
            tab_q, tab_q, tab_k, tab_k,
            pl.BlockSpec((N_HEADS, HEAD_DIM), lambda b, h, i: (0, 0)),
        ],
        out_specs=pl.BlockSpec((tq, qw), lambda b, h, i: (b * nq + i, h)),
        out_shape=jax.ShapeDtypeStruct((T_LAT, ATTN_WIDTH), BF16),
        compiler_params=_cp(("arbitrary", "arbitrary", "arbitrary")),
        name="lat_attention",
    )(proj, proj, proj, cache_k_l, cache_v_l, rope_cos, rope_sin, rope_cos, rope_sin, sink_tile)


def _rope_tables():
    t = jnp.arange(DEC_SEQ)
    row, col = t // GRID_W, t % GRID_W
    quarter = HEAD_DIM // 4
    inv = ROPE_BASE ** (-jnp.arange(quarter, dtype=F32) / quarter)
    lane = jnp.arange(HEAD_DIM)
    pos = jnp.where(lane[None, :] < HEAD_DIM // 2, row[:, None], col[:, None]).astype(F32)
    ang = pos * inv[lane % quarter][None, :]
    sign = jnp.where((lane & quarter) == 0, -1.0, 1.0).astype(F32)
    return jnp.cos(ang), jnp.sin(ang) * sign[None, :]


def _merge_kernel(yp_ref, ysc_ref, ysl_ref, yac_ref, yal_ref, yc_ref, g0_ref, g1_ref, g2_ref,
                  g3_ref, wp_ref, ws_ref, wa_ref, wc_ref, o_ref):
    is_ctx = pl.program_id(1) < N_CTX_BLK
    y_s5 = jnp.where(is_ctx, ysc_ref[...], ysl_ref[...])
    attn = jnp.where(is_ctx, yac_ref[...], yal_ref[...])
    acc = None
    for y, g_ref, w_ref in ((yp_ref[...], g0_ref, wp_ref), (y_s5, g1_ref, ws_ref),
                            (attn, g2_ref, wa_ref), (yc_ref[...], g3_ref, wc_ref)):
        t = g_ref[...].astype(F32) * jnp.dot(y, w_ref[...].astype(BF16), preferred_element_type=F32)
        acc = t if acc is None else acc + t
    o_ref[...] = acc.astype(BF16)


def _merge(gates, y_pool, y_s5_ctx, y_s5_lat, attn_ctx, attn_lat, y_conv, layer,
           w_pool, w_s5, w_attn, w_conv):
    tm, tn = ROW_BLK, 512
    per = D_MODEL // tn

    def act(width):
        return pl.BlockSpec((tm, width), lambda j, i: (i, 0))

    def ctx_part(width):
        return pl.BlockSpec((tm, width), lambda j, i: (jnp.minimum(i, N_CTX_BLK - 1), 0))

    def lat_part(width):
        return pl.BlockSpec((tm, width), lambda j, i: (jnp.maximum(i - N_CTX_BLK, 0), 0))

    def gate(branch):
        return pl.BlockSpec((tm, tn), lambda j, i: (i, branch * per + j))

    def wt(width):
        return pl.BlockSpec((None, width, tn), lambda j, i: (layer, 0, j))

    return pl.pallas_call(
        _merge_kernel,
        grid=(D_MODEL // tn, T_ALL // tm),
        in_specs=[act(POOL_WIDTH), ctx_part(S5_WIDTH), lat_part(S5_WIDTH),
                  ctx_part(ATTN_WIDTH), lat_part(ATTN_WIDTH), act(CONV_WIDTH),
                  gate(0), gate(1), gate(2), gate(3),
                  wt(POOL_WIDTH), wt(S5_WIDTH), wt(ATTN_WIDTH), wt(CONV_WIDTH)],
        out_specs=pl.BlockSpec((tm, tn), lambda j, i: (i, j)),
        out_shape=jax.ShapeDtypeStruct((T_ALL, D_MODEL), BF16),
        compiler_params=_cp(("arbitrary", "arbitrary")),
        name="merge",
    )(y_pool, y_s5_ctx, y_s5_lat, attn_ctx, attn_lat, y_conv, gates, gates, gates, gates,
      w_pool, w_s5, w_attn, w_conv)


FFN_TILE = 256
N_FFN_TILE = D_FF // FFN_TILE


def _ffn_kernel(h_ref, wg_ref, wv_ref, dwg_ref, dwv_ref, dbg_ref, dbv_ref, wd_ref, o_ref,
                ug_ref, uv_ref, acc_ref):
    i = pl.program_id(0)
    f = pl.program_id(1)

    def up(slot):
        h = h_ref[...]
        ug_ref[slot] = jnp.dot(h, wg_ref[...].astype(BF16), preferred_element_type=F32)
        uv_ref[slot] = jnp.dot(h, wv_ref[...].astype(BF16), preferred_element_type=F32)

    def down(slot):
        pos, seq_len = _seq_pos(i, ROW_BLK)
        first = pos == 0
        last = pos == seq_len - 1

        def conv3(u, dw_ref, db_ref):
            prev = jnp.where(first, 0.0, _shift_rows(u, -1))
            nxt = jnp.where(last, 0.0, _shift_rows(u, 1))
            return prev * dw_ref[0:1, :] + u * dw_ref[1:2, :] + nxt * dw_ref[2:3, :] + db_ref[...]

        gt = conv3(ug_ref[slot], dwg_ref, dbg_ref)
        val = conv3(uv_ref[slot], dwv_ref, dbv_ref)
        act = (gt * jax.nn.sigmoid(gt) * val).astype(BF16)
        acc_ref[...] += jnp.dot(act, wd_ref[...].astype(BF16), preferred_element_type=F32)

    @pl.when(f == 0)
    def _():
        acc_ref[...] = jnp.zeros_like(acc_ref)
        up(0)

    for par in (0, 1):
        @pl.when(jnp.logical_and(jnp.logical_and(f > 0, f < N_FFN_TILE), f % 2 == par))
        def _(par=par):
            up(par)
            down(1 - par)

    @pl.when(f == N_FFN_TILE)
    def _():
        down((N_FFN_TILE - 1) % 2)
        o_ref[...] = acc_ref[...].astype(o_ref.dtype)


def _conv_ffn(h, layer, w_up, dw, db, w_down):
    tf, nf = FFN_TILE, N_FFN_TILE
    db3 = db.reshape(DEPTH, 1, 2 * D_FF)

    def cur(f):
        return jnp.minimum(f, nf - 1)

    def prv(f):
        return jnp.maximum(f - 1, 0)

    return pl.pallas_call(
        _ffn_kernel,
        grid=(N_ROW_BLK, nf + 1),
        in_specs=[
            pl.BlockSpec((ROW_BLK, D_MODEL), lambda i, f: (i, 0)),
            pl.BlockSpec((None, D_MODEL, tf), lambda i, f: (layer, 0, cur(f))),
            pl.BlockSpec((None, D_MODEL, tf), lambda i, f: (layer, 0, nf + cur(f))),
            pl.BlockSpec((None, FFN_CONV_K, tf), lambda i, f: (layer, 0, prv(f))),
            pl.BlockSpec((None, FFN_CONV_K, tf), lambda i, f: (layer, 0, nf + prv(f))),
            pl.BlockSpec((None, 1, tf), lambda i, f: (layer, 0, prv(f))),
            pl.BlockSpec((None, 1, tf), lambda i, f: (layer, 0, nf + prv(f))),
            pl.BlockSpec((None, tf, D_MODEL), lambda i, f: (layer, prv(f), 0)),
        ],
        out_specs=pl.BlockSpec((ROW_BLK, D_MODEL), lambda i, f: (i, 0)),
        out_shape=jax.ShapeDtypeStruct((T_ALL, D_MODEL), BF16),
        scratch_shapes=[pltpu.VMEM((2, ROW_BLK, tf), F32), pltpu.VMEM((2, ROW_BLK, tf), F32),
                        pltpu.VMEM((ROW_BLK, D_MODEL), F32)],
        compiler_params=_cp(("arbitrary", "arbitrary")),
        name="conv_ffn",
    )(h, w_up, w_up, dw, dw, db3, db3, w_down)


def kernel(x_prompt, x_sample, cache_k, cache_v, state_s5, c, c_ctx, w_ada, b_ada, w_in, pool_w, pool_scale, s5_lambda_re, s5_lambda_im, s5_log_dt, s5_b_re, s5_b_im, s5_c_re, s5_c_im, s5_d, s5_w_glu, attn_sink, conv_dw, conv_db, conv_ln_g, conv_ln_b, w_br_pool, w_br_s5, w_br_attn, w_br_conv, w_out, ln1_g, ln1_b, ffn_w_up, ffn_dw, ffn_db, ffn_w_down, ln2_g, ln2_b):
    x_parts = (x_prompt.reshape(T_CTX, D_MODEL), x_sample.reshape(T_LAT, D_MODEL))
    cvec = jnp.concatenate(
        [c_ctx[None, :], c, jnp.zeros((MOD_ROWS - 1 - DEC_BATCH, D_MODEL), F32)], axis=0)
    mod = _mod_table(cvec, w_ada, b_ada).reshape(DEPTH, MOD_ROWS, 6, D_MODEL)
    rope_cos, rope_sin = _rope_tables()

    ks_out, vs_out, ss_out = [], [], []
    h = _ln_mod(x_parts[0], x_parts[1], mod[0], 0, 1)
    for l in range(DEPTH):
        proj = _matmul(h, w_in, l, 0, MIX_COLS)
        gates = _matmul(h, w_in, l, COL_GATE, N_BRANCH * D_MODEL, out_dtype=BF16, gate=True)

        y_pool = _pool_mixer(proj, pool_w[l], pool_scale[l])
        y_conv = _conv_mixer(proj, conv_dw[l], conv_db[l], conv_ln_g[l], conv_ln_b[l])

        bq, cq, a_vec = _s5_params(s5_lambda_re[l], s5_lambda_im[l], s5_log_dt[l], s5_b_re[l],
                                   s5_b_im[l], s5_c_re[l], s5_c_im[l])
        proj3 = proj.reshape(T_ALL // SEQ, SEQ, MIX_COLS)
        h0_ctx = jnp.zeros((2, BATCH, S5_LANES), F32)
        y_s5_ctx, hf_ctx = _s5_mixer(proj3, bq, cq, a_vec, h0_ctx, s5_d[l], s5_w_glu[l])
        y_s5_lat = _s5_lat_mixer(proj, bq, cq, a_vec, _s5_state_to_lanes(state_s5[:, l]),
                                 s5_d[l], s5_w_glu[l])

        sink_tile = jnp.broadcast_to(attn_sink[l][:, None], (N_HEADS, HEAD_DIM))
        attn_ctx, k_new, v_new = _ctx_attention(proj, sink_tile)
        attn_lat = _lat_attention(proj, cache_k[:, l].reshape(DEC_BATCH, PAST_LEN, KV_WIDTH),
                                  cache_v[:, l].reshape(DEC_BATCH, PAST_LEN, KV_WIDTH),
                                  rope_cos, rope_sin, sink_tile)

        merged = _merge(gates, y_pool, y_s5_ctx, y_s5_lat, attn_ctx, attn_lat, y_conv, l,
                        w_br_pool, w_br_s5, w_br_attn, w_br_conv)
        mixed = _matmul(merged, w_out, l, out_dtype=BF16)
        x, h2 = _res_ln(x_parts, mixed, mod[l], 2, ln1_g[l], ln1_b[l], mod[l], 3, 4)
        f = _conv_ffn(h2, l, ffn_w_up, ffn_dw, ffn_db, ffn_w_down)
        if l + 1 < DEPTH:
            x, h = _res_ln((x,), f, mod[l], 5, ln2_g[l], ln2_b[l], mod[l + 1], 0, 1)
            x_parts = (x,)
        else:
            y_prompt, y_sample = _res_ln((x,), f, mod[l], 5, ln2_g[l], ln2_b[l])

        ks_out.append(k_new.reshape(BATCH, SEQ, N_KV_HEADS, HEAD_DIM))
        vs_out.append(v_new.reshape(BATCH, SEQ, N_KV_HEADS, HEAD_DIM))
        ss_out.append(_s5_lanes_to_state(hf_ctx))

    return (y_prompt.reshape(BATCH, SEQ, D_MODEL), y_sample.reshape(DEC_BATCH, DEC_SEQ, D_MODEL),
            jnp.stack(ks_out, axis=1), jnp.stack(vs_out, axis=1), jnp.stack(ss_out, axis=1))
```

```python
import functools

import jax
import jax.numpy as jnp
from jax import lax
from jax.experimental import pallas as pl
from jax.experimental.pallas import tpu as pltpu

F32 = jnp.float32
BF16 = jnp.bfloat16

D_MODEL = 2048
BATCH = 16
SEQ = 256
DEPTH = 2
DEC_BATCH = 4
DEC_SEQ = 1024
PAST_LEN = 512
GRID_W = 64
POOL_WIDTH = 512
POOL_GROUPS = 4
POOL_CH = POOL_WIDTH // POOL_GROUPS
POOL_WINDOWS = (2, 4, 8, 16)
S5_WIDTH = 512
S5_CH = 16
S5_GROUPS = S5_WIDTH // S5_CH
S5_STATE = 64
N_HEADS = 8
N_KV_HEADS = 2
HEAD_DIM = 128
REP = N_HEADS // N_KV_HEADS
ATTN_WIDTH = N_HEADS * HEAD_DIM
KV_WIDTH = N_KV_HEADS * HEAD_DIM
WINDOW = 128
ROPE_BASE = 10000.0
NEG_INF = -1e30
CONV_WIDTH = 512
CONV_K = 31
N_BRANCH = 4
D_FF = 5632
FFN_CONV_K = 3
ALPHA = (2.0 * DEPTH) ** 0.25
LN_EPS = 1e-5
MIX_COLS = POOL_WIDTH + S5_WIDTH + ATTN_WIDTH + 2 * KV_WIDTH + 2 * CONV_WIDTH
N_IN = MIX_COLS + N_BRANCH * D_MODEL

COL_POOL = 0
COL_S5 = POOL_WIDTH
COL_Q = COL_S5 + S5_WIDTH
COL_K = COL_Q + ATTN_WIDTH
COL_V = COL_K + KV_WIDTH
COL_CV = COL_V + KV_WIDTH
COL_GATE = MIX_COLS

T_CTX = BATCH * SEQ
T_LAT = DEC_BATCH * DEC_SEQ
T_ALL = T_CTX + T_LAT
ROW_BLK = 1024
N_ROW_BLK = T_ALL // ROW_BLK
N_CTX_BLK = T_CTX // ROW_BLK
MOD_ROWS = 8
LN_ROWS = 512

S5_Q = 4
S5_GQ = S5_GROUPS // S5_Q
S5_QU = S5_GQ * S5_CH
S5_QS = S5_GQ * S5_STATE
S5_LANES = 2 * S5_GROUPS * S5_STATE

VMEM_LIMIT = 56 * 1024 * 1024


def _cp(sem):
    return pltpu.CompilerParams(dimension_semantics=sem, vmem_limit_bytes=VMEM_LIMIT)


def _mod_row(row_block, rows_per_block):
    lat_batch = (row_block * rows_per_block) // DEC_SEQ - T_CTX // DEC_SEQ
    return jnp.maximum(lat_batch + 1, 0)


def _seq_pos(row_block, rows):
    r = lax.broadcasted_iota(jnp.int32, (rows, 1), 0)
    seq_len = jnp.where(row_block < N_CTX_BLK, SEQ, DEC_SEQ)
    return r & (seq_len - 1), seq_len


def _layer_norm(x):
    mu = jnp.mean(x, axis=-1, keepdims=True)
    xc = x - mu
    var = jnp.mean(xc * xc, axis=-1, keepdims=True)
    return xc * lax.rsqrt(var + LN_EPS)


def _shift_rows(x, d):
    n = x.shape[0]
    return pltpu.roll(x, (-d) % n, axis=0)


def _split_bf16(x):
    hi = x.astype(BF16)
    return hi, (x - hi.astype(F32)).astype(BF16)


MOD_TN = 2048
MOD_KC = 256


def _mod_kernel(c_ref, w_ref, b_ref, o_ref):
    c = c_ref[...]
    a_hi, a_lo = _split_bf16(c * jax.nn.sigmoid(c))
    acc = jnp.broadcast_to(b_ref[0], (MOD_ROWS, MOD_TN))
    for k0 in range(0, D_MODEL, MOD_KC):
        w_hi, w_lo = _split_bf16(w_ref[0, k0:k0 + MOD_KC, :])
        ah = a_hi[:, k0:k0 + MOD_KC]
        acc = acc + jnp.dot(ah, w_hi, preferred_element_type=F32)
        acc = acc + jnp.dot(a_lo[:, k0:k0 + MOD_KC], w_hi, preferred_element_type=F32)
        acc = acc + jnp.dot(ah, w_lo, preferred_element_type=F32)
    o_ref[0] = acc


def _mod_table(cvec, w_ada, b_ada):
    tn = MOD_TN
    n6 = 6 * D_MODEL
    return pl.pallas_call(
        _mod_kernel,
        grid=(DEPTH, n6 // tn),
        in_specs=[
            pl.BlockSpec((MOD_ROWS, D_MODEL), lambda l, j: (0, 0)),
            pl.BlockSpec((1, D_MODEL, tn), lambda l, j: (l, 0, j)),
            pl.BlockSpec((1, 1, tn), lambda l, j: (l, 0, j)),
        ],
        out_specs=pl.BlockSpec((1, MOD_ROWS, tn), lambda l, j: (l, 0, j)),
        out_shape=jax.ShapeDtypeStruct((DEPTH, MOD_ROWS, n6), F32),
        compiler_params=_cp(("arbitrary", "arbitrary")),
        name="mod_table",
    )(cvec, w_ada, b_ada.reshape(DEPTH, 1, n6))


N_CTX_LN = T_CTX // LN_ROWS


def _row_spec():
    return pl.BlockSpec((LN_ROWS, D_MODEL), lambda i: (i, 0))


def _ctx_part_spec():
    return pl.BlockSpec((LN_ROWS, D_MODEL), lambda i: (jnp.minimum(i, N_CTX_LN - 1), 0))


def _lat_part_spec():
    return pl.BlockSpec((LN_ROWS, D_MODEL), lambda i: (jnp.maximum(i - N_CTX_LN, 0), 0))


def _mod_spec():
    return pl.BlockSpec((1, 6, D_MODEL), lambda i: (_mod_row(i, LN_ROWS), 0, 0))


def _read_rows(refs):
    if len(refs) == 1:
        return refs[0][...]
    return jnp.where(pl.program_id(0) < N_CTX_LN, refs[0][...], refs[1][...])


def _ln_mod_kernel(xc_ref, xl_ref, mod_ref, h_ref, *, shift_i, scale_i):
    y = _layer_norm(_read_rows((xc_ref, xl_ref)))
    scale = mod_ref[0, scale_i:scale_i + 1, :]
    shift = mod_ref[0, shift_i:shift_i + 1, :]
    h_ref[...] = (y * (1.0 + scale) + shift).astype(BF16)


def _ln_mod(x_ctx, x_lat, mod_l, shift_i, scale_i):
    return pl.pallas_call(
        functools.partial(_ln_mod_kernel, shift_i=shift_i, scale_i=scale_i),
        grid=(T_ALL // LN_ROWS,),
        in_specs=[_ctx_part_spec(), _lat_part_spec(), _mod_spec()],
        out_specs=_row_spec(),
        out_shape=jax.ShapeDtypeStruct((T_ALL, D_MODEL), BF16),
        compiler_params=_cp(("arbitrary",)),
        name="ln_mod",
    )(x_ctx, x_lat, mod_l)


def _res_ln_kernel(*refs, n_x, gate_i, next_shift_i, next_scale_i, split_out):
    x_refs = refs[:n_x]
    y_ref, mod_ref, g_ref, b_ref = refs[n_x:n_x + 4]
    rest = refs[n_x + 4:]
    gate = mod_ref[0, gate_i:gate_i + 1, :]
    z = ALPHA * _read_rows(x_refs) + gate * y_ref[...].astype(F32)
    xn = _layer_norm(z) * g_ref[...] + b_ref[...]
    if split_out:
        xc_ref, xl_ref = rest

        @pl.when(pl.program_id(0) < N_CTX_LN)
        def _():
            xc_ref[...] = xn

        @pl.when(pl.program_id(0) >= N_CTX_LN)
        def _():
            xl_ref[...] = xn
    else:
        nmod_ref, xo_ref, h_ref = rest
        xo_ref[...] = xn
        scale = nmod_ref[0, next_scale_i:next_scale_i + 1, :]
        shift = nmod_ref[0, next_shift_i:next_shift_i + 1, :]
        h_ref[...] = (_layer_norm(xn) * (1.0 + scale) + shift).astype(BF16)


def _res_ln(x_parts, y, mod_l, gate_i, g, b, next_mod=None, next_shift_i=None, next_scale_i=None):
    vec = pl.BlockSpec((1, D_MODEL), lambda i: (0, 0))
    x_specs = [_row_spec()] if len(x_parts) == 1 else [_ctx_part_spec(), _lat_part_spec()]
    in_specs = x_specs + [_row_spec(), _mod_spec(), vec, vec]
    args = list(x_parts) + [y, mod_l, g.reshape(1, D_MODEL), b.reshape(1, D_MODEL)]
    split_out = next_mod is None
    if split_out:
        out_specs = [_ctx_part_spec(), _lat_part_spec()]
        out_shape = [jax.ShapeDtypeStruct((T_CTX, D_MODEL), F32),
                     jax.ShapeDtypeStruct((T_LAT, D_MODEL), F32)]
    else:
        in_specs.append(_mod_spec())
        args.append(next_mod)
        out_specs = [_row_spec(), _row_spec()]
        out_shape = [jax.ShapeDtypeStruct((T_ALL, D_MODEL), F32),
                     jax.ShapeDtypeStruct((T_ALL, D_MODEL), BF16)]
    return pl.pallas_call(
        functools.partial(_res_ln_kernel, n_x=len(x_parts), gate_i=gate_i,
                          next_shift_i=next_shift_i, next_scale_i=next_scale_i,
                          split_out=split_out),
        grid=(T_ALL // LN_ROWS,),
        in_specs=in_specs,
        out_specs=out_specs,
        out_shape=out_shape,
        compiler_params=_cp(("arbitrary",)),
        name="res_ln",
    )(*args)


def _matmul_kernel(a_ref, w_ref, o_ref, *, gate):
    y = jnp.dot(a_ref[...], w_ref[...].astype(BF16), preferred_element_type=F32)
    if gate:
        y = jax.nn.sigmoid(y)
    o_ref[...] = y.astype(o_ref.dtype)


def _matmul(a, w, layer, col0=0, ncols=None, out_dtype=F32, gate=False):
    tm, tn = 2048, 512
    m, k = a.shape
    ncols = w.shape[2] - col0 if ncols is None else ncols
    assert col0 % tn == 0 and ncols % tn == 0 and m % tm == 0
    jb = col0 // tn
    return pl.pallas_call(
        functools.partial(_matmul_kernel, gate=gate),
        grid=(m // tm, ncols // tn),
        in_specs=[
            pl.BlockSpec((tm, k), lambda i, j: (i, 0)),
            pl.BlockSpec((None, k, tn), lambda i, j: (layer, 0, jb + j)),
        ],
        out_specs=pl.BlockSpec((tm, tn), lambda i, j: (i, j)),
        out_shape=jax.ShapeDtypeStruct((m, ncols), out_dtype),
        compiler_params=_cp(("arbitrary", "arbitrary")),
        name="matmul",
    )(a, w)


PAD_LO = 16
PAD_SEQ = 32
STENCIL_ROWS = PAD_LO + (ROW_BLK // SEQ) * (SEQ + PAD_SEQ)


def _stencil_layout(seq_len):
    nseq = ROW_BLK // seq_len
    stride = seq_len + PAD_SEQ
    return nseq, stride, PAD_LO + nseq * stride


def _fill_shifted(xr_ref, pieces, seq_len, residues):
    nseq, stride, rows = _stencil_layout(seq_len)
    width = xr_ref.shape[-1]
    xr_ref[0, 0:PAD_LO, :] = jnp.zeros((PAD_LO, width), F32)
    for s in range(nseq):
        b0 = PAD_LO + s * stride
        xr_ref[0, b0:b0 + seq_len, :] = pieces[s]
        xr_ref[0, b0 + seq_len:b0 + stride, :] = jnp.zeros((PAD_SEQ, width), F32)
    x0 = xr_ref[0, 0:rows, :]
    for r in residues:
        if r:
            xr_ref[r, 0:rows, :] = pltpu.roll(x0, rows - r, axis=0)


def _tap(xr_ref, r0, d, rows):
    r = d % 8
    return xr_ref[r, pl.ds(r0 + (d - r), rows), :]


def _per_path(i, body):
    @pl.when(i < N_CTX_BLK)
    def _():
        body(SEQ)

    @pl.when(i >= N_CTX_BLK)
    def _():
        body(DEC_SEQ)


POOL_CHUNK = 64


def _pool_kernel(a_ref, w_ref, s_ref, o_ref, xr_ref, pooled_ref):
    def body(seq_len):
        nseq, stride, _ = _stencil_layout(seq_len)
        for g, win in enumerate(POOL_WINDOWS):
            left = win // 2
            right = win - 1 - left
            cols = slice(g * POOL_CH, (g + 1) * POOL_CH)
            offsets = range(-left, right + 1)
            _fill_shifted(xr_ref, [a_ref[s * seq_len:(s + 1) * seq_len, cols] for s in range(nseq)],
                          seq_len, sorted({d % 8 for d in offsets}))
            for s in range(nseq):
                def chunk(c, carry, s=s, offsets=offsets, left=left, right=right):
                    t0 = c * POOL_CHUNK
                    r0 = pl.multiple_of(PAD_LO + s * stride + t0, 8)
                    x = _tap(xr_ref, r0, 0, POOL_CHUNK)
                    acc = x
                    for d in offsets:
                        if d:
                            acc = acc + _tap(xr_ref, r0, d, POOL_CHUNK)
                    pos = t0 + lax.broadcasted_iota(jnp.int32, (POOL_CHUNK, 1), 0)
                    cnt = jnp.minimum(pos + right + 1, seq_len) - jnp.maximum(pos - left, 0)
                    pooled_ref[pl.ds(pl.multiple_of(s * seq_len + t0, 8), POOL_CHUNK), :] = (
                        acc / cnt.astype(F32) - x)
                    return carry
                lax.fori_loop(0, seq_len // POOL_CHUNK, chunk, 0)
            mixed = jnp.dot(pooled_ref[...].astype(BF16), w_ref[g].astype(BF16),
                            preferred_element_type=F32)
            o_ref[:, cols] = (mixed * s_ref[:, cols]).astype(BF16)

    _per_path(pl.program_id(0), body)


def _pool_mixer(proj, pool_w, pool_scale):
    return pl.pallas_call(
        _pool_kernel,
        grid=(N_ROW_BLK,),
        in_specs=[
            pl.BlockSpec((ROW_BLK, POOL_WIDTH), lambda i: (i, COL_POOL // POOL_WIDTH)),
            pl.BlockSpec((POOL_GROUPS, POOL_CH, POOL_CH), lambda i: (0, 0, 0)),
            pl.BlockSpec((1, POOL_WIDTH), lambda i: (0, 0)),
        ],
        out_specs=pl.BlockSpec((ROW_BLK, POOL_WIDTH), lambda i: (i, 0)),
        out_shape=jax.ShapeDtypeStruct((T_ALL, POOL_WIDTH), BF16),
        scratch_shapes=[pltpu.VMEM((8, STENCIL_ROWS, POOL_CH), F32),
                        pltpu.VMEM((ROW_BLK, POOL_CH), F32)],
        compiler_params=_cp(("arbitrary",)),
        name="pool_mixer",
    )(proj, pool_w, pool_scale.reshape(1, POOL_WIDTH))


CONV_CHUNK = 32


def _conv_kernel(a_ref, g_ref, dw_ref, db_ref, lg_ref, lb_ref, o_ref, xr_ref, y_ref):
    half = CONV_K // 2

    def body(seq_len):
        nseq, stride, _ = _stencil_layout(seq_len)
        pieces = []
        for s in range(nseq):
            rows = slice(s * seq_len, (s + 1) * seq_len)
            pieces.append(a_ref[rows, :] * jax.nn.sigmoid(g_ref[rows, :]))
        _fill_shifted(xr_ref, pieces, seq_len, range(8))
        for s in range(nseq):
            def chunk(c, carry, s=s):
                t0 = c * CONV_CHUNK
                r0 = pl.multiple_of(PAD_LO + s * stride + t0, 8)
                groups = CONV_CHUNK // 8
                acc = jnp.broadcast_to(db_ref[...], (CONV_CHUNK, CONV_WIDTH)).reshape(
                    groups, 8, CONV_WIDTH)
                for k in range(CONV_K):
                    tap = _tap(xr_ref, r0, k - half, CONV_CHUNK).reshape(groups, 8, CONV_WIDTH)
                    acc = acc + tap * dw_ref[k]
                y_ref[pl.ds(pl.multiple_of(s * seq_len + t0, CONV_CHUNK), CONV_CHUNK), :] = (
                    acc.reshape(CONV_CHUNK, CONV_WIDTH))
                return carry
            lax.fori_loop(0, seq_len // CONV_CHUNK, chunk, 0, unroll=2)

    _per_path(pl.program_id(0), body)
    y = _layer_norm(y_ref[...]) * lg_ref[...] + lb_ref[...]
    o_ref[...] = (y * jax.nn.sigmoid(y)).astype(BF16)


def _conv_mixer(proj, dw, db, ln_g, ln_b):
    cb = COL_CV // CONV_WIDTH
    vec = pl.BlockSpec((1, CONV_WIDTH), lambda i: (0, 0))
    return pl.pallas_call(
        _conv_kernel,
        grid=(N_ROW_BLK,),
        in_specs=[
            pl.BlockSpec((ROW_BLK, CONV_WIDTH), lambda i: (i, cb)),
            pl.BlockSpec((ROW_BLK, CONV_WIDTH), lambda i: (i, cb + 1)),
            pl.BlockSpec((CONV_K, 8, CONV_WIDTH), lambda i: (0, 0, 0)),
            vec, vec, vec,
        ],
        out_specs=pl.BlockSpec((ROW_BLK, CONV_WIDTH), lambda i: (i, 0)),
        out_shape=jax.ShapeDtypeStruct((T_ALL, CONV_WIDTH), BF16),
        scratch_shapes=[pltpu.VMEM((8, STENCIL_ROWS, CONV_WIDTH), F32),
                        pltpu.VMEM((ROW_BLK, CONV_WIDTH), F32)],
        compiler_params=_cp(("arbitrary",)),
        name="conv_mixer",
    )(proj, proj, jnp.broadcast_to(dw[:, None, :], (CONV_K, 8, CONV_WIDTH)), db.reshape(1, -1),
      ln_g.reshape(1, -1), ln_b.reshape(1, -1))


S5_ROWS = 512


def _s5_kernel(*refs, tc, nb, reverse, post):
    u_ref, bq_ref, cq_ref, a_ref, h0_ref = refs[:5]
    k = 5
    if post:
        yf_ref, dskip_ref, wglu_ref = refs[k:k + 3]
        k += 3
    y_ref, hf_ref = refs[k:k + 2]
    y_scr, hs_ref, st_ref = refs[k + 2:]
    j = pl.program_id(0)

    @pl.when(j == 0)
    def _():
        st_ref[...] = h0_ref[...]

    rows = tc * nb
    u_tm = pltpu.einshape("btc->tbc", u_ref[...]).reshape(rows, S5_WIDTH)
    u2 = u_tm.astype(BF16)
    for q in range(S5_Q):
        hs_ref[:, q * 2 * S5_QS:(q + 1) * 2 * S5_QS] = jnp.dot(
            u2[:, q * S5_QU:(q + 1) * S5_QU], bq_ref[q], preferred_element_type=F32)

    lane_w = 4096 // nb
    for q in range(S5_Q):
        for w in range(S5_QS // lane_w):
            re0 = q * 2 * S5_QS + w * lane_w
            im0 = re0 + S5_QS
            a_re = jnp.broadcast_to(a_ref[:, re0:re0 + lane_w], (nb, lane_w))
            a_im = jnp.broadcast_to(a_ref[:, im0:im0 + lane_w], (nb, lane_w))

            def step(t, carry, re0=re0, im0=im0, a_re=a_re, a_im=a_im):
                h_re, h_im = carry
                te = tc - 1 - t if reverse else t
                r0 = pl.multiple_of(te * nb, nb)
                n_re = a_re * h_re - a_im * h_im + hs_ref[pl.ds(r0, nb), re0:re0 + lane_w]
                n_im = a_re * h_im + a_im * h_re + hs_ref[pl.ds(r0, nb), im0:im0 + lane_w]
                hs_ref[pl.ds(r0, nb), re0:re0 + lane_w] = n_re
                hs_ref[pl.ds(r0, nb), im0:im0 + lane_w] = n_im
                return n_re, n_im

            h_re, h_im = lax.fori_loop(
                0, tc, step, (st_ref[:, re0:re0 + lane_w], st_ref[:, im0:im0 + lane_w]), unroll=4)
            st_ref[:, re0:re0 + lane_w] = h_re
            st_ref[:, im0:im0 + lane_w] = h_im

    for q in range(S5_Q):
        yq = jnp.dot(hs_ref[:, q * 2 * S5_QS:(q + 1) * 2 * S5_QS].astype(BF16), cq_ref[q],
                     preferred_element_type=F32)
        y_scr[:, :, q * S5_QU:(q + 1) * S5_QU] = yq.reshape(tc, nb, S5_QU)

    if post:
        y_scr[...] = y_scr[...] + (dskip_ref[...] * u_tm).reshape(tc, nb, S5_WIDTH)
    y_bt = pltpu.einshape("tbc->btc", y_scr[...])
    if post:
        y = jax.nn.gelu(y_bt + yf_ref[...]).reshape(rows, S5_WIDTH)
        gate = jnp.dot(y.astype(BF16), wglu_ref[...].astype(BF16), preferred_element_type=F32)
        y_ref[...] = (y * jax.nn.sigmoid(gate)).astype(BF16).reshape(nb, tc, S5_WIDTH)
    else:
        y_ref[...] = y_bt

    @pl.when(j == pl.num_programs(0) - 1)
    def _():
        hf_ref[...] = st_ref[...]


def _s5_pass(proj3, direction, bq, cq, a_vec, h0, yf=None, d_skip=None, w_glu=None):
    post = yf is not None
    reverse = direction == 1
    cb = COL_S5 // S5_WIDTH
    nb, n = BATCH, SEQ
    tc = S5_ROWS // nb
    nc = n // tc

    def chunk(j):
        return nc - 1 - j if reverse else j

    seq_blk = pl.BlockSpec((nb, tc, S5_WIDTH), lambda j: (0, chunk(j), 0))
    in_specs = [
        pl.BlockSpec((nb, tc, S5_WIDTH), lambda j: (0, chunk(j), cb)),
        pl.BlockSpec((None, S5_Q, S5_QU, 2 * S5_QS), lambda j: (direction, 0, 0, 0)),
        pl.BlockSpec((None, S5_Q, 2 * S5_QS, S5_QU), lambda j: (direction, 0, 0, 0)),
        pl.BlockSpec((None, 1, S5_LANES), lambda j: (direction, 0, 0)),
        pl.BlockSpec((None, nb, S5_LANES), lambda j: (direction, 0, 0)),
    ]
    args = [proj3, bq, cq, a_vec, h0]
    if post:
        in_specs += [seq_blk, pl.BlockSpec((1, S5_WIDTH), lambda j: (0, 0)),
                     pl.BlockSpec((S5_WIDTH, S5_WIDTH), lambda j: (0, 0))]
        args += [yf, d_skip.reshape(1, S5_WIDTH), w_glu]
    return pl.pallas_call(
        functools.partial(_s5_kernel, tc=tc, nb=nb, reverse=reverse, post=post),
        grid=(nc,),
        in_specs=in_specs,
        out_specs=[seq_blk, pl.BlockSpec((nb, S5_LANES), lambda j: (0, 0))],
        out_shape=[jax.ShapeDtypeStruct((nb, n, S5_WIDTH), BF16 if post else F32),
                   jax.ShapeDtypeStruct((nb, S5_LANES), F32)],
        scratch_shapes=[
            pltpu.VMEM((tc, nb, S5_WIDTH), F32),
            pltpu.VMEM((tc * nb, S5_LANES), F32),
            pltpu.VMEM((nb, S5_LANES), F32),
        ],
        compiler_params=_cp(("arbitrary",)),
        name="s5_pass",
    )(*args)


def _s5_mixer(proj3, bq, cq, a_vec, h0, d_skip, w_glu):
    yf, hf_f = _s5_pass(proj3, 0, bq, cq, a_vec, h0)
    y, hf_b = _s5_pass(proj3, 1, bq, cq, a_vec, h0, yf, d_skip, w_glu)
    return y.reshape(-1, S5_WIDTH), jnp.stack([hf_f, hf_b])


LAT_TC = S5_ROWS // (2 * DEC_BATCH)
LAT_NC = DEC_SEQ // LAT_TC


def _s5_lat_kernel(*refs):
    nbat, tc = DEC_BATCH, LAT_TC
    uf_refs = refs[:nbat]
    ub_refs = refs[nbat:2 * nbat]
    (bq_ref, cq_ref, a_ref, h0_ref, yf_ref, yb_ref, hf_ref,
     u_scr, y_scr, hs_ref, hb_ref, st_ref) = refs[2 * nbat:]
    j = pl.program_id(0)

    @pl.when(j == 0)
    def _():
        st_ref[...] = h0_ref[...]

    u_scr[:, 0:nbat, :] = pltpu.einshape("btc->tbc", jnp.stack([r[...] for r in uf_refs], axis=0))
    u_scr[:, nbat:, :] = pltpu.einshape("btc->tbc", jnp.stack([r[...] for r in ub_refs], axis=0))
    rows = 2 * nbat * tc
    u2 = u_scr[...].reshape(rows, S5_WIDTH).astype(BF16)
    is_fwd = (lax.broadcasted_iota(jnp.int32, (rows, 1), 0) & (2 * nbat - 1)) < nbat
    zero = jnp.zeros((rows, S5_QU), BF16)
    for q in range(S5_Q):
        uq = u2[:, q * S5_QU:(q + 1) * S5_QU]
        lhs = jnp.concatenate([jnp.where(is_fwd, uq, zero), jnp.where(is_fwd, zero, uq)], axis=1)
        bu = jnp.dot(lhs, bq_ref[q], preferred_element_type=F32)
        hs_ref[:, q * 2 * S5_QS:(q + 1) * 2 * S5_QS] = bu
        hb_ref[:, q * 2 * S5_QS:(q + 1) * 2 * S5_QS] = bu

    lane_w = S5_QS
    low = lax.broadcasted_iota(jnp.int32, (2 * nbat, lane_w), 0) < nbat
    for q in range(S5_Q):
        re0 = q * 2 * S5_QS
        im0 = re0 + S5_QS
        a_re = a_ref[:, re0:re0 + lane_w]
        a_im = a_ref[:, im0:im0 + lane_w]

        def step(t, carry, re0=re0, im0=im0, a_re=a_re, a_im=a_im):
            h_re, h_im = carry
            ra = pl.multiple_of(t * 2 * nbat, 2 * nbat)
            rb = pl.multiple_of((tc - 1 - t) * 2 * nbat, 2 * nbat)
            n_re = (a_re * h_re - a_im * h_im
                    + jnp.where(low, hs_ref[pl.ds(ra, 2 * nbat), re0:re0 + lane_w],
                                hb_ref[pl.ds(rb, 2 * nbat), re0:re0 + lane_w]))
            n_im = (a_re * h_im + a_im * h_re
                    + jnp.where(low, hs_ref[pl.ds(ra, 2 * nbat), im0:im0 + lane_w],
                                hb_ref[pl.ds(rb, 2 * nbat), im0:im0 + lane_w]))
            hs_ref[pl.ds(ra, 2 * nbat), re0:re0 + lane_w] = n_re
            hs_ref[pl.ds(ra, 2 * nbat), im0:im0 + lane_w] = n_im
            hb_ref[pl.ds(rb, 2 * nbat), re0:re0 + lane_w] = n_re
            hb_ref[pl.ds(rb, 2 * nbat), im0:im0 + lane_w] = n_im
            return n_re, n_im

        h_re, h_im = lax.fori_loop(
            0, tc, step, (st_ref[:, re0:re0 + lane_w], st_ref[:, im0:im0 + lane_w]), unroll=2)
        st_ref[:, re0:re0 + lane_w] = h_re
        st_ref[:, im0:im0 + lane_w] = h_im

    for q in range(S5_Q):
        cols = slice(q * 2 * S5_QS, (q + 1) * 2 * S5_QS)
        hq = jnp.where(is_fwd, hs_ref[:, cols], hb_ref[:, cols]).astype(BF16)
        y2 = jnp.dot(hq, cq_ref[q], preferred_element_type=F32)
        yq = jnp.where(is_fwd, y2[:, :S5_QU], y2[:, S5_QU:])
        y_scr[:, :, q * S5_QU:(q + 1) * S5_QU] = yq.reshape(tc, 2 * nbat, S5_QU)
    yf_ref[...] = pltpu.einshape("tbc->btc", y_scr[:, 0:nbat, :])
    yb_ref[...] = pltpu.einshape("tbc->btc", y_scr[:, nbat:, :])

    @pl.when(j == pl.num_programs(0) - 1)
    def _():
        hf_ref[...] = st_ref[...]


def _s5_lat_post_kernel(yf_ref, yb_ref, u_ref, d_ref, w_ref, o_ref):
    y = jax.nn.gelu(yf_ref[...] + yb_ref[...] + d_ref[...] * u_ref[...])
    gate = jnp.dot(y.astype(BF16), w_ref[...].astype(BF16), preferred_element_type=F32)
    o_ref[...] = (y * jax.nn.sigmoid(gate)).astype(BF16)


def _s5_lat_mixer(proj, bq, cq, a_vec, h0, d_skip, w_glu):
    nbat, tc, nc = DEC_BATCH, LAT_TC, LAT_NC
    cb = COL_S5 // S5_WIDTH
    proj3 = proj.reshape(T_ALL // SEQ, SEQ, MIX_COLS)
    per = SEQ // tc

    def u_spec(b, mirrored):
        def index(j):
            c = nc - 1 - j if mirrored else j
            return (T_CTX // SEQ + b * (DEC_SEQ // SEQ) + c // per, c % per, cb)
        return pl.BlockSpec((None, tc, S5_WIDTH), index)

    bq2 = jnp.concatenate([bq[0], bq[1]], axis=1)
    cq2 = jnp.concatenate([cq[0], cq[1]], axis=2)
    a8 = jnp.broadcast_to(a_vec, (2, nbat, S5_LANES)).reshape(2 * nbat, S5_LANES)
    h08 = h0.reshape(2 * nbat, S5_LANES)
    yf, yb, _ = pl.pallas_call(
        _s5_lat_kernel,
        grid=(nc,),
        in_specs=[u_spec(b, False) for b in range(nbat)] + [u_spec(b, True) for b in range(nbat)] + [
            pl.BlockSpec((S5_Q, 2 * S5_QU, 2 * S5_QS), lambda j: (0, 0, 0)),
            pl.BlockSpec((S5_Q, 2 * S5_QS, 2 * S5_QU), lambda j: (0, 0, 0)),
            pl.BlockSpec((2 * nbat, S5_LANES), lambda j: (0, 0)),
            pl.BlockSpec((2 * nbat, S5_LANES), lambda j: (0, 0)),
        ],
        out_specs=[pl.BlockSpec((nbat, tc, S5_WIDTH), lambda j: (0, j, 0)),
                   pl.BlockSpec((nbat, tc, S5_WIDTH), lambda j: (0, nc - 1 - j, 0)),
                   pl.BlockSpec((2 * nbat, S5_LANES), lambda j: (0, 0))],
        out_shape=[jax.ShapeDtypeStruct((nbat, DEC_SEQ, S5_WIDTH), F32),
                   jax.ShapeDtypeStruct((nbat, DEC_SEQ, S5_WIDTH), F32),
                   jax.ShapeDtypeStruct((2 * nbat, S5_LANES), F32)],
        scratch_shapes=[
            pltpu.VMEM((tc, 2 * nbat, S5_WIDTH), F32),
            pltpu.VMEM((tc, 2 * nbat, S5_WIDTH), F32),
            pltpu.VMEM((tc * 2 * nbat, S5_LANES), F32),
            pltpu.VMEM((tc * 2 * nbat, S5_LANES), F32),
            pltpu.VMEM((2 * nbat, S5_LANES), F32),
        ],
        compiler_params=_cp(("arbitrary",)),
        name="s5_lat",
    )(*([proj3] * (2 * nbat)), bq2, cq2, a8, h08)
    tr = ROW_BLK
    row = pl.BlockSpec((tr, S5_WIDTH), lambda i: (i, 0))
    return pl.pallas_call(
        _s5_lat_post_kernel,
        grid=(T_LAT // tr,),
        in_specs=[row, row, pl.BlockSpec((tr, S5_WIDTH), lambda i: (T_CTX // tr + i, cb)),
                  pl.BlockSpec((1, S5_WIDTH), lambda i: (0, 0)),
                  pl.BlockSpec((S5_WIDTH, S5_WIDTH), lambda i: (0, 0))],
        out_specs=row,
        out_shape=jax.ShapeDtypeStruct((T_LAT, S5_WIDTH), BF16),
        compiler_params=_cp(("arbitrary",)),
        name="s5_lat_post",
    )(yf.reshape(T_LAT, S5_WIDTH), yb.reshape(T_LAT, S5_WIDTH), proj, d_skip.reshape(1, S5_WIDTH), w_glu)


def _s5_params(lam_re, lam_im, log_dt, b_re, b_im, c_re, c_im):
    dt = jnp.exp(log_dt)[..., None]
    x = lam_re * dt
    y = lam_im * dt
    ex = jnp.exp(x)
    abar_re = ex * jnp.cos(y)
    abar_im = ex * jnp.sin(y)
    num_re = jnp.expm1(x) * jnp.cos(y) - 2.0 * jnp.square(jnp.sin(0.5 * y))
    num_im = abar_im
    den = lam_re * lam_re + lam_im * lam_im
    coef_re = (num_re * lam_re + num_im * lam_im) / den
    coef_im = (num_im * lam_re - num_re * lam_im) / den
    bbar_re = coef_re[..., None] * b_re - coef_im[..., None] * b_im
    bbar_im = coef_re[..., None] * b_im + coef_im[..., None] * b_re
    eye = jnp.eye(S5_GQ, dtype=F32)
    bb = jnp.stack([bbar_re, bbar_im]).reshape(2, 2, S5_Q, S5_GQ, S5_STATE, S5_CH)
    bq = jnp.einsum("rdqgpc,gh->dqgcrhp", bb, eye).reshape(2, S5_Q, S5_QU, 2 * S5_QS)
    cc = jnp.stack([c_re, -c_im]).reshape(2, 2, S5_Q, S5_GQ, S5_CH, S5_STATE)
    cq = jnp.einsum("rdqgcp,gh->dqrhpgc", cc, eye).reshape(2, S5_Q, 2 * S5_QS, S5_QU)
    ab = jnp.stack([abar_re, abar_im]).reshape(2, 2, S5_Q, S5_GQ, S5_STATE)
    a_vec = ab.transpose(1, 2, 0, 3, 4).reshape(2, 1, S5_LANES)
    return bq.astype(BF16), cq.astype(BF16), a_vec


def _s5_state_to_lanes(st):
    b = st.shape[0]
    s = st.reshape(b, 2, 2, S5_Q, S5_GQ, S5_STATE).transpose(1, 0, 3, 2, 4, 5)
    return s.reshape(2, b, S5_LANES)


def _s5_lanes_to_state(hl):
    b = hl.shape[1]
    s = hl.reshape(2, b, S5_Q, 2, S5_GQ, S5_STATE).transpose(1, 0, 3, 2, 4, 5)
    return s.reshape(b, 2, 2, S5_GROUPS, S5_STATE)


def _rope(x, cos, sin_signed):
    lane = lax.broadcasted_iota(jnp.int32, x.shape, 1)
    quarter = HEAD_DIM // 4
    partner = jnp.where((lane & quarter) == 0,
                        pltpu.roll(x, HEAD_DIM - quarter, axis=1),
                        pltpu.roll(x, quarter, axis=1))
    return x * cos + partner * sin_signed


def _nt_dot(a, b):
    return lax.dot_general(a, b, (((1,), (1,)), ((), ())), preferred_element_type=F32)


def _stack_heads(q_ref, rope=None):
    parts = []
    for r in range(REP):
        q = q_ref[:, r * HEAD_DIM:(r + 1) * HEAD_DIM]
        if rope is not None:
            q = _rope(q, *rope)
        parts.append(q.astype(BF16))
    return jnp.concatenate(parts, axis=0)


def _with_ones(v):
    return jnp.concatenate([v.astype(BF16), jnp.ones(v.shape, BF16)], axis=1)


def _sink_column(sink_ref, h0, rows):
    return jnp.concatenate(
        [jnp.broadcast_to(sink_ref[pl.ds(h0 + r, 1), 0:1], (rows, 1)) for r in range(REP)], axis=0)


def _ctx_attn_kernel(q_ref, k_ref, v_ref, sink_ref, o_ref, ko_ref, vo_ref):
    h0 = pl.program_id(1) * REP
    k = k_ref[...]
    v = v_ref[...]
    ko_ref[...] = k
    vo_ref[...] = v
    q = _stack_heads(q_ref)
    s = _nt_dot(q, k.astype(BF16)) * (HEAD_DIM ** -0.5)
    sink = _sink_column(sink_ref, h0, SEQ)
    m = jnp.maximum(jnp.max(s, axis=-1, keepdims=True), sink)
    p = jnp.exp(s - m)
    acc = jnp.dot(p.astype(BF16), _with_ones(v), preferred_element_type=F32)
    o = acc[:, :HEAD_DIM] / (acc[:, HEAD_DIM:HEAD_DIM + 1] + jnp.exp(sink - m))
    for r in range(REP):
        o_ref[:, r * HEAD_DIM:(r + 1) * HEAD_DIM] = o[r * SEQ:(r + 1) * SEQ].astype(BF16)


def _ctx_attention(proj, sink_tile):
    qw = REP * HEAD_DIM
    kv_out = pl.BlockSpec((SEQ, HEAD_DIM), lambda b, h: (b, h))
    return pl.pallas_call(
        _ctx_attn_kernel,
        grid=(BATCH, N_KV_HEADS),
        in_specs=[
            pl.BlockSpec((SEQ, qw), lambda b, h: (b, COL_Q // qw + h)),
            pl.BlockSpec((SEQ, HEAD_DIM), lambda b, h: (b, COL_K // HEAD_DIM + h)),
            pl.BlockSpec((SEQ, HEAD_DIM), lambda b, h: (b, COL_V // HEAD_DIM + h)),
            pl.BlockSpec((N_HEADS, HEAD_DIM), lambda b, h: (0, 0)),
        ],
        out_specs=[pl.BlockSpec((SEQ, qw), lambda b, h: (b, h)), kv_out, kv_out],
        out_shape=[jax.ShapeDtypeStruct((T_CTX, ATTN_WIDTH), BF16),
                   jax.ShapeDtypeStruct((T_CTX, KV_WIDTH), F32),
                   jax.ShapeDtypeStruct((T_CTX, KV_WIDTH), F32)],
        compiler_params=_cp(("arbitrary", "arbitrary")),
        name="ctx_attention",
    )(proj, proj, proj, sink_tile)


LAT_TQ = 256
LAT_KWIN = LAT_TQ + 2 * WINDOW


def _lat_attn_kernel(q_ref, k_ref, v_ref, kc_ref, vc_ref, cq_ref, sq_ref, ck_ref, sk_ref,
                     sink_ref, o_ref):
    h0 = pl.program_id(1) * REP
    q0 = pl.program_id(2) * LAT_TQ
    k0 = pl.multiple_of(jnp.clip(q0 - WINDOW, 0, DEC_SEQ - LAT_KWIN), WINDOW)
    win = pl.ds(k0, LAT_KWIN)
    kb = _rope(k_ref[win, :], ck_ref[win, :], sk_ref[win, :]).astype(BF16)
    rows = REP * LAT_TQ
    row = lax.broadcasted_iota(jnp.int32, (rows, LAT_KWIN), 0)
    qpos = q0 + (row & (LAT_TQ - 1))
    kpos = k0 + lax.broadcasted_iota(jnp.int32, (rows, LAT_KWIN), 1)
    mask = jnp.abs(qpos - kpos) <= WINDOW
    scale = HEAD_DIM ** -0.5
    q = _stack_heads(q_ref, (cq_ref[...], sq_ref[...]))
    s_lat = jnp.where(mask, _nt_dot(q, kb) * scale, NEG_INF)
    s_ctx = _nt_dot(q, kc_ref[...].astype(BF16)) * scale
    sink = _sink_column(sink_ref, h0, LAT_TQ)
    m = jnp.maximum(jnp.maximum(jnp.max(s_lat, axis=-1, keepdims=True),
                                jnp.max(s_ctx, axis=-1, keepdims=True)), sink)
    p_lat = jnp.exp(s_lat - m).astype(BF16)
    p_ctx = jnp.exp(s_ctx - m).astype(BF16)
    acc = (jnp.dot(p_lat, _with_ones(v_ref[win, :]), preferred_element_type=F32)
           + jnp.dot(p_ctx, _with_ones(vc_ref[...]), preferred_element_type=F32))
    o = acc[:, :HEAD_DIM] / (acc[:, HEAD_DIM:HEAD_DIM + 1] + jnp.exp(sink - m))
    for r in range(REP):
        o_ref[:, r * HEAD_DIM:(r + 1) * HEAD_DIM] = o[r * LAT_TQ:(r + 1) * LAT_TQ].astype(BF16)


def _lat_attention(proj, cache_k_l, cache_v_l, rope_cos, rope_sin, sink_tile):
    tq = LAT_TQ
    nq = DEC_SEQ // tq
    qw = REP * HEAD_DIM
    row0_q = T_CTX // tq
    row0_k = T_CTX // DEC_SEQ
    tab_q = pl.BlockSpec((tq, HEAD_DIM), lambda b, h, i: (i, 0))
    tab_k = pl.BlockSpec((DEC_SEQ, HEAD_DIM), lambda b, h, i: (0, 0))
    return pl.pallas_call(
        _lat_attn_kernel,
        grid=(DEC_BATCH, N_KV_HEADS, nq),
        in_specs=[
            pl.BlockSpec((tq, qw), lambda b, h, i: (row0_q + b * nq + i, COL_Q // qw + h)),
            pl.BlockSpec((DEC_SEQ, HEAD_DIM), lambda b, h, i: (row0_k + b, COL_K // HEAD_DIM + h)),
            pl.BlockSpec((DEC_SEQ, HEAD_DIM), lambda b, h, i: (row0_k + b, COL_V // HEAD_DIM + h)),
            pl.BlockSpec((None, PAST_LEN, HEAD_DIM), lambda b, h, i: (b, 0, h)),
            pl.BlockSpec((None, PAST_LEN, HEAD_DIM), lambda b, h, i: (b, 0, h)),
            tab_q, tab_q, tab_k, tab_k,
            pl.BlockSpec((N_HEADS, HEAD_DIM), lambda b, h, i: (0, 0)),
        ],
        out_specs=pl.BlockSpec((tq, qw), lambda b, h, i: (b * nq + i, h)),
        out_shape=jax.ShapeDtypeStruct((T_LAT, ATTN_WIDTH), BF16),
        compiler_params=_cp(("arbitrary", "arbitrary", "arbitrary")),
        name="lat_attention",
    )(proj, proj, proj, cache_k_l, cache_v_l, rope_cos, rope_sin, rope_cos, rope_sin, sink_tile)


def _rope_tables():
    t = jnp.arange(DEC_SEQ)
    row, col = t // GRID_W, t % GRID_W
    quarter = HEAD_DIM // 4
    inv = ROPE_BASE ** (-jnp.arange(quarter, dtype=F32) / quarter)
    lane = jnp.arange(HEAD_DIM)
    pos = jnp.where(lane[None, :] < HEAD_DIM // 2, row[:, None], col[:, None]).astype(F32)
    ang = pos * inv[lane % quarter][None, :]
    sign = jnp.where((lane & quarter) == 0, -1.0, 1.0).astype(F32)
    return jnp.cos(ang), jnp.sin(ang) * sign[None, :]


def _merge_kernel(yp_ref, ysc_ref, ysl_ref, yac_ref, yal_ref, yc_ref, g0_ref, g1_ref, g2_ref,
                  g3_ref, wp_ref, ws_ref, wa_ref, wc_ref, o_ref):
    is_ctx = pl.program_id(1) < N_CTX_BLK
    y_s5 = jnp.where(is_ctx, ysc_ref[...], ysl_ref[...])
    attn = jnp.where(is_ctx, yac_ref[...], yal_ref[...])
    acc = None
    for y, g_ref, w_ref in ((yp_ref[...], g0_ref, wp_ref), (y_s5, g1_ref, ws_ref),
                            (attn, g2_ref, wa_ref), (yc_ref[...], g3_ref, wc_ref)):
        t = g_ref[...].astype(F32) * jnp.dot(y, w_ref[...].astype(BF16), preferred_element_type=F32)
        acc = t if acc is None else acc + t
    o_ref[...] = acc.astype(BF16)


def _merge(gates, y_pool, y_s5_ctx, y_s5_lat, attn_ctx, attn_lat, y_conv, layer,
           w_pool, w_s5, w_attn, w_conv):
    tm, tn = ROW_BLK, 512
    per = D_MODEL // tn

    def act(width):
        return pl.BlockSpec((tm, width), lambda j, i: (i, 0))

    def ctx_part(width):
        return pl.BlockSpec((tm, width), lambda j, i: (jnp.minimum(i, N_CTX_BLK - 1), 0))

    def lat_part(width):
        return pl.BlockSpec((tm, width), lambda j, i: (jnp.maximum(i - N_CTX_BLK, 0), 0))

    def gate(branch):
        return pl.BlockSpec((tm, tn), lambda j, i: (i, branch * per + j))

    def wt(width):
        return pl.BlockSpec((None, width, tn), lambda j, i: (layer, 0, j))

    return pl.pallas_call(
        _merge_kernel,
        grid=(D_MODEL // tn, T_ALL // tm),
        in_specs=[act(POOL_WIDTH), ctx_part(S5_WIDTH), lat_part(S5_WIDTH),
                  ctx_part(ATTN_WIDTH), lat_part(ATTN_WIDTH), act(CONV_WIDTH),
                  gate(0), gate(1), gate(2), gate(3),
                  wt(POOL_WIDTH), wt(S5_WIDTH), wt(ATTN_WIDTH), wt(CONV_WIDTH)],
        out_specs=pl.BlockSpec((tm, tn), lambda j, i: (i, j)),
        out_shape=jax.ShapeDtypeStruct((T_ALL, D_MODEL), BF16),
        compiler_params=_cp(("arbitrary", "arbitrary")),
        name="merge",
    )(y_pool, y_s5_ctx, y_s5_lat, attn_ctx, attn_lat, y_conv, gates, gates, gates, gates,
      w_pool, w_s5, w_attn, w_conv)


FFN_TILE = 256
N_FFN_TILE = D_FF // FFN_TILE


def _ffn_kernel(h_ref, wg_ref, wv_ref, dwg_ref, dwv_ref, dbg_ref, dbv_ref, wd_ref, o_ref,
                ug_ref, uv_ref, acc_ref):
    i = pl.program_id(0)
    f = pl.program_id(1)

    def up(slot):
        h = h_ref[...]
        ug_ref[slot] = jnp.dot(h, wg_ref[...].astype(BF16), preferred_element_type=F32)
        uv_ref[slot] = jnp.dot(h, wv_ref[...].astype(BF16), preferred_element_type=F32)

    def down(slot):
        pos, seq_len = _seq_pos(i, ROW_BLK)
        first = pos == 0
        last = pos == seq_len - 1

        def conv3(u, dw_ref, db_ref):
            prev = jnp.where(first, 0.0, _shift_rows(u, -1))
            nxt = jnp.where(last, 0.0, _shift_rows(u, 1))
            return prev * dw_ref[0:1, :] + u * dw_ref[1:2, :] + nxt * dw_ref[2:3, :] + db_ref[...]

        gt = conv3(ug_ref[slot], dwg_ref, dbg_ref)
        val = conv3(uv_ref[slot], dwv_ref, dbv_ref)
        act = (gt * jax.nn.sigmoid(gt) * val).astype(BF16)
        acc_ref[...] += jnp.dot(act, wd_ref[...].astype(BF16), preferred_element_type=F32)

    @pl.when(f == 0)
    def _():
        acc_ref[...] = jnp.zeros_like(acc_ref)
        up(0)

    for par in (0, 1):
        @pl.when(jnp.logical_and(jnp.logical_and(f > 0, f < N_FFN_TILE), f % 2 == par))
        def _(par=par):
            up(par)
            down(1 - par)

    @pl.when(f == N_FFN_TILE)
    def _():
        down((N_FFN_TILE - 1) % 2)
        o_ref[...] = acc_ref[...].astype(o_ref.dtype)


def _conv_ffn(h, layer, w_up, dw, db, w_down):
    tf, nf = FFN_TILE, N_FFN_TILE
    db3 = db.reshape(DEPTH, 1, 2 * D_FF)

    def cur(f):
        return jnp.minimum(f, nf - 1)

    def prv(f):
        return jnp.maximum(f - 1, 0)

    return pl.pallas_call(
        _ffn_kernel,
        grid=(N_ROW_BLK, nf + 1),
        in_specs=[
            pl.BlockSpec((ROW_BLK, D_MODEL), lambda i, f: (i, 0)),
            pl.BlockSpec((None, D_MODEL, tf), lambda i, f: (layer, 0, cur(f))),
            pl.BlockSpec((None, D_MODEL, tf), lambda i, f: (layer, 0, nf + cur(f))),
            pl.BlockSpec((None, FFN_CONV_K, tf), lambda i, f: (layer, 0, prv(f))),
            pl.BlockSpec((None, FFN_CONV_K, tf), lambda i, f: (layer, 0, nf + prv(f))),
            pl.BlockSpec((None, 1, tf), lambda i, f: (layer, 0, prv(f))),
            pl.BlockSpec((None, 1, tf), lambda i, f: (layer, 0, nf + prv(f))),
            pl.BlockSpec((None, tf, D_MODEL), lambda i, f: (layer, prv(f), 0)),
        ],
        out_specs=pl.BlockSpec((ROW_BLK, D_MODEL), lambda i, f: (i, 0)),
        out_shape=jax.ShapeDtypeStruct((T_ALL, D_MODEL), BF16),
        scratch_shapes=[pltpu.VMEM((2, ROW_BLK, tf), F32), pltpu.VMEM((2, ROW_BLK, tf), F32),
                        pltpu.VMEM((ROW_BLK, D_MODEL), F32)],
        compiler_params=_cp(("arbitrary", "arbitrary")),
        name="conv_ffn",
    )(h, w_up, w_up, dw, dw, db3, db3, w_down)


def kernel(x_prompt, x_sample, cache_k, cache_v, state_s5, c, c_ctx, w_ada, b_ada, w_in, pool_w, pool_scale, s5_lambda_re, s5_lambda_im, s5_log_dt, s5_b_re, s5_b_im, s5_c_re, s5_c_im, s5_d, s5_w_glu, attn_sink, conv_dw, conv_db, conv_ln_g, conv_ln_b, w_br_pool, w_br_s5, w_br_attn, w_br_conv, w_out, ln1_g, ln1_b, ffn_w_up, ffn_dw, ffn_db, ffn_w_down, ln2_g, ln2_b):
    x_parts = (x_prompt.reshape(T_CTX, D_MODEL), x_sample.reshape(T_LAT, D_MODEL))
    cvec = jnp.concatenate(
        [c_ctx[None, :], c, jnp.zeros((MOD_ROWS - 1 - DEC_BATCH, D_MODEL), F32)], axis=0)
    mod = _mod_table(cvec, w_ada, b_ada).reshape(DEPTH, MOD_ROWS, 6, D_MODEL)
    rope_cos, rope_sin = _rope_tables()

    ks_out, vs_out, ss_out = [], [], []
    h = _ln_mod(x_parts[0], x_parts[1], mod[0], 0, 1)
    for l in range(DEPTH):
        proj = _matmul(h, w_in, l, 0, MIX_COLS)
        gates = _matmul(h, w_in, l, COL_GATE, N_BRANCH * D_MODEL, out_dtype=BF16, gate=True)

        y_pool = _pool_mixer(proj, pool_w[l], pool_scale[l])
        y_conv = _conv_mixer(proj, conv_dw[l], conv_db[l], conv_ln_g[l], conv_ln_b[l])

        bq, cq, a_vec = _s5_params(s5_lambda_re[l], s5_lambda_im[l], s5_log_dt[l], s5_b_re[l],
                                   s5_b_im[l], s5_c_re[l], s5_c_im[l])
        proj3 = proj.reshape(T_ALL // SEQ, SEQ, MIX_COLS)
        h0_ctx = jnp.zeros((2, BATCH, S5_LANES), F32)
        y_s5_ctx, hf_ctx = _s5_mixer(proj3, bq, cq, a_vec, h0_ctx, s5_d[l], s5_w_glu[l])
        y_s5_lat = _s5_lat_mixer(proj, bq, cq, a_vec, _s5_state_to_lanes(state_s5[:, l]),
                                 s5_d[l], s5_w_glu[l])

        sink_tile = jnp.broadcast_to(attn_sink[l][:, None], (N_HEADS, HEAD_DIM))
        attn_ctx, k_new, v_new = _ctx_attention(proj, sink_tile)
        attn_lat = _lat_attention(proj, cache_k[:, l].reshape(DEC_BATCH, PAST_LEN, KV_WIDTH),
                                  cache_v[:, l].reshape(DEC_BATCH, PAST_LEN, KV_WIDTH),
                                  rope_cos, rope_sin, sink_tile)

        merged = _merge(gates, y_pool, y_s5_ctx, y_s5_lat, attn_ctx, attn_lat, y_conv, l,
                        w_br_pool, w_br_s5, w_br_attn, w_br_conv)
        mixed = _matmul(merged, w_out, l, out_dtype=BF16)
        x, h2 = _res_ln(x_parts, mixed, mod[l], 2, ln1_g[l], ln1_b[l], mod[l], 3, 4)
        f = _conv_ffn(h2, l, ffn_w_up, ffn_dw, ffn_db, ffn_w_down)
        if l + 1 < DEPTH:
            x, h = _res_ln((x,), f, mod[l], 5, ln2_g[l], ln2_b[l], mod[l + 1], 0, 1)
            x_parts = (x,)
        else:
            y_prompt, y_sample = _res_ln((x,), f, mod[l], 5, ln2_g[l], ln2_b[l])

        ks_out.append(k_new.reshape(BATCH, SEQ, N_KV_HEADS, HEAD_DIM))
        vs_out.append(v_new.reshape(BATCH, SEQ, N_KV_HEADS, HEAD_DIM))
        ss_out.append(_s5_lanes_to_state(hf_ctx))

    return (y_prompt.reshape(BATCH, SEQ, D_MODEL), y_sample.reshape(DEC_BATCH, DEC_SEQ, D_MODEL),
            jnp.stack(ks_out, axis=1), jnp.stack(vs_out, axis=1), jnp.stack(ss_out, axis=1))
```

```python
import functools

import jax
import jax.numpy as jnp
from jax import lax
from jax.experimental import pallas as pl
from jax.experimental.pallas import tpu as pltpu

F32 = jnp.float32
BF16 = jnp.bfloat16

D_MODEL = 2048
BATCH = 16
SEQ = 256
DEPTH = 2
DEC_BATCH = 4
DEC_SEQ = 1024
PAST_LEN = 512
GRID_W = 64
POOL_WIDTH = 512
POOL_GROUPS = 4
POOL_CH = POOL_WIDTH // POOL_GROUPS
POOL_WINDOWS = (2, 4, 8, 16)
S5_WIDTH = 512
S5_CH = 16
S5_GROUPS = S5_WIDTH // S5_CH
S5_STATE = 64
N_HEADS = 8
N_KV_HEADS = 2
HEAD_DIM = 128
REP = N_HEADS // N_KV_HEADS
ATTN_WIDTH = N_HEADS * HEAD_DIM
KV_WIDTH = N_KV_HEADS * HEAD_DIM
WINDOW = 128
ROPE_BASE = 10000.0
NEG_INF = -1e30
CONV_WIDTH = 512
CONV_K = 31
N_BRANCH = 4
D_FF = 5632
FFN_CONV_K = 3
ALPHA = (2.0 * DEPTH) ** 0.25
LN_EPS = 1e-5
MIX_COLS = POOL_WIDTH + S5_WIDTH + ATTN_WIDTH + 2 * KV_WIDTH + 2 * CONV_WIDTH
N_IN = MIX_COLS + N_BRANCH * D_MODEL

COL_POOL = 0
COL_S5 = POOL_WIDTH
COL_Q = COL_S5 + S5_WIDTH
COL_K = COL_Q + ATTN_WIDTH
COL_V = COL_K + KV_WIDTH
COL_CV = COL_V + KV_WIDTH

T_CTX = BATCH * SEQ
T_LAT = DEC_BATCH * DEC_SEQ
T_ALL = T_CTX + T_LAT
ROW_BLK = 1024
N_ROW_BLK = T_ALL // ROW_BLK
N_CTX_BLK = T_CTX // ROW_BLK
MOD_ROWS = 8
LN_ROWS = 512

S5_Q = 4
S5_GQ = S5_GROUPS // S5_Q
S5_QU = S5_GQ * S5_CH
S5_QS = S5_GQ * S5_STATE
S5_LANES = 2 * S5_GROUPS * S5_STATE

VMEM_LIMIT = 56 * 1024 * 1024


def _cp(sem):
    return pltpu.CompilerParams(dimension_semantics=sem, vmem_limit_bytes=VMEM_LIMIT)


def _mod_row(row_block, rows_per_block):
    lat_batch = (row_block * rows_per_block) // DEC_SEQ - T_CTX // DEC_SEQ
    return jnp.maximum(lat_batch + 1, 0)


def _seq_pos(row_block, rows):
    r = lax.broadcasted_iota(jnp.int32, (rows, 1), 0)
    seq_len = jnp.where(row_block < N_CTX_BLK, SEQ, DEC_SEQ)
    return r & (seq_len - 1), seq_len


def _layer_norm(x):
    mu = jnp.mean(x, axis=-1, keepdims=True)
    xc = x - mu
    var = jnp.mean(xc * xc, axis=-1, keepdims=True)
    return xc * lax.rsqrt(var + LN_EPS)


def _shift_rows(x, d):
    n = x.shape[0]
    return pltpu.roll(x, (-d) % n, axis=0)


def _split_bf16(x):
    hi = x.astype(BF16)
    return hi, (x - hi.astype(F32)).astype(BF16)


MOD_TN = 2048
MOD_KC = 256


def _mod_kernel(c_ref, w_ref, b_ref, o_ref):
    c = c_ref[...]
    a_hi, a_lo = _split_bf16(c * jax.nn.sigmoid(c))
    acc = jnp.broadcast_to(b_ref[0], (MOD_ROWS, MOD_TN))
    for k0 in range(0, D_MODEL, MOD_KC):
        w_hi, w_lo = _split_bf16(w_ref[0, k0:k0 + MOD_KC, :])
        ah = a_hi[:, k0:k0 + MOD_KC]
        acc = acc + jnp.dot(ah, w_hi, preferred_element_type=F32)
        acc = acc + jnp.dot(a_lo[:, k0:k0 + MOD_KC], w_hi, preferred_element_type=F32)
        acc = acc + jnp.dot(ah, w_lo, preferred_element_type=F32)
    o_ref[0] = acc


def _mod_table(cvec, w_ada, b_ada):
    tn = MOD_TN
    n6 = 6 * D_MODEL
    return pl.pallas_call(
        _mod_kernel,
        grid=(DEPTH, n6 // tn),
        in_specs=[
            pl.BlockSpec((MOD_ROWS, D_MODEL), lambda l, j: (0, 0)),
            pl.BlockSpec((1, D_MODEL, tn), lambda l, j: (l, 0, j)),
            pl.BlockSpec((1, 1, tn), lambda l, j: (l, 0, j)),
        ],
        out_specs=pl.BlockSpec((1, MOD_ROWS, tn), lambda l, j: (l, 0, j)),
        out_shape=jax.ShapeDtypeStruct((DEPTH, MOD_ROWS, n6), F32),
        compiler_params=_cp(("arbitrary", "arbitrary")),
        name="mod_table",
    )(cvec, w_ada, b_ada.reshape(DEPTH, 1, n6))


N_CTX_LN = T_CTX // LN_ROWS


def _row_spec():
    return pl.BlockSpec((LN_ROWS, D_MODEL), lambda i: (i, 0))


def _ctx_part_spec():
    return pl.BlockSpec((LN_ROWS, D_MODEL), lambda i: (jnp.minimum(i, N_CTX_LN - 1), 0))


def _lat_part_spec():
    return pl.BlockSpec((LN_ROWS, D_MODEL), lambda i: (jnp.maximum(i - N_CTX_LN, 0), 0))


def _mod_spec():
    return pl.BlockSpec((1, 6, D_MODEL), lambda i: (_mod_row(i, LN_ROWS), 0, 0))


def _read_rows(refs):
    if len(refs) == 1:
        return refs[0][...]
    return jnp.where(pl.program_id(0) < N_CTX_LN, refs[0][...], refs[1][...])


def _ln_mod_kernel(xc_ref, xl_ref, mod_ref, h_ref, *, shift_i, scale_i):
    y = _layer_norm(_read_rows((xc_ref, xl_ref)))
    scale = mod_ref[0, scale_i:scale_i + 1, :]
    shift = mod_ref[0, shift_i:shift_i + 1, :]
    h_ref[...] = (y * (1.0 + scale) + shift).astype(BF16)


def _ln_mod(x_ctx, x_lat, mod_l, shift_i, scale_i):
    return pl.pallas_call(
        functools.partial(_ln_mod_kernel, shift_i=shift_i, scale_i=scale_i),
        grid=(T_ALL // LN_ROWS,),
        in_specs=[_ctx_part_spec(), _lat_part_spec(), _mod_spec()],
        out_specs=_row_spec(),
        out_shape=jax.ShapeDtypeStruct((T_ALL, D_MODEL), BF16),
        compiler_params=_cp(("arbitrary",)),
        name="ln_mod",
    )(x_ctx, x_lat, mod_l)


def _res_ln_kernel(*refs, n_x, gate_i, next_shift_i, next_scale_i, split_out):
    x_refs = refs[:n_x]
    y_ref, mod_ref, g_ref, b_ref = refs[n_x:n_x + 4]
    rest = refs[n_x + 4:]
    gate = mod_ref[0, gate_i:gate_i + 1, :]
    z = ALPHA * _read_rows(x_refs) + gate * y_ref[...].astype(F32)
    xn = _layer_norm(z) * g_ref[...] + b_ref[...]
    if split_out:
        xc_ref, xl_ref = rest

        @pl.when(pl.program_id(0) < N_CTX_LN)
        def _():
            xc_ref[...] = xn

        @pl.when(pl.program_id(0) >= N_CTX_LN)
        def _():
            xl_ref[...] = xn
    else:
        nmod_ref, xo_ref, h_ref = rest
        xo_ref[...] = xn
        scale = nmod_ref[0, next_scale_i:next_scale_i + 1, :]
        shift = nmod_ref[0, next_shift_i:next_shift_i + 1, :]
        h_ref[...] = (_layer_norm(xn) * (1.0 + scale) + shift).astype(BF16)


def _res_ln(x_parts, y, mod_l, gate_i, g, b, next_mod=None, next_shift_i=None, next_scale_i=None):
    vec = pl.BlockSpec((1, D_MODEL), lambda i: (0, 0))
    x_specs = [_row_spec()] if len(x_parts) == 1 else [_ctx_part_spec(), _lat_part_spec()]
    in_specs = x_specs + [_row_spec(), _mod_spec(), vec, vec]
    args = list(x_parts) + [y, mod_l, g.reshape(1, D_MODEL), b.reshape(1, D_MODEL)]
    split_out = next_mod is None
    if split_out:
        out_specs = [_ctx_part_spec(), _lat_part_spec()]
        out_shape = [jax.ShapeDtypeStruct((T_CTX, D_MODEL), F32),
                     jax.ShapeDtypeStruct((T_LAT, D_MODEL), F32)]
    else:
        in_specs.append(_mod_spec())
        args.append(next_mod)
        out_specs = [_row_spec(), _row_spec()]
        out_shape = [jax.ShapeDtypeStruct((T_ALL, D_MODEL), F32),
                     jax.ShapeDtypeStruct((T_ALL, D_MODEL), BF16)]
    return pl.pallas_call(
        functools.partial(_res_ln_kernel, n_x=len(x_parts), gate_i=gate_i,
                          next_shift_i=next_shift_i, next_scale_i=next_scale_i,
                          split_out=split_out),
        grid=(T_ALL // LN_ROWS,),
        in_specs=in_specs,
        out_specs=out_specs,
        out_shape=out_shape,
        compiler_params=_cp(("arbitrary",)),
        name="res_ln",
    )(*args)


MM_ROWS = 2048
MM_COLS = 512


def _matmul_kernel(a_ref, w_ref, o_ref):
    o_ref[...] = jnp.dot(a_ref[...], w_ref[...].astype(BF16),
                         preferred_element_type=F32).astype(o_ref.dtype)


def _matmul(a, w, layer, out_dtype=F32):
    tm, tn = MM_ROWS, MM_COLS
    m, k = a.shape
    n = w.shape[2]
    return pl.pallas_call(
        _matmul_kernel,
        grid=(m // tm, n // tn),
        in_specs=[
            pl.BlockSpec((tm, k), lambda i, j: (i, 0)),
            pl.BlockSpec((None, k, tn), lambda i, j: (layer, 0, j)),
        ],
        out_specs=pl.BlockSpec((tm, tn), lambda i, j: (i, j)),
        out_shape=jax.ShapeDtypeStruct((m, n), out_dtype),
        compiler_params=_cp(("arbitrary", "arbitrary")),
        name="matmul",
    )(a, w)


N_MIX_TILES = MIX_COLS // MM_COLS


def _in_proj_kernel(a_ref, w_ref, mix_ref, gate_ref):
    j = pl.program_id(1)
    y = jnp.dot(a_ref[...], w_ref[...].astype(BF16), preferred_element_type=F32)

    @pl.when(j < N_MIX_TILES)
    def _():
        mix_ref[...] = y

    @pl.when(j >= N_MIX_TILES)
    def _():
        gate_ref[...] = jax.nn.sigmoid(y).astype(BF16)


def _in_proj(h, w_in, layer):
    tm, tn = MM_ROWS, MM_COLS
    return pl.pallas_call(
        _in_proj_kernel,
        grid=(T_ALL // tm, N_IN // tn),
        in_specs=[
            pl.BlockSpec((tm, D_MODEL), lambda i, j: (i, 0)),
            pl.BlockSpec((None, D_MODEL, tn), lambda i, j: (layer, 0, j)),
        ],
        out_specs=[
            pl.BlockSpec((tm, tn), lambda i, j: (i, jnp.minimum(j, N_MIX_TILES - 1))),
            pl.BlockSpec((tm, tn), lambda i, j: (i, jnp.maximum(j - N_MIX_TILES, 0))),
        ],
        out_shape=[jax.ShapeDtypeStruct((T_ALL, MIX_COLS), F32),
                   jax.ShapeDtypeStruct((T_ALL, N_BRANCH * D_MODEL), BF16)],
        compiler_params=_cp(("arbitrary", "arbitrary")),
        name="in_proj",
    )(h, w_in)


PAD_LO = 16
PAD_SEQ = 32
STENCIL_ROWS = PAD_LO + (ROW_BLK // SEQ) * (SEQ + PAD_SEQ)


def _stencil_layout(seq_len):
    nseq = ROW_BLK // seq_len
    stride = seq_len + PAD_SEQ
    return nseq, stride, PAD_LO + nseq * stride


def _fill_shifted(xr_ref, pieces, seq_len, residues):
    nseq, stride, rows = _stencil_layout(seq_len)
    width = xr_ref.shape[-1]
    xr_ref[0, 0:PAD_LO, :] = jnp.zeros((PAD_LO, width), F32)
    for s in range(nseq):
        b0 = PAD_LO + s * stride
        xr_ref[0, b0:b0 + seq_len, :] = pieces[s]
        xr_ref[0, b0 + seq_len:b0 + stride, :] = jnp.zeros((PAD_SEQ, width), F32)
    x0 = xr_ref[0, 0:rows, :]
    for r in residues:
        if r:
            xr_ref[r, 0:rows, :] = pltpu.roll(x0, rows - r, axis=0)


def _tap(xr_ref, r0, d, rows):
    r = d % 8
    return xr_ref[r, pl.ds(r0 + (d - r), rows), :]


def _per_path(i, body):
    @pl.when(i < N_CTX_BLK)
    def _():
        body(SEQ)

    @pl.when(i >= N_CTX_BLK)
    def _():
        body(DEC_SEQ)


POOL_CHUNK = 64


def _pool_kernel(a_ref, w_ref, s_ref, o_ref, xr_ref, pooled_ref):
    def body(seq_len):
        nseq, stride, _ = _stencil_layout(seq_len)
        for g, win in enumerate(POOL_WINDOWS):
            left = win // 2
            right = win - 1 - left
            cols = slice(g * POOL_CH, (g + 1) * POOL_CH)
            offsets = range(-left, right + 1)
            _fill_shifted(xr_ref, [a_ref[s * seq_len:(s + 1) * seq_len, cols] for s in range(nseq)],
                          seq_len, sorted({d % 8 for d in offsets}))
            for s in range(nseq):
                def chunk(c, carry, s=s, offsets=offsets, left=left, right=right):
                    t0 = c * POOL_CHUNK
                    r0 = pl.multiple_of(PAD_LO + s * stride + t0, 8)
                    x = _tap(xr_ref, r0, 0, POOL_CHUNK)
                    acc = x
                    for d in offsets:
                        if d:
                            acc = acc + _tap(xr_ref, r0, d, POOL_CHUNK)
                    pos = t0 + lax.broadcasted_iota(jnp.int32, (POOL_CHUNK, 1), 0)
                    cnt = jnp.minimum(pos + right + 1, seq_len) - jnp.maximum(pos - left, 0)
                    pooled_ref[pl.ds(pl.multiple_of(s * seq_len + t0, 8), POOL_CHUNK), :] = (
                        acc / cnt.astype(F32) - x)
                    return carry
                lax.fori_loop(0, seq_len // POOL_CHUNK, chunk, 0)
            mixed = jnp.dot(pooled_ref[...].astype(BF16), w_ref[g].astype(BF16),
                            preferred_element_type=F32)
            o_ref[:, cols] = (mixed * s_ref[:, cols]).astype(BF16)

    _per_path(pl.program_id(0), body)


def _pool_mixer(proj, pool_w, pool_scale):
    return pl.pallas_call(
        _pool_kernel,
        grid=(N_ROW_BLK,),
        in_specs=[
            pl.BlockSpec((ROW_BLK, POOL_WIDTH), lambda i: (i, COL_POOL // POOL_WIDTH)),
            pl.BlockSpec((POOL_GROUPS, POOL_CH, POOL_CH), lambda i: (0, 0, 0)),
            pl.BlockSpec((1, POOL_WIDTH), lambda i: (0, 0)),
        ],
        out_specs=pl.BlockSpec((ROW_BLK, POOL_WIDTH), lambda i: (i, 0)),
        out_shape=jax.ShapeDtypeStruct((T_ALL, POOL_WIDTH), BF16),
        scratch_shapes=[pltpu.VMEM((8, STENCIL_ROWS, POOL_CH), F32),
                        pltpu.VMEM((ROW_BLK, POOL_CH), F32)],
        compiler_params=_cp(("arbitrary",)),
        name="pool_mixer",
    )(proj, pool_w, pool_scale.reshape(1, POOL_WIDTH))


CONV_CHUNK = 32


def _conv_kernel(a_ref, g_ref, dw_ref, db_ref, lg_ref, lb_ref, o_ref, xr_ref, y_ref):
    half = CONV_K // 2

    def body(seq_len):
        nseq, stride, _ = _stencil_layout(seq_len)
        pieces = []
        for s in range(nseq):
            rows = slice(s * seq_len, (s + 1) * seq_len)
            pieces.append(a_ref[rows, :] * jax.nn.sigmoid(g_ref[rows, :]))
        _fill_shifted(xr_ref, pieces, seq_len, range(8))
        for s in range(nseq):
            def chunk(c, carry, s=s):
                t0 = c * CONV_CHUNK
                r0 = pl.multiple_of(PAD_LO + s * stride + t0, 8)
                groups = CONV_CHUNK // 8
                acc = jnp.broadcast_to(db_ref[...], (CONV_CHUNK, CONV_WIDTH)).reshape(
                    groups, 8, CONV_WIDTH)
                for k in range(CONV_K):
                    tap = _tap(xr_ref, r0, k - half, CONV_CHUNK).reshape(groups, 8, CONV_WIDTH)
                    acc = acc + tap * dw_ref[k]
                y_ref[pl.ds(pl.multiple_of(s * seq_len + t0, CONV_CHUNK), CONV_CHUNK), :] = (
                    acc.reshape(CONV_CHUNK, CONV_WIDTH))
                return carry
            lax.fori_loop(0, seq_len // CONV_CHUNK, chunk, 0, unroll=2)

    _per_path(pl.program_id(0), body)
    y = _layer_norm(y_ref[...]) * lg_ref[...] + lb_ref[...]
    o_ref[...] = (y * jax.nn.sigmoid(y)).astype(BF16)


def _conv_mixer(proj, dw, db, ln_g, ln_b):
    cb = COL_CV // CONV_WIDTH
    vec = pl.BlockSpec((1, CONV_WIDTH), lambda i: (0, 0))
    return pl.pallas_call(
        _conv_kernel,
        grid=(N_ROW_BLK,),
        in_specs=[
            pl.BlockSpec((ROW_BLK, CONV_WIDTH), lambda i: (i, cb)),
            pl.BlockSpec((ROW_BLK, CONV_WIDTH), lambda i: (i, cb + 1)),
            pl.BlockSpec((CONV_K, 8, CONV_WIDTH), lambda i: (0, 0, 0)),
            vec, vec, vec,
        ],
        out_specs=pl.BlockSpec((ROW_BLK, CONV_WIDTH), lambda i: (i, 0)),
        out_shape=jax.ShapeDtypeStruct((T_ALL, CONV_WIDTH), BF16),
        scratch_shapes=[pltpu.VMEM((8, STENCIL_ROWS, CONV_WIDTH), F32),
                        pltpu.VMEM((ROW_BLK, CONV_WIDTH), F32)],
        compiler_params=_cp(("arbitrary",)),
        name="conv_mixer",
    )(proj, proj, jnp.broadcast_to(dw[:, None, :], (CONV_K, 8, CONV_WIDTH)), db.reshape(1, -1),
      ln_g.reshape(1, -1), ln_b.reshape(1, -1))


S5_ROWS = 512


def _s5_kernel(*refs, tc, nb, reverse, post):
    u_ref, bq_ref, cq_ref, a_ref, h0_ref = refs[:5]
    k = 5
    if post:
        yf_ref, dskip_ref, wglu_ref = refs[k:k + 3]
        k += 3
    y_ref, hf_ref = refs[k:k + 2]
    y_scr, hs_ref, st_ref = refs[k + 2:]
    j = pl.program_id(0)

    @pl.when(j == 0)
    def _():
        st_ref[...] = h0_ref[...]

    rows = tc * nb
    u_tm = pltpu.einshape("btc->tbc", u_ref[...]).reshape(rows, S5_WIDTH)
    u2 = u_tm.astype(BF16)
    for q in range(S5_Q):
        hs_ref[:, q * 2 * S5_QS:(q + 1) * 2 * S5_QS] = jnp.dot(
            u2[:, q * S5_QU:(q + 1) * S5_QU], bq_ref[q], preferred_element_type=F32)

    lane_w = 4096 // nb
    for q in range(S5_Q):
        for w in range(S5_QS // lane_w):
            re0 = q * 2 * S5_QS + w * lane_w
            im0 = re0 + S5_QS
            a_re = jnp.broadcast_to(a_ref[:, re0:re0 + lane_w], (nb, lane_w))
            a_im = jnp.broadcast_to(a_ref[:, im0:im0 + lane_w], (nb, lane_w))

            def step(t, carry, re0=re0, im0=im0, a_re=a_re, a_im=a_im):
                h_re, h_im = carry
                te = tc - 1 - t if reverse else t
                r0 = pl.multiple_of(te * nb, nb)
                n_re = a_re * h_re - a_im * h_im + hs_ref[pl.ds(r0, nb), re0:re0 + lane_w]
                n_im = a_re * h_im + a_im * h_re + hs_ref[pl.ds(r0, nb), im0:im0 + lane_w]
                hs_ref[pl.ds(r0, nb), re0:re0 + lane_w] = n_re
                hs_ref[pl.ds(r0, nb), im0:im0 + lane_w] = n_im
                return n_re, n_im

            h_re, h_im = lax.fori_loop(
                0, tc, step, (st_ref[:, re0:re0 + lane_w], st_ref[:, im0:im0 + lane_w]), unroll=4)
            st_ref[:, re0:re0 + lane_w] = h_re
            st_ref[:, im0:im0 + lane_w] = h_im

    for q in range(S5_Q):
        yq = jnp.dot(hs_ref[:, q * 2 * S5_QS:(q + 1) * 2 * S5_QS].astype(BF16), cq_ref[q],
                     preferred_element_type=F32)
        y_scr[:, :, q * S5_QU:(q + 1) * S5_QU] = yq.reshape(tc, nb, S5_QU)

    if post:
        y_scr[...] = y_scr[...] + (dskip_ref[...] * u_tm).reshape(tc, nb, S5_WIDTH)
    y_bt = pltpu.einshape("tbc->btc", y_scr[...])
    if post:
        y = jax.nn.gelu(y_bt + yf_ref[...]).reshape(rows, S5_WIDTH)
        gate = jnp.dot(y.astype(BF16), wglu_ref[...].astype(BF16), preferred_element_type=F32)
        y_ref[...] = (y * jax.nn.sigmoid(gate)).astype(BF16).reshape(nb, tc, S5_WIDTH)
    else:
        y_ref[...] = y_bt

    @pl.when(j == pl.num_programs(0) - 1)
    def _():
        hf_ref[...] = st_ref[...]


def _s5_pass(proj3, direction, bq, cq, a_vec, h0, yf=None, d_skip=None, w_glu=None):
    post = yf is not None
    reverse = direction == 1
    cb = COL_S5 // S5_WIDTH
    nb, n = BATCH, SEQ
    tc = S5_ROWS // nb
    nc = n // tc

    def chunk(j):
        return nc - 1 - j if reverse else j

    seq_blk = pl.BlockSpec((nb, tc, S5_WIDTH), lambda j: (0, chunk(j), 0))
    in_specs = [
        pl.BlockSpec((nb, tc, S5_WIDTH), lambda j: (0, chunk(j), cb)),
        pl.BlockSpec((None, S5_Q, S5_QU, 2 * S5_QS), lambda j: (direction, 0, 0, 0)),
        pl.BlockSpec((None, S5_Q, 2 * S5_QS, S5_QU), lambda j: (direction, 0, 0, 0)),
        pl.BlockSpec((None, 1, S5_LANES), lambda j: (direction, 0, 0)),
        pl.BlockSpec((None, nb, S5_LANES), lambda j: (direction, 0, 0)),
    ]
    args = [proj3, bq, cq, a_vec, h0]
    if post:
        in_specs += [seq_blk, pl.BlockSpec((1, S5_WIDTH), lambda j: (0, 0)),
                     pl.BlockSpec((S5_WIDTH, S5_WIDTH), lambda j: (0, 0))]
        args += [yf, d_skip.reshape(1, S5_WIDTH), w_glu]
    return pl.pallas_call(
        functools.partial(_s5_kernel, tc=tc, nb=nb, reverse=reverse, post=post),
        grid=(nc,),
        in_specs=in_specs,
        out_specs=[seq_blk, pl.BlockSpec((nb, S5_LANES), lambda j: (0, 0))],
        out_shape=[jax.ShapeDtypeStruct((nb, n, S5_WIDTH), BF16 if post else F32),
                   jax.ShapeDtypeStruct((nb, S5_LANES), F32)],
        scratch_shapes=[
            pltpu.VMEM((tc, nb, S5_WIDTH), F32),
            pltpu.VMEM((tc * nb, S5_LANES), F32),
            pltpu.VMEM((nb, S5_LANES), F32),
        ],
        compiler_params=_cp(("arbitrary",)),
        name="s5_pass",
    )(*args)


def _s5_mixer(proj3, bq, cq, a_vec, h0, d_skip, w_glu):
    yf, hf_f = _s5_pass(proj3, 0, bq, cq, a_vec, h0)
    y, hf_b = _s5_pass(proj3, 1, bq, cq, a_vec, h0, yf, d_skip, w_glu)
    return y.reshape(-1, S5_WIDTH), jnp.stack([hf_f, hf_b])


LAT_TC = S5_ROWS // (2 * DEC_BATCH)
LAT_NC = DEC_SEQ // LAT_TC


def _s5_lat_kernel(*refs):
    nbat, tc = DEC_BATCH, LAT_TC
    uf_refs = refs[:nbat]
    ub_refs = refs[nbat:2 * nbat]
    (bq_ref, cq_ref, a_ref, h0_ref, yf_ref, yb_ref, hf_ref,
     u_scr, y_scr, hs_ref, hb_ref, st_ref) = refs[2 * nbat:]
    j = pl.program_id(0)

    @pl.when(j == 0)
    def _():
        st_ref[...] = h0_ref[...]

    u_scr[:, 0:nbat, :] = pltpu.einshape("btc->tbc", jnp.stack([r[...] for r in uf_refs], axis=0))
    u_scr[:, nbat:, :] = pltpu.einshape("btc->tbc", jnp.stack([r[...] for r in ub_refs], axis=0))
    rows = 2 * nbat * tc
    u2 = u_scr[...].reshape(rows, S5_WIDTH).astype(BF16)
    is_fwd = (lax.broadcasted_iota(jnp.int32, (rows, 1), 0) & (2 * nbat - 1)) < nbat
    zero = jnp.zeros((rows, S5_QU), BF16)
    for q in range(S5_Q):
        uq = u2[:, q * S5_QU:(q + 1) * S5_QU]
        lhs = jnp.concatenate([jnp.where(is_fwd, uq, zero), jnp.where(is_fwd, zero, uq)], axis=1)
        bu = jnp.dot(lhs, bq_ref[q], preferred_element_type=F32)
        hs_ref[:, q * 2 * S5_QS:(q + 1) * 2 * S5_QS] = bu
        hb_ref[:, q * 2 * S5_QS:(q + 1) * 2 * S5_QS] = bu

    lane_w = S5_QS
    low = lax.broadcasted_iota(jnp.int32, (2 * nbat, lane_w), 0) < nbat
    for q in range(S5_Q):
        re0 = q * 2 * S5_QS
        im0 = re0 + S5_QS
        a_re = a_ref[:, re0:re0 + lane_w]
        a_im = a_ref[:, im0:im0 + lane_w]

        def step(t, carry, re0=re0, im0=im0, a_re=a_re, a_im=a_im):
            h_re, h_im = carry
            ra = pl.multiple_of(t * 2 * nbat, 2 * nbat)
            rb = pl.multiple_of((tc - 1 - t) * 2 * nbat, 2 * nbat)
            n_re = (a_re * h_re - a_im * h_im
                    + jnp.where(low, hs_ref[pl.ds(ra, 2 * nbat), re0:re0 + lane_w],
                                hb_ref[pl.ds(rb, 2 * nbat), re0:re0 + lane_w]))
            n_im = (a_re * h_im + a_im * h_re
                    + jnp.where(low, hs_ref[pl.ds(ra, 2 * nbat), im0:im0 + lane_w],
                                hb_ref[pl.ds(rb, 2 * nbat), im0:im0 + lane_w]))
            hs_ref[pl.ds(ra, 2 * nbat), re0:re0 + lane_w] = n_re
            hs_ref[pl.ds(ra, 2 * nbat), im0:im0 + lane_w] = n_im
            hb_ref[pl.ds(rb, 2 * nbat), re0:re0 + lane_w] = n_re
            hb_ref[pl.ds(rb, 2 * nbat), im0:im0 + lane_w] = n_im
            return n_re, n_im

        h_re, h_im = lax.fori_loop(
            0, tc, step, (st_ref[:, re0:re0 + lane_w], st_ref[:, im0:im0 + lane_w]), unroll=2)
        st_ref[:, re0:re0 + lane_w] = h_re
        st_ref[:, im0:im0 + lane_w] = h_im

    for q in range(S5_Q):
        cols = slice(q * 2 * S5_QS, (q + 1) * 2 * S5_QS)
        hq = jnp.where(is_fwd, hs_ref[:, cols], hb_ref[:, cols]).astype(BF16)
        y2 = jnp.dot(hq, cq_ref[q], preferred_element_type=F32)
        yq = jnp.where(is_fwd, y2[:, :S5_QU], y2[:, S5_QU:])
        y_scr[:, :, q * S5_QU:(q + 1) * S5_QU] = yq.reshape(tc, 2 * nbat, S5_QU)
    yf_ref[...] = pltpu.einshape("tbc->btc", y_scr[:, 0:nbat, :])
    yb_ref[...] = pltpu.einshape("tbc->btc", y_scr[:, nbat:, :])

    @pl.when(j == pl.num_programs(0) - 1)
    def _():
        hf_ref[...] = st_ref[...]


def _s5_lat_post_kernel(yf_ref, yb_ref, u_ref, d_ref, w_ref, o_ref):
    y = jax.nn.gelu(yf_ref[...] + yb_ref[...] + d_ref[...] * u_ref[...])
    gate = jnp.dot(y.astype(BF16), w_ref[...].astype(BF16), preferred_element_type=F32)
    o_ref[...] = (y * jax.nn.sigmoid(gate)).astype(BF16)


def _s5_lat_mixer(proj, bq, cq, a_vec, h0, d_skip, w_glu):
    nbat, tc, nc = DEC_BATCH, LAT_TC, LAT_NC
    cb = COL_S5 // S5_WIDTH
    proj3 = proj.reshape(T_ALL // SEQ, SEQ, MIX_COLS)
    per = SEQ // tc

    def u_spec(b, mirrored):
        def index(j):
            c = nc - 1 - j if mirrored else j
            return (T_CTX // SEQ + b * (DEC_SEQ // SEQ) + c // per, c % per, cb)
        return pl.BlockSpec((None, tc, S5_WIDTH), index)

    bq2 = jnp.concatenate([bq[0], bq[1]], axis=1)
    cq2 = jnp.concatenate([cq[0], cq[1]], axis=2)
    a8 = jnp.broadcast_to(a_vec, (2, nbat, S5_LANES)).reshape(2 * nbat, S5_LANES)
    h08 = h0.reshape(2 * nbat, S5_LANES)
    yf, yb, _ = pl.pallas_call(
        _s5_lat_kernel,
        grid=(nc,),
        in_specs=[u_spec(b, False) for b in range(nbat)] + [u_spec(b, True) for b in range(nbat)] + [
            pl.BlockSpec((S5_Q, 2 * S5_QU, 2 * S5_QS), lambda j: (0, 0, 0)),
            pl.BlockSpec((S5_Q, 2 * S5_QS, 2 * S5_QU), lambda j: (0, 0, 0)),
            pl.BlockSpec((2 * nbat, S5_LANES), lambda j: (0, 0)),
            pl.BlockSpec((2 * nbat, S5_LANES), lambda j: (0, 0)),
        ],
        out_specs=[pl.BlockSpec((nbat, tc, S5_WIDTH), lambda j: (0, j, 0)),
                   pl.BlockSpec((nbat, tc, S5_WIDTH), lambda j: (0, nc - 1 - j, 0)),
                   pl.BlockSpec((2 * nbat, S5_LANES), lambda j: (0, 0))],
        out_shape=[jax.ShapeDtypeStruct((nbat, DEC_SEQ, S5_WIDTH), F32),
                   jax.ShapeDtypeStruct((nbat, DEC_SEQ, S5_WIDTH), F32),
                   jax.ShapeDtypeStruct((2 * nbat, S5_LANES), F32)],
        scratch_shapes=[
            pltpu.VMEM((tc, 2 * nbat, S5_WIDTH), F32),
            pltpu.VMEM((tc, 2 * nbat, S5_WIDTH), F32),
            pltpu.VMEM((tc * 2 * nbat, S5_LANES), F32),
            pltpu.VMEM((tc * 2 * nbat, S5_LANES), F32),
            pltpu.VMEM((2 * nbat, S5_LANES), F32),
        ],
        compiler_params=_cp(("arbitrary",)),
        name="s5_lat",
    )(*([proj3] * (2 * nbat)), bq2, cq2, a8, h08)
    tr = ROW_BLK
    row = pl.BlockSpec((tr, S5_WIDTH), lambda i: (i, 0))
    return pl.pallas_call(
        _s5_lat_post_kernel,
        grid=(T_LAT // tr,),
        in_specs=[row, row, pl.BlockSpec((tr, S5_WIDTH), lambda i: (T_CTX // tr + i, cb)),
                  pl.BlockSpec((1, S5_WIDTH), lambda i: (0, 0)),
                  pl.BlockSpec((S5_WIDTH, S5_WIDTH), lambda i: (0, 0))],
        out_specs=row,
        out_shape=jax.ShapeDtypeStruct((T_LAT, S5_WIDTH), BF16),
        compiler_params=_cp(("arbitrary",)),
        name="s5_lat_post",
    )(yf.reshape(T_LAT, S5_WIDTH), yb.reshape(T_LAT, S5_WIDTH), proj, d_skip.reshape(1, S5_WIDTH), w_glu)


def _s5_params(lam_re, lam_im, log_dt, b_re, b_im, c_re, c_im):
    dt = jnp.exp(log_dt)[..., None]
    x = lam_re * dt
    y = lam_im * dt
    ex = jnp.exp(x)
    abar_re = ex * jnp.cos(y)
    abar_im = ex * jnp.sin(y)
    num_re = jnp.expm1(x) * jnp.cos(y) - 2.0 * jnp.square(jnp.sin(0.5 * y))
    num_im = abar_im
    den = lam_re * lam_re + lam_im * lam_im
    coef_re = (num_re * lam_re + num_im * lam_im) / den
    coef_im = (num_im * lam_re - num_re * lam_im) / den
    bbar_re = coef_re[..., None] * b_re - coef_im[..., None] * b_im
    bbar_im = coef_re[..., None] * b_im + coef_im[..., None] * b_re
    eye = jnp.eye(S5_GQ, dtype=F32)
    bb = jnp.stack([bbar_re, bbar_im]).reshape(2, 2, S5_Q, S5_GQ, S5_STATE, S5_CH)
    bq = jnp.einsum("rdqgpc,gh->dqgcrhp", bb, eye).reshape(2, S5_Q, S5_QU, 2 * S5_QS)
    cc = jnp.stack([c_re, -c_im]).reshape(2, 2, S5_Q, S5_GQ, S5_CH, S5_STATE)
    cq = jnp.einsum("rdqgcp,gh->dqrhpgc", cc, eye).reshape(2, S5_Q, 2 * S5_QS, S5_QU)
    ab = jnp.stack([abar_re, abar_im]).reshape(2, 2, S5_Q, S5_GQ, S5_STATE)
    a_vec = ab.transpose(1, 2, 0, 3, 4).reshape(2, 1, S5_LANES)
    return bq.astype(BF16), cq.astype(BF16), a_vec


def _s5_state_to_lanes(st):
    b = st.shape[0]
    s = st.reshape(b, 2, 2, S5_Q, S5_GQ, S5_STATE).transpose(1, 0, 3, 2, 4, 5)
    return s.reshape(2, b, S5_LANES)


def _s5_lanes_to_state(hl):
    b = hl.shape[1]
    s = hl.reshape(2, b, S5_Q, 2, S5_GQ, S5_STATE).transpose(1, 0, 3, 2, 4, 5)
    return s.reshape(b, 2, 2, S5_GROUPS, S5_STATE)


def _rope(x, cos, sin_signed):
    lane = lax.broadcasted_iota(jnp.int32, x.shape, 1)
    quarter = HEAD_DIM // 4
    partner = jnp.where((lane & quarter) == 0,
                        pltpu.roll(x, HEAD_DIM - quarter, axis=1),
                        pltpu.roll(x, quarter, axis=1))
    return x * cos + partner * sin_signed


def _nt_dot(a, b):
    return lax.dot_general(a, b, (((1,), (1,)), ((), ())), preferred_element_type=F32)


def _stack_heads(q_ref, rope=None):
    parts = []
    for r in range(REP):
        q = q_ref[:, r * HEAD_DIM:(r + 1) * HEAD_DIM]
        if rope is not None:
            q = _rope(q, *rope)
        parts.append(q.astype(BF16))
    return jnp.concatenate(parts, axis=0)


def _with_ones(v):
    return jnp.concatenate([v.astype(BF16), jnp.ones(v.shape, BF16)], axis=1)


def _sink_column(sink_ref, h0, rows):
    return jnp.concatenate(
        [jnp.broadcast_to(sink_ref[pl.ds(h0 + r, 1), 0:1], (rows, 1)) for r in range(REP)], axis=0)


def _ctx_attn_kernel(q_ref, k_ref, v_ref, sink_ref, o_ref, ko_ref, vo_ref):
    h0 = pl.program_id(1) * REP
    k = k_ref[...]
    v = v_ref[...]
    ko_ref[...] = k
    vo_ref[...] = v
    q = _stack_heads(q_ref)
    s = _nt_dot(q, k.astype(BF16)) * (HEAD_DIM ** -0.5)
    sink = _sink_column(sink_ref, h0, SEQ)
    m = jnp.maximum(jnp.max(s, axis=-1, keepdims=True), sink)
    p = jnp.exp(s - m)
    acc = jnp.dot(p.astype(BF16), _with_ones(v), preferred_element_type=F32)
    o = acc[:, :HEAD_DIM] / (acc[:, HEAD_DIM:HEAD_DIM + 1] + jnp.exp(sink - m))
    for r in range(REP):
        o_ref[:, r * HEAD_DIM:(r + 1) * HEAD_DIM] = o[r * SEQ:(r + 1) * SEQ].astype(BF16)


def _ctx_attention(proj, sink_tile):
    qw = REP * HEAD_DIM
    kv_out = pl.BlockSpec((SEQ, HEAD_DIM), lambda b, h: (b, h))
    return pl.pallas_call(
        _ctx_attn_kernel,
        grid=(BATCH, N_KV_HEADS),
        in_specs=[
            pl.BlockSpec((SEQ, qw), lambda b, h: (b, COL_Q // qw + h)),
            pl.BlockSpec((SEQ, HEAD_DIM), lambda b, h: (b, COL_K // HEAD_DIM + h)),
            pl.BlockSpec((SEQ, HEAD_DIM), lambda b, h: (b, COL_V // HEAD_DIM + h)),
            pl.BlockSpec((N_HEADS, HEAD_DIM), lambda b, h: (0, 0)),
        ],
        out_specs=[pl.BlockSpec((SEQ, qw), lambda b, h: (b, h)), kv_out, kv_out],
        out_shape=[jax.ShapeDtypeStruct((T_CTX, ATTN_WIDTH), BF16),
                   jax.ShapeDtypeStruct((T_CTX, KV_WIDTH), F32),
                   jax.ShapeDtypeStruct((T_CTX, KV_WIDTH), F32)],
        compiler_params=_cp(("arbitrary", "arbitrary")),
        name="ctx_attention",
    )(proj, proj, proj, sink_tile)


LAT_TQ = 256
LAT_KWIN = LAT_TQ + 2 * WINDOW


def _lat_attn_kernel(q_ref, k_ref, v_ref, kc_ref, vc_ref, cq_ref, sq_ref, ck_ref, sk_ref,
                     sink_ref, o_ref):
    h0 = pl.program_id(1) * REP
    q0 = pl.program_id(2) * LAT_TQ
    k0 = pl.multiple_of(jnp.clip(q0 - WINDOW, 0, DEC_SEQ - LAT_KWIN), WINDOW)
    win = pl.ds(k0, LAT_KWIN)
    kb = _rope(k_ref[win, :], ck_ref[win, :], sk_ref[win, :]).astype(BF16)
    rows = REP * LAT_TQ
    row = lax.broadcasted_iota(jnp.int32, (rows, LAT_KWIN), 0)
    qpos = q0 + (row & (LAT_TQ - 1))
    kpos = k0 + lax.broadcasted_iota(jnp.int32, (rows, LAT_KWIN), 1)
    mask = jnp.abs(qpos - kpos) <= WINDOW
    scale = HEAD_DIM ** -0.5
    q = _stack_heads(q_ref, (cq_ref[...], sq_ref[...]))
    s_lat = jnp.where(mask, _nt_dot(q, kb) * scale, NEG_INF)
    s_ctx = _nt_dot(q, kc_ref[...].astype(BF16)) * scale
    sink = _sink_column(sink_ref, h0, LAT_TQ)
    m = jnp.maximum(jnp.maximum(jnp.max(s_lat, axis=-1, keepdims=True),
                                jnp.max(s_ctx, axis=-1, keepdims=True)), sink)
    p_lat = jnp.exp(s_lat - m).astype(BF16)
    p_ctx = jnp.exp(s_ctx - m).astype(BF16)
    acc = (jnp.dot(p_lat, _with_ones(v_ref[win, :]), preferred_element_type=F32)
           + jnp.dot(p_ctx, _with_ones(vc_ref[...]), preferred_element_type=F32))
    o = acc[:, :HEAD_DIM] / (acc[:, HEAD_DIM:HEAD_DIM + 1] + jnp.exp(sink - m))
    for r in range(REP):
        o_ref[:, r * HEAD_DIM:(r + 1) * HEAD_DIM] = o[r * LAT_TQ:(r + 1) * LAT_TQ].astype(BF16)


def _lat_attention(proj, cache_k_l, cache_v_l, rope_cos, rope_sin, sink_tile):
    tq = LAT_TQ
    nq = DEC_SEQ // tq
    qw = REP * HEAD_DIM
    row0_q = T_CTX // tq
    row0_k = T_CTX // DEC_SEQ
    tab_q = pl.BlockSpec((tq, HEAD_DIM), lambda b, h, i: (i, 0))
    tab_k = pl.BlockSpec((DEC_SEQ, HEAD_DIM), lambda b, h, i: (0, 0))
    return pl.pallas_call(
        _lat_attn_kernel,
        grid=(DEC_BATCH, N_KV_HEADS, nq),
        in_specs=[
            pl.BlockSpec((tq, qw), lambda b, h, i: (row0_q + b * nq + i, COL_Q // qw + h)),
            pl.BlockSpec((DEC_SEQ, HEAD_DIM), lambda b, h, i: (row0_k + b, COL_K // HEAD_DIM + h)),
            pl.BlockSpec((DEC_SEQ, HEAD_DIM), lambda b, h, i: (row0_k + b, COL_V // HEAD_DIM + h)),
            pl.BlockSpec((None, PAST_LEN, HEAD_DIM), lambda b, h, i: (b, 0, h)),
            pl.BlockSpec((None, PAST_LEN, HEAD_DIM), lambda b, h, i: (b, 0, h)),
            tab_q, tab_q, tab_k, tab_k,
            pl.BlockSpec((N_HEADS, HEAD_DIM), lambda b, h, i: (0, 0)),
        ],
        out_specs=pl.BlockSpec((tq, qw), lambda b, h, i: (b * nq + i, h)),
        out_shape=jax.ShapeDtypeStruct((T_LAT, ATTN_WIDTH), BF16),
        compiler_params=_cp(("arbitrary", "arbitrary", "arbitrary")),
        name="lat_attention",
    )(proj, proj, proj, cache_k_l, cache_v_l, rope_cos, rope_sin, rope_cos, rope_sin, sink_tile)


def _rope_tables():
    t = jnp.arange(DEC_SEQ)
    row, col = t // GRID_W, t % GRID_W
    quarter = HEAD_DIM // 4
    inv = ROPE_BASE ** (-jnp.arange(quarter, dtype=F32) / quarter)
    lane = jnp.arange(HEAD_DIM)
    pos = jnp.where(lane[None, :] < HEAD_DIM // 2, row[:, None], col[:, None]).astype(F32)
    ang = pos * inv[lane % quarter][None, :]
    sign = jnp.where((lane & quarter) == 0, -1.0, 1.0).astype(F32)
    return jnp.cos(ang), jnp.sin(ang) * sign[None, :]


def _merge_kernel(yp_ref, ysc_ref, ysl_ref, yac_ref, yal_ref, yc_ref, g0_ref, g1_ref, g2_ref,
                  g3_ref, wp_ref, ws_ref, wa_ref, wc_ref, o_ref):
    is_ctx = pl.program_id(1) < N_CTX_BLK
    y_s5 = jnp.where(is_ctx, ysc_ref[...], ysl_ref[...])
    attn = jnp.where(is_ctx, yac_ref[...], yal_ref[...])
    acc = None
    for y, g_ref, w_ref in ((yp_ref[...], g0_ref, wp_ref), (y_s5, g1_ref, ws_ref),
                            (attn, g2_ref, wa_ref), (yc_ref[...], g3_ref, wc_ref)):
        t = g_ref[...].astype(F32) * jnp.dot(y, w_ref[...].astype(BF16), preferred_element_type=F32)
        acc = t if acc is None else acc + t
    o_ref[...] = acc.astype(BF16)


def _merge(gates, y_pool, y_s5_ctx, y_s5_lat, attn_ctx, attn_lat, y_conv, layer,
           w_pool, w_s5, w_attn, w_conv):
    tm, tn = ROW_BLK, 512
    per = D_MODEL // tn

    def act(width):
        return pl.BlockSpec((tm, width), lambda j, i: (i, 0))

    def ctx_part(width):
        return pl.BlockSpec((tm, width), lambda j, i: (jnp.minimum(i, N_CTX_BLK - 1), 0))

    def lat_part(width):
        return pl.BlockSpec((tm, width), lambda j, i: (jnp.maximum(i - N_CTX_BLK, 0), 0))

    def gate(branch):
        return pl.BlockSpec((tm, tn), lambda j, i: (i, branch * per + j))

    def wt(width):
        return pl.BlockSpec((None, width, tn), lambda j, i: (layer, 0, j))

    return pl.pallas_call(
        _merge_kernel,
        grid=(D_MODEL // tn, T_ALL // tm),
        in_specs=[act(POOL_WIDTH), ctx_part(S5_WIDTH), lat_part(S5_WIDTH),
                  ctx_part(ATTN_WIDTH), lat_part(ATTN_WIDTH), act(CONV_WIDTH),
                  gate(0), gate(1), gate(2), gate(3),
                  wt(POOL_WIDTH), wt(S5_WIDTH), wt(ATTN_WIDTH), wt(CONV_WIDTH)],
        out_specs=pl.BlockSpec((tm, tn), lambda j, i: (i, j)),
        out_shape=jax.ShapeDtypeStruct((T_ALL, D_MODEL), BF16),
        compiler_params=_cp(("arbitrary", "arbitrary")),
        name="merge",
    )(y_pool, y_s5_ctx, y_s5_lat, attn_ctx, attn_lat, y_conv, gates, gates, gates, gates,
      w_pool, w_s5, w_attn, w_conv)


FFN_TILE = 256
N_FFN_TILE = D_FF // FFN_TILE


def _ffn_kernel(h_ref, wg_ref, wv_ref, dwg_ref, dwv_ref, dbg_ref, dbv_ref, wd_ref, o_ref,
                ug_ref, uv_ref, acc_ref):
    i = pl.program_id(0)
    f = pl.program_id(1)

    def up(slot):
        h = h_ref[...]
        ug_ref[slot] = jnp.dot(h, wg_ref[...].astype(BF16), preferred_element_type=F32)
        uv_ref[slot] = jnp.dot(h, wv_ref[...].astype(BF16), preferred_element_type=F32)

    def down(slot):
        pos, seq_len = _seq_pos(i, ROW_BLK)
        first = pos == 0
        last = pos == seq_len - 1

        def conv3(u, dw_ref, db_ref):
            prev = jnp.where(first, 0.0, _shift_rows(u, -1))
            nxt = jnp.where(last, 0.0, _shift_rows(u, 1))
            return prev * dw_ref[0:1, :] + u * dw_ref[1:2, :] + nxt * dw_ref[2:3, :] + db_ref[...]

        gt = conv3(ug_ref[slot], dwg_ref, dbg_ref)
        val = conv3(uv_ref[slot], dwv_ref, dbv_ref)
        act = (gt * jax.nn.sigmoid(gt) * val).astype(BF16)
        acc_ref[...] += jnp.dot(act, wd_ref[...].astype(BF16), preferred_element_type=F32)

    @pl.when(f == 0)
    def _():
        acc_ref[...] = jnp.zeros_like(acc_ref)
        up(0)

    for par in (0, 1):
        @pl.when(jnp.logical_and(jnp.logical_and(f > 0, f < N_FFN_TILE), f % 2 == par))
        def _(par=par):
            up(par)
            down(1 - par)

    @pl.when(f == N_FFN_TILE)
    def _():
        down((N_FFN_TILE - 1) % 2)
        o_ref[...] = acc_ref[...].astype(o_ref.dtype)


def _conv_ffn(h, layer, w_up, dw, db, w_down):
    tf, nf = FFN_TILE, N_FFN_TILE
    db3 = db.reshape(DEPTH, 1, 2 * D_FF)

    def cur(f):
        return jnp.minimum(f, nf - 1)

    def prv(f):
        return jnp.maximum(f - 1, 0)

    return pl.pallas_call(
        _ffn_kernel,
        grid=(N_ROW_BLK, nf + 1),
        in_specs=[
            pl.BlockSpec((ROW_BLK, D_MODEL), lambda i, f: (i, 0)),
            pl.BlockSpec((None, D_MODEL, tf), lambda i, f: (layer, 0, cur(f))),
            pl.BlockSpec((None, D_MODEL, tf), lambda i, f: (layer, 0, nf + cur(f))),
            pl.BlockSpec((None, FFN_CONV_K, tf), lambda i, f: (layer, 0, prv(f))),
            pl.BlockSpec((None, FFN_CONV_K, tf), lambda i, f: (layer, 0, nf + prv(f))),
            pl.BlockSpec((None, 1, tf), lambda i, f: (layer, 0, prv(f))),
            pl.BlockSpec((None, 1, tf), lambda i, f: (layer, 0, nf + prv(f))),
            pl.BlockSpec((None, tf, D_MODEL), lambda i, f: (layer, prv(f), 0)),
        ],
        out_specs=pl.BlockSpec((ROW_BLK, D_MODEL), lambda i, f: (i, 0)),
        out_shape=jax.ShapeDtypeStruct((T_ALL, D_MODEL), BF16),
        scratch_shapes=[pltpu.VMEM((2, ROW_BLK, tf), F32), pltpu.VMEM((2, ROW_BLK, tf), F32),
                        pltpu.VMEM((ROW_BLK, D_MODEL), F32)],
        compiler_params=_cp(("arbitrary", "arbitrary")),
        name="conv_ffn",
    )(h, w_up, w_up, dw, dw, db3, db3, w_down)


def kernel(x_prompt, x_sample, cache_k, cache_v, state_s5, c, c_ctx, w_ada, b_ada, w_in, pool_w, pool_scale, s5_lambda_re, s5_lambda_im, s5_log_dt, s5_b_re, s5_b_im, s5_c_re, s5_c_im, s5_d, s5_w_glu, attn_sink, conv_dw, conv_db, conv_ln_g, conv_ln_b, w_br_pool, w_br_s5, w_br_attn, w_br_conv, w_out, ln1_g, ln1_b, ffn_w_up, ffn_dw, ffn_db, ffn_w_down, ln2_g, ln2_b):
    x_parts = (x_prompt.reshape(T_CTX, D_MODEL), x_sample.reshape(T_LAT, D_MODEL))
    cvec = jnp.concatenate(
        [c_ctx[None, :], c, jnp.zeros((MOD_ROWS - 1 - DEC_BATCH, D_MODEL), F32)], axis=0)
    mod = _mod_table(cvec, w_ada, b_ada).reshape(DEPTH, MOD_ROWS, 6, D_MODEL)
    rope_cos, rope_sin = _rope_tables()

    ks_out, vs_out, ss_out = [], [], []
    h = _ln_mod(x_parts[0], x_parts[1], mod[0], 0, 1)
    for l in range(DEPTH):
        proj, gates = _in_proj(h, w_in, l)

        y_pool = _pool_mixer(proj, pool_w[l], pool_scale[l])
        y_conv = _conv_mixer(proj, conv_dw[l], conv_db[l], conv_ln_g[l], conv_ln_b[l])

        bq, cq, a_vec = _s5_params(s5_lambda_re[l], s5_lambda_im[l], s5_log_dt[l], s5_b_re[l],
                                   s5_b_im[l], s5_c_re[l], s5_c_im[l])
        proj3 = proj.reshape(T_ALL // SEQ, SEQ, MIX_COLS)
        h0_ctx = jnp.zeros((2, BATCH, S5_LANES), F32)
        y_s5_ctx, hf_ctx = _s5_mixer(proj3, bq, cq, a_vec, h0_ctx, s5_d[l], s5_w_glu[l])
        y_s5_lat = _s5_lat_mixer(proj, bq, cq, a_vec, _s5_state_to_lanes(state_s5[:, l]),
                                 s5_d[l], s5_w_glu[l])

        sink_tile = jnp.broadcast_to(attn_sink[l][:, None], (N_HEADS, HEAD_DIM))
        attn_ctx, k_new, v_new = _ctx_attention(proj, sink_tile)
        attn_lat = _lat_attention(proj, cache_k[:, l].reshape(DEC_BATCH, PAST_LEN, KV_WIDTH),
                                  cache_v[:, l].reshape(DEC_BATCH, PAST_LEN, KV_WIDTH),
                                  rope_cos, rope_sin, sink_tile)

        merged = _merge(gates, y_pool, y_s5_ctx, y_s5_lat, attn_ctx, attn_lat, y_conv, l,
                        w_br_pool, w_br_s5, w_br_attn, w_br_conv)
        mixed = _matmul(merged, w_out, l, out_dtype=BF16)
        x, h2 = _res_ln(x_parts, mixed, mod[l], 2, ln1_g[l], ln1_b[l], mod[l], 3, 4)
        f = _conv_ffn(h2, l, ffn_w_up, ffn_dw, ffn_db, ffn_w_down)
        if l + 1 < DEPTH:
            x, h = _res_ln((x,), f, mod[l], 5, ln2_g[l], ln2_b[l], mod[l + 1], 0, 1)
            x_parts = (x,)
        else:
            y_prompt, y_sample = _res_ln((x,), f, mod[l], 5, ln2_g[l], ln2_b[l])

        ks_out.append(k_new.reshape(BATCH, SEQ, N_KV_HEADS, HEAD_DIM))
        vs_out.append(v_new.reshape(BATCH, SEQ, N_KV_HEADS, HEAD_DIM))
        ss_out.append(_s5_lanes_to_state(hf_ctx))

    return (y_prompt.reshape(BATCH, SEQ, D_MODEL), y_sample.reshape(DEC_BATCH, DEC_SEQ, D_MODEL),
            jnp.stack(ks_out, axis=1), jnp.stack(vs_out, axis=1), jnp.stack(ss_out, axis=1))
```

```python
import functools

import jax
import jax.numpy as jnp
from jax import lax
from jax.experimental import pallas as pl
from jax.experimental.pallas import tpu as pltpu

F32 = jnp.float32
BF16 = jnp.bfloat16

D_MODEL = 2048
BATCH = 16
SEQ = 256
DEPTH = 2
DEC_BATCH = 4
DEC_SEQ = 1024
PAST_LEN = 512
GRID_W = 64
POOL_WIDTH = 512
POOL_GROUPS = 4
POOL_CH = POOL_WIDTH // POOL_GROUPS
POOL_WINDOWS = (2, 4, 8, 16)
S5_WIDTH = 512
S5_CH = 16
S5_GROUPS = S5_WIDTH // S5_CH
S5_STATE = 64
N_HEADS = 8
N_KV_HEADS = 2
HEAD_DIM = 128
REP = N_HEADS // N_KV_HEADS
ATTN_WIDTH = N_HEADS * HEAD_DIM
KV_WIDTH = N_KV_HEADS * HEAD_DIM
WINDOW = 128
ROPE_BASE = 10000.0
NEG_INF = -1e30
CONV_WIDTH = 512
CONV_K = 31
N_BRANCH = 4
D_FF = 5632
FFN_CONV_K = 3
ALPHA = (2.0 * DEPTH) ** 0.25
LN_EPS = 1e-5
MIX_COLS = POOL_WIDTH + S5_WIDTH + ATTN_WIDTH + 2 * KV_WIDTH + 2 * CONV_WIDTH
N_IN = MIX_COLS + N_BRANCH * D_MODEL

COL_POOL = 0
COL_S5 = POOL_WIDTH
COL_Q = COL_S5 + S5_WIDTH
COL_K = COL_Q + ATTN_WIDTH
COL_V = COL_K + KV_WIDTH
COL_CV = COL_V + KV_WIDTH
COL_GATE = MIX_COLS

T_CTX = BATCH * SEQ
T_LAT = DEC_BATCH * DEC_SEQ
T_ALL = T_CTX + T_LAT
ROW_BLK = 1024
N_ROW_BLK = T_ALL // ROW_BLK
N_CTX_BLK = T_CTX // ROW_BLK
MOD_ROWS = 8
LN_ROWS = 512

S5_Q = 4
S5_GQ = S5_GROUPS // S5_Q
S5_QU = S5_GQ * S5_CH
S5_QS = S5_GQ * S5_STATE
S5_LANES = 2 * S5_GROUPS * S5_STATE

VMEM_LIMIT = 56 * 1024 * 1024


def _cp(sem):
    return pltpu.CompilerParams(dimension_semantics=sem, vmem_limit_bytes=VMEM_LIMIT)


def _mod_row(row_block, rows_per_block):
    lat_batch = (row_block * rows_per_block) // DEC_SEQ - T_CTX // DEC_SEQ
    return jnp.maximum(lat_batch + 1, 0)


def _seq_pos(row_block, rows):
    r = lax.broadcasted_iota(jnp.int32, (rows, 1), 0)
    seq_len = jnp.where(row_block < N_CTX_BLK, SEQ, DEC_SEQ)
    return r & (seq_len - 1), seq_len


def _layer_norm(x):
    mu = jnp.mean(x, axis=-1, keepdims=True)
    xc = x - mu
    var = jnp.mean(xc * xc, axis=-1, keepdims=True)
    return xc * lax.rsqrt(var + LN_EPS)


def _shift_rows(x, d):
    n = x.shape[0]
    return pltpu.roll(x, (-d) % n, axis=0)


def _split_bf16(x):
    hi = x.astype(BF16)
    return hi, (x - hi.astype(F32)).astype(BF16)


MOD_TN = 2048
MOD_KC = 256


def _mod_kernel(c_ref, w_ref, b_ref, o_ref):
    c = c_ref[...]
    a_hi, a_lo = _split_bf16(c * jax.nn.sigmoid(c))
    acc = jnp.broadcast_to(b_ref[0], (MOD_ROWS, MOD_TN))
    for k0 in range(0, D_MODEL, MOD_KC):
        w_hi, w_lo = _split_bf16(w_ref[0, k0:k0 + MOD_KC, :])
        ah = a_hi[:, k0:k0 + MOD_KC]
        acc = acc + jnp.dot(ah, w_hi, preferred_element_type=F32)
        acc = acc + jnp.dot(a_lo[:, k0:k0 + MOD_KC], w_hi, preferred_element_type=F32)
        acc = acc + jnp.dot(ah, w_lo, preferred_element_type=F32)
    o_ref[0] = acc


def _mod_table(cvec, w_ada, b_ada):
    tn = MOD_TN
    n6 = 6 * D_MODEL
    return pl.pallas_call(
        _mod_kernel,
        grid=(DEPTH, n6 // tn),
        in_specs=[
            pl.BlockSpec((MOD_ROWS, D_MODEL), lambda l, j: (0, 0)),
            pl.BlockSpec((1, D_MODEL, tn), lambda l, j: (l, 0, j)),
            pl.BlockSpec((1, 1, tn), lambda l, j: (l, 0, j)),
        ],
        out_specs=pl.BlockSpec((1, MOD_ROWS, tn), lambda l, j: (l, 0, j)),
        out_shape=jax.ShapeDtypeStruct((DEPTH, MOD_ROWS, n6), F32),
        compiler_params=_cp(("arbitrary", "arbitrary")),
        name="mod_table",
    )(cvec, w_ada, b_ada.reshape(DEPTH, 1, n6))


N_CTX_LN = T_CTX // LN_ROWS


def _row_spec():
    return pl.BlockSpec((LN_ROWS, D_MODEL), lambda i: (i, 0))


def _ctx_part_spec():
    return pl.BlockSpec((LN_ROWS, D_MODEL), lambda i: (jnp.minimum(i, N_CTX_LN - 1), 0))


def _lat_part_spec():
    return pl.BlockSpec((LN_ROWS, D_MODEL), lambda i: (jnp.maximum(i - N_CTX_LN, 0), 0))


def _mod_spec():
    return pl.BlockSpec((1, 6, D_MODEL), lambda i: (_mod_row(i, LN_ROWS), 0, 0))


def _read_rows(refs):
    if len(refs) == 1:
        return refs[0][...]
    return jnp.where(pl.program_id(0) < N_CTX_LN, refs[0][...], refs[1][...])


def _ln_mod_kernel(xc_ref, xl_ref, mod_ref, h_ref, *, shift_i, scale_i):
    y = _layer_norm(_read_rows((xc_ref, xl_ref)))
    scale = mod_ref[0, scale_i:scale_i + 1, :]
    shift = mod_ref[0, shift_i:shift_i + 1, :]
    h_ref[...] = (y * (1.0 + scale) + shift).astype(BF16)


def _ln_mod(x_ctx, x_lat, mod_l, shift_i, scale_i):
    return pl.pallas_call(
        functools.partial(_ln_mod_kernel, shift_i=shift_i, scale_i=scale_i),
        grid=(T_ALL // LN_ROWS,),
        in_specs=[_ctx_part_spec(), _lat_part_spec(), _mod_spec()],
        out_specs=_row_spec(),
        out_shape=jax.ShapeDtypeStruct((T_ALL, D_MODEL), BF16),
        compiler_params=_cp(("arbitrary",)),
        name="ln_mod",
    )(x_ctx, x_lat, mod_l)


def _res_ln_kernel(*refs, n_x, gate_i, next_shift_i, next_scale_i, split_out):
    x_refs = refs[:n_x]
    y_ref, mod_ref, g_ref, b_ref = refs[n_x:n_x + 4]
    rest = refs[n_x + 4:]
    gate = mod_ref[0, gate_i:gate_i + 1, :]
    z = ALPHA * _read_rows(x_refs) + gate * y_ref[...].astype(F32)
    xn = _layer_norm(z) * g_ref[...] + b_ref[...]
    if split_out:
        xc_ref, xl_ref = rest

        @pl.when(pl.program_id(0) < N_CTX_LN)
        def _():
            xc_ref[...] = xn

        @pl.when(pl.program_id(0) >= N_CTX_LN)
        def _():
            xl_ref[...] = xn
    else:
        nmod_ref, xo_ref, h_ref = rest
        xo_ref[...] = xn
        scale = nmod_ref[0, next_scale_i:next_scale_i + 1, :]
        shift = nmod_ref[0, next_shift_i:next_shift_i + 1, :]
        h_ref[...] = (_layer_norm(xn) * (1.0 + scale) + shift).astype(BF16)


def _res_ln(x_parts, y, mod_l, gate_i, g, b, next_mod=None, next_shift_i=None, next_scale_i=None):
    vec = pl.BlockSpec((1, D_MODEL), lambda i: (0, 0))
    x_specs = [_row_spec()] if len(x_parts) == 1 else [_ctx_part_spec(), _lat_part_spec()]
    in_specs = x_specs + [_row_spec(), _mod_spec(), vec, vec]
    args = list(x_parts) + [y, mod_l, g.reshape(1, D_MODEL), b.reshape(1, D_MODEL)]
    split_out = next_mod is None
    if split_out:
        out_specs = [_ctx_part_spec(), _lat_part_spec()]
        out_shape = [jax.ShapeDtypeStruct((T_CTX, D_MODEL), F32),
                     jax.ShapeDtypeStruct((T_LAT, D_MODEL), F32)]
    else:
        in_specs.append(_mod_spec())
        args.append(next_mod)
        out_specs = [_row_spec(), _row_spec()]
        out_shape = [jax.ShapeDtypeStruct((T_ALL, D_MODEL), F32),
                     jax.ShapeDtypeStruct((T_ALL, D_MODEL), BF16)]
    return pl.pallas_call(
        functools.partial(_res_ln_kernel, n_x=len(x_parts), gate_i=gate_i,
                          next_shift_i=next_shift_i, next_scale_i=next_scale_i,
                          split_out=split_out),
        grid=(T_ALL // LN_ROWS,),
        in_specs=in_specs,
        out_specs=out_specs,
        out_shape=out_shape,
        compiler_params=_cp(("arbitrary",)),
        name="res_ln",
    )(*args)


def _matmul_kernel(a_ref, w_ref, o_ref, *, gate):
    y = jnp.dot(a_ref[...], w_ref[...].astype(BF16), preferred_element_type=F32)
    if gate:
        y = jax.nn.sigmoid(y)
    o_ref[...] = y.astype(o_ref.dtype)


def _matmul(a, w, layer, col0=0, ncols=None, out_dtype=F32, gate=False):
    tm, tn = 2048, 512
    m, k = a.shape
    ncols = w.shape[2] - col0 if ncols is None else ncols
    assert col0 % tn == 0 and ncols % tn == 0 and m % tm == 0
    jb = col0 // tn
    return pl.pallas_call(
        functools.partial(_matmul_kernel, gate=gate),
        grid=(m // tm, ncols // tn),
        in_specs=[
            pl.BlockSpec((tm, k), lambda i, j: (i, 0)),
            pl.BlockSpec((None, k, tn), lambda i, j: (layer, 0, jb + j)),
        ],
        out_specs=pl.BlockSpec((tm, tn), lambda i, j: (i, j)),
        out_shape=jax.ShapeDtypeStruct((m, ncols), out_dtype),
        compiler_params=_cp(("arbitrary", "arbitrary")),
        name="matmul",
    )(a, w)


PAD_LO = 16
PAD_SEQ = 32
STENCIL_ROWS = PAD_LO + (ROW_BLK // SEQ) * (SEQ + PAD_SEQ)


def _stencil_layout(seq_len):
    nseq = ROW_BLK // seq_len
    stride = seq_len + PAD_SEQ
    return nseq, stride, PAD_LO + nseq * stride


def _fill_shifted(xr_ref, pieces, seq_len, residues):
    nseq, stride, rows = _stencil_layout(seq_len)
    width = xr_ref.shape[-1]
    xr_ref[0, 0:PAD_LO, :] = jnp.zeros((PAD_LO, width), F32)
    for s in range(nseq):
        b0 = PAD_LO + s * stride
        xr_ref[0, b0:b0 + seq_len, :] = pieces[s]
        xr_ref[0, b0 + seq_len:b0 + stride, :] = jnp.zeros((PAD_SEQ, width), F32)
    x0 = xr_ref[0, 0:rows, :]
    for r in residues:
        if r:
            xr_ref[r, 0:rows, :] = pltpu.roll(x0, rows - r, axis=0)


def _tap(xr_ref, r0, d, rows):
    r = d % 8
    return xr_ref[r, pl.ds(r0 + (d - r), rows), :]


def _per_path(i, body):
    @pl.when(i < N_CTX_BLK)
    def _():
        body(SEQ)

    @pl.when(i >= N_CTX_BLK)
    def _():
        body(DEC_SEQ)


POOL_CHUNK = 64


def _pool_kernel(a_ref, w_ref, s_ref, o_ref, xr_ref, pooled_ref):
    def body(seq_len):
        nseq, stride, _ = _stencil_layout(seq_len)
        for g, win in enumerate(POOL_WINDOWS):
            left = win // 2
            right = win - 1 - left
            cols = slice(g * POOL_CH, (g + 1) * POOL_CH)
            offsets = range(-left, right + 1)
            _fill_shifted(xr_ref, [a_ref[s * seq_len:(s + 1) * seq_len, cols] for s in range(nseq)],
                          seq_len, sorted({d % 8 for d in offsets}))
            for s in range(nseq):
                def chunk(c, carry, s=s, offsets=offsets, left=left, right=right):
                    t0 = c * POOL_CHUNK
                    r0 = pl.multiple_of(PAD_LO + s * stride + t0, 8)
                    x = _tap(xr_ref, r0, 0, POOL_CHUNK)
                    acc = x
                    for d in offsets:
                        if d:
                            acc = acc + _tap(xr_ref, r0, d, POOL_CHUNK)
                    pos = t0 + lax.broadcasted_iota(jnp.int32, (POOL_CHUNK, 1), 0)
                    cnt = jnp.minimum(pos + right + 1, seq_len) - jnp.maximum(pos - left, 0)
                    pooled_ref[pl.ds(pl.multiple_of(s * seq_len + t0, 8), POOL_CHUNK), :] = (
                        acc / cnt.astype(F32) - x)
                    return carry
                lax.fori_loop(0, seq_len // POOL_CHUNK, chunk, 0)
            mixed = jnp.dot(pooled_ref[...].astype(BF16), w_ref[g].astype(BF16),
                            preferred_element_type=F32)
            o_ref[:, cols] = (mixed * s_ref[:, cols]).astype(BF16)

    _per_path(pl.program_id(0), body)


def _pool_mixer(proj, pool_w, pool_scale):
    return pl.pallas_call(
        _pool_kernel,
        grid=(N_ROW_BLK,),
        in_specs=[
            pl.BlockSpec((ROW_BLK, POOL_WIDTH), lambda i: (i, COL_POOL // POOL_WIDTH)),
            pl.BlockSpec((POOL_GROUPS, POOL_CH, POOL_CH), lambda i: (0, 0, 0)),
            pl.BlockSpec((1, POOL_WIDTH), lambda i: (0, 0)),
        ],
        out_specs=pl.BlockSpec((ROW_BLK, POOL_WIDTH), lambda i: (i, 0)),
        out_shape=jax.ShapeDtypeStruct((T_ALL, POOL_WIDTH), BF16),
        scratch_shapes=[pltpu.VMEM((8, STENCIL_ROWS, POOL_CH), F32),
                        pltpu.VMEM((ROW_BLK, POOL_CH), F32)],
        compiler_params=_cp(("arbitrary",)),
        name="pool_mixer",
    )(proj, pool_w, pool_scale.reshape(1, POOL_WIDTH))


CONV_CHUNK = 32


def _conv_kernel(a_ref, g_ref, dw_ref, db_ref, lg_ref, lb_ref, o_ref, xr_ref, y_ref):
    half = CONV_K // 2

    def body(seq_len):
        nseq, stride, _ = _stencil_layout(seq_len)
        pieces = []
        for s in range(nseq):
            rows = slice(s * seq_len, (s + 1) * seq_len)
            pieces.append(a_ref[rows, :] * jax.nn.sigmoid(g_ref[rows, :]))
        _fill_shifted(xr_ref, pieces, seq_len, range(8))
        for s in range(nseq):
            def chunk(c, carry, s=s):
                t0 = c * CONV_CHUNK
                r0 = pl.multiple_of(PAD_LO + s * stride + t0, 8)
                groups = CONV_CHUNK // 8
                acc = jnp.broadcast_to(db_ref[...], (CONV_CHUNK, CONV_WIDTH)).reshape(
                    groups, 8, CONV_WIDTH)
                for k in range(CONV_K):
                    tap = _tap(xr_ref, r0, k - half, CONV_CHUNK).reshape(groups, 8, CONV_WIDTH)
                    acc = acc + tap * dw_ref[k]
                y_ref[pl.ds(pl.multiple_of(s * seq_len + t0, CONV_CHUNK), CONV_CHUNK), :] = (
                    acc.reshape(CONV_CHUNK, CONV_WIDTH))
                return carry
            lax.fori_loop(0, seq_len // CONV_CHUNK, chunk, 0, unroll=2)

    _per_path(pl.program_id(0), body)
    y = _layer_norm(y_ref[...]) * lg_ref[...] + lb_ref[...]
    o_ref[...] = (y * jax.nn.sigmoid(y)).astype(BF16)


def _conv_mixer(proj, dw, db, ln_g, ln_b):
    cb = COL_CV // CONV_WIDTH
    vec = pl.BlockSpec((1, CONV_WIDTH), lambda i: (0, 0))
    return pl.pallas_call(
        _conv_kernel,
        grid=(N_ROW_BLK,),
        in_specs=[
            pl.BlockSpec((ROW_BLK, CONV_WIDTH), lambda i: (i, cb)),
            pl.BlockSpec((ROW_BLK, CONV_WIDTH), lambda i: (i, cb + 1)),
            pl.BlockSpec((CONV_K, 8, CONV_WIDTH), lambda i: (0, 0, 0)),
            vec, vec, vec,
        ],
        out_specs=pl.BlockSpec((ROW_BLK, CONV_WIDTH), lambda i: (i, 0)),
        out_shape=jax.ShapeDtypeStruct((T_ALL, CONV_WIDTH), BF16),
        scratch_shapes=[pltpu.VMEM((8, STENCIL_ROWS, CONV_WIDTH), F32),
                        pltpu.VMEM((ROW_BLK, CONV_WIDTH), F32)],
        compiler_params=_cp(("arbitrary",)),
        name="conv_mixer",
    )(proj, proj, jnp.broadcast_to(dw[:, None, :], (CONV_K, 8, CONV_WIDTH)), db.reshape(1, -1),
      ln_g.reshape(1, -1), ln_b.reshape(1, -1))


S5_ROWS = 512


def _s5_kernel(*refs, tc, nb, reverse, post):
    u_ref, bq_ref, cq_ref, a_ref, h0_ref = refs[:5]
    k = 5
    if post:
        yf_ref, dskip_ref, wglu_ref = refs[k:k + 3]
        k += 3
    y_ref, hf_ref = refs[k:k + 2]
    y_scr, hs_ref, st_ref = refs[k + 2:]
    j = pl.program_id(0)

    @pl.when(j == 0)
    def _():
        st_ref[...] = h0_ref[...]

    rows = tc * nb
    u_tm = pltpu.einshape("btc->tbc", u_ref[...]).reshape(rows, S5_WIDTH)
    u2 = u_tm.astype(BF16)
    for q in range(S5_Q):
        hs_ref[:, q * 2 * S5_QS:(q + 1) * 2 * S5_QS] = jnp.dot(
            u2[:, q * S5_QU:(q + 1) * S5_QU], bq_ref[q], preferred_element_type=F32)

    lane_w = 4096 // nb
    for q in range(S5_Q):
        for w in range(S5_QS // lane_w):
            re0 = q * 2 * S5_QS + w * lane_w
            im0 = re0 + S5_QS
            a_re = jnp.broadcast_to(a_ref[:, re0:re0 + lane_w], (nb, lane_w))
            a_im = jnp.broadcast_to(a_ref[:, im0:im0 + lane_w], (nb, lane_w))

            def step(t, carry, re0=re0, im0=im0, a_re=a_re, a_im=a_im):
                h_re, h_im = carry
                te = tc - 1 - t if reverse else t
                r0 = pl.multiple_of(te * nb, nb)
                n_re = a_re * h_re - a_im * h_im + hs_ref[pl.ds(r0, nb), re0:re0 + lane_w]
                n_im = a_re * h_im + a_im * h_re + hs_ref[pl.ds(r0, nb), im0:im0 + lane_w]
                hs_ref[pl.ds(r0, nb), re0:re0 + lane_w] = n_re
                hs_ref[pl.ds(r0, nb), im0:im0 + lane_w] = n_im
                return n_re, n_im

            h_re, h_im = lax.fori_loop(
                0, tc, step, (st_ref[:, re0:re0 + lane_w], st_ref[:, im0:im0 + lane_w]), unroll=4)
            st_ref[:, re0:re0 + lane_w] = h_re
            st_ref[:, im0:im0 + lane_w] = h_im

    for q in range(S5_Q):
        yq = jnp.dot(hs_ref[:, q * 2 * S5_QS:(q + 1) * 2 * S5_QS].astype(BF16), cq_ref[q],
                     preferred_element_type=F32)
        y_scr[:, :, q * S5_QU:(q + 1) * S5_QU] = yq.reshape(tc, nb, S5_QU)

    if post:
        y_scr[...] = y_scr[...] + (dskip_ref[...] * u_tm).reshape(tc, nb, S5_WIDTH)
    y_bt = pltpu.einshape("tbc->btc", y_scr[...])
    if post:
        y = jax.nn.gelu(y_bt + yf_ref[...]).reshape(rows, S5_WIDTH)
        gate = jnp.dot(y.astype(BF16), wglu_ref[...].astype(BF16), preferred_element_type=F32)
        y_ref[...] = (y * jax.nn.sigmoid(gate)).astype(BF16).reshape(nb, tc, S5_WIDTH)
    else:
        y_ref[...] = y_bt

    @pl.when(j == pl.num_programs(0) - 1)
    def _():
        hf_ref[...] = st_ref[...]


def _s5_pass(proj3, direction, bq, cq, a_vec, h0, yf=None, d_skip=None, w_glu=None):
    post = yf is not None
    reverse = direction == 1
    cb = COL_S5 // S5_WIDTH
    nb, n = BATCH, SEQ
    tc = S5_ROWS // nb
    nc = n // tc

    def chunk(j):
        return nc - 1 - j if reverse else j

    seq_blk = pl.BlockSpec((nb, tc, S5_WIDTH), lambda j: (0, chunk(j), 0))
    in_specs = [
        pl.BlockSpec((nb, tc, S5_WIDTH), lambda j: (0, chunk(j), cb)),
        pl.BlockSpec((None, S5_Q, S5_QU, 2 * S5_QS), lambda j: (direction, 0, 0, 0)),
        pl.BlockSpec((None, S5_Q, 2 * S5_QS, S5_QU), lambda j: (direction, 0, 0, 0)),
        pl.BlockSpec((None, 1, S5_LANES), lambda j: (direction, 0, 0)),
        pl.BlockSpec((None, nb, S5_LANES), lambda j: (direction, 0, 0)),
    ]
    args = [proj3, bq, cq, a_vec, h0]
    if post:
        in_specs += [seq_blk, pl.BlockSpec((1, S5_WIDTH), lambda j: (0, 0)),
                     pl.BlockSpec((S5_WIDTH, S5_WIDTH), lambda j: (0, 0))]
        args += [yf, d_skip.reshape(1, S5_WIDTH), w_glu]
    return pl.pallas_call(
        functools.partial(_s5_kernel, tc=tc, nb=nb, reverse=reverse, post=post),
        grid=(nc,),
        in_specs=in_specs,
        out_specs=[seq_blk, pl.BlockSpec((nb, S5_LANES), lambda j: (0, 0))],
        out_shape=[jax.ShapeDtypeStruct((nb, n, S5_WIDTH), BF16 if post else F32),
                   jax.ShapeDtypeStruct((nb, S5_LANES), F32)],
        scratch_shapes=[
            pltpu.VMEM((tc, nb, S5_WIDTH), F32),
            pltpu.VMEM((tc * nb, S5_LANES), F32),
            pltpu.VMEM((nb, S5_LANES), F32),
        ],
        compiler_params=_cp(("arbitrary",)),
        name="s5_pass",
    )(*args)


def _s5_mixer(proj3, bq, cq, a_vec, h0, d_skip, w_glu):
    yf, hf_f = _s5_pass(proj3, 0, bq, cq, a_vec, h0)
    y, hf_b = _s5_pass(proj3, 1, bq, cq, a_vec, h0, yf, d_skip, w_glu)
    return y.reshape(-1, S5_WIDTH), jnp.stack([hf_f, hf_b])


LAT_TC = S5_ROWS // (2 * DEC_BATCH)
LAT_NC = DEC_SEQ // LAT_TC


def _s5_lat_kernel(*refs):
    nbat, tc = DEC_BATCH, LAT_TC
    uf_refs = refs[:nbat]
    ub_refs = refs[nbat:2 * nbat]
    (bq_ref, cq_ref, a_ref, h0_ref, yf_ref, yb_ref, hf_ref,
     u_scr, y_scr, hs_ref, hb_ref, st_ref) = refs[2 * nbat:]
    j = pl.program_id(0)

    @pl.when(j == 0)
    def _():
        st_ref[...] = h0_ref[...]

    u_scr[:, 0:nbat, :] = pltpu.einshape("btc->tbc", jnp.stack([r[...] for r in uf_refs], axis=0))
    u_scr[:, nbat:, :] = pltpu.einshape("btc->tbc", jnp.stack([r[...] for r in ub_refs], axis=0))
    rows = 2 * nbat * tc
    u2 = u_scr[...].reshape(rows, S5_WIDTH).astype(BF16)
    is_fwd = (lax.broadcasted_iota(jnp.int32, (rows, 1), 0) & (2 * nbat - 1)) < nbat
    zero = jnp.zeros((rows, S5_QU), BF16)
    for q in range(S5_Q):
        uq = u2[:, q * S5_QU:(q + 1) * S5_QU]
        lhs = jnp.concatenate([jnp.where(is_fwd, uq, zero), jnp.where(is_fwd, zero, uq)], axis=1)
        bu = jnp.dot(lhs, bq_ref[q], preferred_element_type=F32)
        hs_ref[:, q * 2 * S5_QS:(q + 1) * 2 * S5_QS] = bu
        hb_ref[:, q * 2 * S5_QS:(q + 1) * 2 * S5_QS] = bu

    lane_w = S5_QS
    low = lax.broadcasted_iota(jnp.int32, (2 * nbat, lane_w), 0) < nbat
    for q in range(S5_Q):
        re0 = q * 2 * S5_QS
        im0 = re0 + S5_QS
        a_re = a_ref[:, re0:re0 + lane_w]
        a_im = a_ref[:, im0:im0 + lane_w]

        def step(t, carry, re0=re0, im0=im0, a_re=a_re, a_im=a_im):
            h_re, h_im = carry
            ra = pl.multiple_of(t * 2 * nbat, 2 * nbat)
            rb = pl.multiple_of((tc - 1 - t) * 2 * nbat, 2 * nbat)
            n_re = (a_re * h_re - a_im * h_im
                    + jnp.where(low, hs_ref[pl.ds(ra, 2 * nbat), re0:re0 + lane_w],
                                hb_ref[pl.ds(rb, 2 * nbat), re0:re0 + lane_w]))
            n_im = (a_re * h_im + a_im * h_re
                    + jnp.where(low, hs_ref[pl.ds(ra, 2 * nbat), im0:im0 + lane_w],
                                hb_ref[pl.ds(rb, 2 * nbat), im0:im0 + lane_w]))
            hs_ref[pl.ds(ra, 2 * nbat), re0:re0 + lane_w] = n_re
            hs_ref[pl.ds(ra, 2 * nbat), im0:im0 + lane_w] = n_im
            hb_ref[pl.ds(rb, 2 * nbat), re0:re0 + lane_w] = n_re
            hb_ref[pl.ds(rb, 2 * nbat), im0:im0 + lane_w] = n_im
            return n_re, n_im

        h_re, h_im = lax.fori_loop(
            0, tc, step, (st_ref[:, re0:re0 + lane_w], st_ref[:, im0:im0 + lane_w]), unroll=2)
        st_ref[:, re0:re0 + lane_w] = h_re
        st_ref[:, im0:im0 + lane_w] = h_im

    for q in range(S5_Q):
        cols = slice(q * 2 * S5_QS, (q + 1) * 2 * S5_QS)
        hq = jnp.where(is_fwd, hs_ref[:, cols], hb_ref[:, cols]).astype(BF16)
        y2 = jnp.dot(hq, cq_ref[q], preferred_element_type=F32)
        yq = jnp.where(is_fwd, y2[:, :S5_QU], y2[:, S5_QU:])
        y_scr[:, :, q * S5_QU:(q + 1) * S5_QU] = yq.reshape(tc, 2 * nbat, S5_QU)
    yf_ref[...] = pltpu.einshape("tbc->btc", y_scr[:, 0:nbat, :])
    yb_ref[...] = pltpu.einshape("tbc->btc", y_scr[:, nbat:, :])

    @pl.when(j == pl.num_programs(0) - 1)
    def _():
        hf_ref[...] = st_ref[...]


def _s5_lat_post_kernel(yf_ref, yb_ref, u_ref, d_ref, w_ref, o_ref):
    y = jax.nn.gelu(yf_ref[...] + yb_ref[...] + d_ref[...] * u_ref[...])
    gate = jnp.dot(y.astype(BF16), w_ref[...].astype(BF16), preferred_element_type=F32)
    o_ref[...] = (y * jax.nn.sigmoid(gate)).astype(BF16)


def _s5_lat_mixer(proj, bq, cq, a_vec, h0, d_skip, w_glu):
    nbat, tc, nc = DEC_BATCH, LAT_TC, LAT_NC
    cb = COL_S5 // S5_WIDTH
    proj3 = proj.reshape(T_ALL // SEQ, SEQ, MIX_COLS)
    per = SEQ // tc

    def u_spec(b, mirrored):
        def index(j):
            c = nc - 1 - j if mirrored else j
            return (T_CTX // SEQ + b * (DEC_SEQ // SEQ) + c // per, c % per, cb)
        return pl.BlockSpec((None, tc, S5_WIDTH), index)

    bq2 = jnp.concatenate([bq[0], bq[1]], axis=1)
    cq2 = jnp.concatenate([cq[0], cq[1]], axis=2)
    a8 = jnp.broadcast_to(a_vec, (2, nbat, S5_LANES)).reshape(2 * nbat, S5_LANES)
    h08 = h0.reshape(2 * nbat, S5_LANES)
    yf, yb, _ = pl.pallas_call(
        _s5_lat_kernel,
        grid=(nc,),
        in_specs=[u_spec(b, False) for b in range(nbat)] + [u_spec(b, True) for b in range(nbat)] + [
            pl.BlockSpec((S5_Q, 2 * S5_QU, 2 * S5_QS), lambda j: (0, 0, 0)),
            pl.BlockSpec((S5_Q, 2 * S5_QS, 2 * S5_QU), lambda j: (0, 0, 0)),
            pl.BlockSpec((2 * nbat, S5_LANES), lambda j: (0, 0)),
            pl.BlockSpec((2 * nbat, S5_LANES), lambda j: (0, 0)),
        ],
        out_specs=[pl.BlockSpec((nbat, tc, S5_WIDTH), lambda j: (0, j, 0)),
                   pl.BlockSpec((nbat, tc, S5_WIDTH), lambda j: (0, nc - 1 - j, 0)),
                   pl.BlockSpec((2 * nbat, S5_LANES), lambda j: (0, 0))],
        out_shape=[jax.ShapeDtypeStruct((nbat, DEC_SEQ, S5_WIDTH), F32),
                   jax.ShapeDtypeStruct((nbat, DEC_SEQ, S5_WIDTH), F32),
                   jax.ShapeDtypeStruct((2 * nbat, S5_LANES), F32)],
        scratch_shapes=[
            pltpu.VMEM((tc, 2 * nbat, S5_WIDTH), F32),
            pltpu.VMEM((tc, 2 * nbat, S5_WIDTH), F32),
            pltpu.VMEM((tc * 2 * nbat, S5_LANES), F32),
            pltpu.VMEM((tc * 2 * nbat, S5_LANES), F32),
            pltpu.VMEM((2 * nbat, S5_LANES), F32),
        ],
        compiler_params=_cp(("arbitrary",)),
        name="s5_lat",
    )(*([proj3] * (2 * nbat)), bq2, cq2, a8, h08)
    tr = ROW_BLK
    row = pl.BlockSpec((tr, S5_WIDTH), lambda i: (i, 0))
    return pl.pallas_call(
        _s5_lat_post_kernel,
        grid=(T_LAT // tr,),
        in_specs=[row, row, pl.BlockSpec((tr, S5_WIDTH), lambda i: (T_CTX // tr + i, cb)),
                  pl.BlockSpec((1, S5_WIDTH), lambda i: (0, 0)),
                  pl.BlockSpec((S5_WIDTH, S5_WIDTH), lambda i: (0, 0))],
        out_specs=row,
        out_shape=jax.ShapeDtypeStruct((T_LAT, S5_WIDTH), BF16),
        compiler_params=_cp(("arbitrary",)),
        name="s5_lat_post",
    )(yf.reshape(T_LAT, S5_WIDTH), yb.reshape(T_LAT, S5_WIDTH), proj, d_skip.reshape(1, S5_WIDTH), w_glu)


def _s5_params(lam_re, lam_im, log_dt, b_re, b_im, c_re, c_im):
    dt = jnp.exp(log_dt)[..., None]
    x = lam_re * dt
    y = lam_im * dt
    ex = jnp.exp(x)
    abar_re = ex * jnp.cos(y)
    abar_im = ex * jnp.sin(y)
    num_re = jnp.expm1(x) * jnp.cos(y) - 2.0 * jnp.square(jnp.sin(0.5 * y))
    num_im = abar_im
    den = lam_re * lam_re + lam_im * lam_im
    coef_re = (num_re * lam_re + num_im * lam_im) / den
    coef_im = (num_im * lam_re - num_re * lam_im) / den
    bbar_re = coef_re[..., None] * b_re - coef_im[..., None] * b_im
    bbar_im = coef_re[..., None] * b_im + coef_im[..., None] * b_re
    eye = jnp.eye(S5_GQ, dtype=F32)
    bb = jnp.stack([bbar_re, bbar_im]).reshape(2, 2, S5_Q, S5_GQ, S5_STATE, S5_CH)
    bq = jnp.einsum("rdqgpc,gh->dqgcrhp", bb, eye).reshape(2, S5_Q, S5_QU, 2 * S5_QS)
    cc = jnp.stack([c_re, -c_im]).reshape(2, 2, S5_Q, S5_GQ, S5_CH, S5_STATE)
    cq = jnp.einsum("rdqgcp,gh->dqrhpgc", cc, eye).reshape(2, S5_Q, 2 * S5_QS, S5_QU)
    ab = jnp.stack([abar_re, abar_im]).reshape(2, 2, S5_Q, S5_GQ, S5_STATE)
    a_vec = ab.transpose(1, 2, 0, 3, 4).reshape(2, 1, S5_LANES)
    return bq.astype(BF16), cq.astype(BF16), a_vec


def _s5_state_to_lanes(st):
    b = st.shape[0]
    s = st.reshape(b, 2, 2, S5_Q, S5_GQ, S5_STATE).transpose(1, 0, 3, 2, 4, 5)
    return s.reshape(2, b, S5_LANES)


def _s5_lanes_to_state(hl):
    b = hl.shape[1]
    s = hl.reshape(2, b, S5_Q, 2, S5_GQ, S5_STATE).transpose(1, 0, 3, 2, 4, 5)
    return s.reshape(b, 2, 2, S5_GROUPS, S5_STATE)


def _rope(x, cos, sin_signed):
    lane = lax.broadcasted_iota(jnp.int32, x.shape, 1)
    quarter = HEAD_DIM // 4
    partner = jnp.where((lane & quarter) == 0,
                        pltpu.roll(x, HEAD_DIM - quarter, axis=1),
                        pltpu.roll(x, quarter, axis=1))
    return x * cos + partner * sin_signed


def _nt_dot(a, b):
    return lax.dot_general(a, b, (((1,), (1,)), ((), ())), preferred_element_type=F32)


def _stack_heads(q_ref, rope=None):
    parts = []
    for r in range(REP):
        q = q_ref[:, r * HEAD_DIM:(r + 1) * HEAD_DIM]
        if rope is not None:
            q = _rope(q, *rope)
        parts.append(q.astype(BF16))
    return jnp.concatenate(parts, axis=0)


def _with_ones(v):
    return jnp.concatenate([v.astype(BF16), jnp.ones(v.shape, BF16)], axis=1)


def _sink_column(sink_ref, h0, rows):
    return jnp.concatenate(
        [jnp.broadcast_to(sink_ref[pl.ds(h0 + r, 1), 0:1], (rows, 1)) for r in range(REP)], axis=0)


def _ctx_attn_kernel(q_ref, k_ref, v_ref, sink_ref, o_ref, ko_ref, vo_ref):
    h0 = pl.program_id(1) * REP
    k = k_ref[...]
    v = v_ref[...]
    ko_ref[...] = k
    vo_ref[...] = v
    q = _stack_heads(q_ref)
    s = _nt_dot(q, k.astype(BF16)) * (HEAD_DIM ** -0.5)
    sink = _sink_column(sink_ref, h0, SEQ)
    m = jnp.maximum(jnp.max(s, axis=-1, keepdims=True), sink)
    p = jnp.exp(s - m)
    acc = jnp.dot(p.astype(BF16), _with_ones(v), preferred_element_type=F32)
    o = acc[:, :HEAD_DIM] / (acc[:, HEAD_DIM:HEAD_DIM + 1] + jnp.exp(sink - m))
    for r in range(REP):
        o_ref[:, r * HEAD_DIM:(r + 1) * HEAD_DIM] = o[r * SEQ:(r + 1) * SEQ].astype(BF16)


def _ctx_attention(proj, sink_tile):
    qw = REP * HEAD_DIM
    kv_out = pl.BlockSpec((SEQ, HEAD_DIM), lambda b, h: (b, h))
    return pl.pallas_call(
        _ctx_attn_kernel,
        grid=(BATCH, N_KV_HEADS),
        in_specs=[
            pl.BlockSpec((SEQ, qw), lambda b, h: (b, COL_Q // qw + h)),
            pl.BlockSpec((SEQ, HEAD_DIM), lambda b, h: (b, COL_K // HEAD_DIM + h)),
            pl.BlockSpec((SEQ, HEAD_DIM), lambda b, h: (b, COL_V // HEAD_DIM + h)),
            pl.BlockSpec((N_HEADS, HEAD_DIM), lambda b, h: (0, 0)),
        ],
        out_specs=[pl.BlockSpec((SEQ, qw), lambda b, h: (b, h)), kv_out, kv_out],
        out_shape=[jax.ShapeDtypeStruct((T_CTX, ATTN_WIDTH), BF16),
                   jax.ShapeDtypeStruct((T_CTX, KV_WIDTH), F32),
                   jax.ShapeDtypeStruct((T_CTX, KV_WIDTH), F32)],
        compiler_params=_cp(("arbitrary", "arbitrary")),
        name="ctx_attention",
    )(proj, proj, proj, sink_tile)


LAT_TQ = 256
LAT_KWIN = LAT_TQ + 2 * WINDOW


def _lat_attn_kernel(q_ref, k_ref, v_ref, kc_ref, vc_ref, cq_ref, sq_ref, ck_ref, sk_ref,
                     sink_ref, o_ref):
    h0 = pl.program_id(1) * REP
    q0 = pl.program_id(2) * LAT_TQ
    k0 = pl.multiple_of(jnp.clip(q0 - WINDOW, 0, DEC_SEQ - LAT_KWIN), WINDOW)
    win = pl.ds(k0, LAT_KWIN)
    kb = _rope(k_ref[win, :], ck_ref[win, :], sk_ref[win, :]).astype(BF16)
    rows = REP * LAT_TQ
    row = lax.broadcasted_iota(jnp.int32, (rows, LAT_KWIN), 0)
    qpos = q0 + (row & (LAT_TQ - 1))
    kpos = k0 + lax.broadcasted_iota(jnp.int32, (rows, LAT_KWIN), 1)
    mask = jnp.abs(qpos - kpos) <= WINDOW
    scale = HEAD_DIM ** -0.5
    q = _stack_heads(q_ref, (cq_ref[...], sq_ref[...]))
    s_lat = jnp.where(mask, _nt_dot(q, kb) * scale, NEG_INF)
    s_ctx = _nt_dot(q, kc_ref[...].astype(BF16)) * scale
    sink = _sink_column(sink_ref, h0, LAT_TQ)
    m = jnp.maximum(jnp.maximum(jnp.max(s_lat, axis=-1, keepdims=True),
                                jnp.max(s_ctx, axis=-1, keepdims=True)), sink)
    p_lat = jnp.exp(s_lat - m).astype(BF16)
    p_ctx = jnp.exp(s_ctx - m).astype(BF16)
    acc = (jnp.dot(p_lat, _with_ones(v_ref[win, :]), preferred_element_type=F32)
           + jnp.dot(p_ctx, _with_ones(vc_ref[...]), preferred_element_type=F32))
    o = acc[:, :HEAD_DIM] / (acc[:, HEAD_DIM:HEAD_DIM + 1] + jnp.exp(sink - m))
    for r in range(REP):
        o_ref[:, r * HEAD_DIM:(r + 1) * HEAD_DIM] = o[r * LAT_TQ:(r + 1) * LAT_TQ].astype(BF16)


def _lat_attention(proj, cache_k_l, cache_v_l, rope_cos, rope_sin, sink_tile):
    tq = LAT_TQ
    nq = DEC_SEQ // tq
    qw = REP * HEAD_DIM
    row0_q = T_CTX // tq
    row0_k = T_CTX // DEC_SEQ
    tab_q = pl.BlockSpec((tq, HEAD_DIM), lambda b, h, i: (i, 0))
    tab_k = pl.BlockSpec((DEC_SEQ, HEAD_DIM), lambda b, h, i: (0, 0))
    return pl.pallas_call(
        _lat_attn_kernel,
        grid=(DEC_BATCH, N_KV_HEADS, nq),
        in_specs=[
            pl.BlockSpec((tq, qw), lambda b, h, i: (row0_q + b * nq + i, COL_Q // qw + h)),
            pl.BlockSpec((DEC_SEQ, HEAD_DIM), lambda b, h, i: (row0_k + b, COL_K // HEAD_DIM + h)),
            pl.BlockSpec((DEC_SEQ, HEAD_DIM), lambda b, h, i: (row0_k + b, COL_V // HEAD_DIM + h)),
            pl.BlockSpec((None, PAST_LEN, HEAD_DIM), lambda b, h, i: (b, 0, h)),
            pl.BlockSpec((None, PAST_LEN, HEAD_DIM), lambda b, h, i: (b, 0, h)),
            tab_q, tab_q, tab_k, tab_k,
            pl.BlockSpec((N_HEADS, HEAD_DIM), lambda b, h, i: (0, 0)),
        ],
        out_specs=pl.BlockSpec((tq, qw), lambda b, h, i: (b * nq + i, h)),
        out_shape=jax.ShapeDtypeStruct((T_LAT, ATTN_WIDTH), BF16),
        compiler_params=_cp(("arbitrary", "arbitrary", "arbitrary")),
        name="lat_attention",
    )(proj, proj, proj, cache_k_l, cache_v_l, rope_cos, rope_sin, rope_cos, rope_sin, sink_tile)


def _rope_tables():
    t = jnp.arange(DEC_SEQ)
    row, col = t // GRID_W, t % GRID_W
    quarter = HEAD_DIM // 4
    inv = ROPE_BASE ** (-jnp.arange(quarter, dtype=F32) / quarter)
    lane = jnp.arange(HEAD_DIM)
    pos = jnp.where(lane[None, :] < HEAD_DIM // 2, row[:, None], col[:, None]).astype(F32)
    ang = pos * inv[lane % quarter][None, :]
    sign = jnp.where((lane & quarter) == 0, -1.0, 1.0).astype(F32)
    return jnp.cos(ang), jnp.sin(ang) * sign[None, :]


def _merge_kernel(yp_ref, ysc_ref, ysl_ref, yac_ref, yal_ref, yc_ref, g0_ref, g1_ref, g2_ref,
                  g3_ref, wp_ref, ws_ref, wa_ref, wc_ref, o_ref):
    is_ctx = pl.program_id(1) < N_CTX_BLK
    y_s5 = jnp.where(is_ctx, ysc_ref[...], ysl_ref[...])
    attn = jnp.where(is_ctx, yac_ref[...], yal_ref[...])
    acc = None
    for y, g_ref, w_ref in ((yp_ref[...], g0_ref, wp_ref), (y_s5, g1_ref, ws_ref),
                            (attn, g2_ref, wa_ref), (yc_ref[...], g3_ref, wc_ref)):
        t = g_ref[...].astype(F32) * jnp.dot(y, w_ref[...].astype(BF16), preferred_element_type=F32)
        acc = t if acc is None else acc + t
    o_ref[...] = acc.astype(BF16)


def _merge(gates, y_pool, y_s5_ctx, y_s5_lat, attn_ctx, attn_lat, y_conv, layer,
           w_pool, w_s5, w_attn, w_conv):
    tm, tn = ROW_BLK, 512
    per = D_MODEL // tn

    def act(width):
        return pl.BlockSpec((tm, width), lambda j, i: (i, 0))

    def ctx_part(width):
        return pl.BlockSpec((tm, width), lambda j, i: (jnp.minimum(i, N_CTX_BLK - 1), 0))

    def lat_part(width):
        return pl.BlockSpec((tm, width), lambda j, i: (jnp.maximum(i - N_CTX_BLK, 0), 0))

    def gate(branch):
        return pl.BlockSpec((tm, tn), lambda j, i: (i, branch * per + j))

    def wt(width):
        return pl.BlockSpec((None, width, tn), lambda j, i: (layer, 0, j))

    return pl.pallas_call(
        _merge_kernel,
        grid=(D_MODEL // tn, T_ALL // tm),
        in_specs=[act(POOL_WIDTH), ctx_part(S5_WIDTH), lat_part(S5_WIDTH),
                  ctx_part(ATTN_WIDTH), lat_part(ATTN_WIDTH), act(CONV_WIDTH),
                  gate(0), gate(1), gate(2), gate(3),
                  wt(POOL_WIDTH), wt(S5_WIDTH), wt(ATTN_WIDTH), wt(CONV_WIDTH)],
        out_specs=pl.BlockSpec((tm, tn), lambda j, i: (i, j)),
        out_shape=jax.ShapeDtypeStruct((T_ALL, D_MODEL), BF16),
        compiler_params=_cp(("arbitrary", "arbitrary")),
        name="merge",
    )(y_pool, y_s5_ctx, y_s5_lat, attn_ctx, attn_lat, y_conv, gates, gates, gates, gates,
      w_pool, w_s5, w_attn, w_conv)


FFN_TILE = 512
N_FFN_TILE = D_FF // FFN_TILE


def _ffn_kernel(h_ref, wg_ref, wv_ref, dwg_ref, dwv_ref, dbg_ref, dbv_ref, wd_ref, o_ref,
                ug_ref, uv_ref, acc_ref):
    i = pl.program_id(0)
    f = pl.program_id(1)

    def up(slot):
        h = h_ref[...]
        ug_ref[slot] = jnp.dot(h, wg_ref[...].astype(BF16), preferred_element_type=F32)
        uv_ref[slot] = jnp.dot(h, wv_ref[...].astype(BF16), preferred_element_type=F32)

    def down(slot):
        pos, seq_len = _seq_pos(i, ROW_BLK)
        first = pos == 0
        last = pos == seq_len - 1

        def conv3(u, dw_ref, db_ref):
            prev = jnp.where(first, 0.0, _shift_rows(u, -1))
            nxt = jnp.where(last, 0.0, _shift_rows(u, 1))
            return prev * dw_ref[0:1, :] + u * dw_ref[1:2, :] + nxt * dw_ref[2:3, :] + db_ref[...]

        gt = conv3(ug_ref[slot], dwg_ref, dbg_ref)
        val = conv3(uv_ref[slot], dwv_ref, dbv_ref)
        act = (gt * jax.nn.sigmoid(gt) * val).astype(BF16)
        acc_ref[...] += jnp.dot(act, wd_ref[...].astype(BF16), preferred_element_type=F32)

    @pl.when(f == 0)
    def _():
        acc_ref[...] = jnp.zeros_like(acc_ref)
        up(0)

    for par in (0, 1):
        @pl.when(jnp.logical_and(jnp.logical_and(f > 0, f < N_FFN_TILE), f % 2 == par))
        def _(par=par):
            up(par)
            down(1 - par)

    @pl.when(f == N_FFN_TILE)
    def _():
        down((N_FFN_TILE - 1) % 2)
        o_ref[...] = acc_ref[...].astype(o_ref.dtype)


def _conv_ffn(h, layer, w_up, dw, db, w_down):
    tf, nf = FFN_TILE, N_FFN_TILE
    db3 = db.reshape(DEPTH, 1, 2 * D_FF)

    def cur(f):
        return jnp.minimum(f, nf - 1)

    def prv(f):
        return jnp.maximum(f - 1, 0)

    return pl.pallas_call(
        _ffn_kernel,
        grid=(N_ROW_BLK, nf + 1),
        in_specs=[
            pl.BlockSpec((ROW_BLK, D_MODEL), lambda i, f: (i, 0), pipeline_mode=pl.Buffered(1)),
            pl.BlockSpec((None, D_MODEL, tf), lambda i, f: (layer, 0, cur(f))),
            pl.BlockSpec((None, D_MODEL, tf), lambda i, f: (layer, 0, nf + cur(f))),
            pl.BlockSpec((None, FFN_CONV_K, tf), lambda i, f: (layer, 0, prv(f))),
            pl.BlockSpec((None, FFN_CONV_K, tf), lambda i, f: (layer, 0, nf + prv(f))),
            pl.BlockSpec((None, 1, tf), lambda i, f: (layer, 0, prv(f))),
            pl.BlockSpec((None, 1, tf), lambda i, f: (layer, 0, nf + prv(f))),
            pl.BlockSpec((None, tf, D_MODEL), lambda i, f: (layer, prv(f), 0)),
        ],
        out_specs=pl.BlockSpec((ROW_BLK, D_MODEL), lambda i, f: (i, 0), pipeline_mode=pl.Buffered(1)),
        out_shape=jax.ShapeDtypeStruct((T_ALL, D_MODEL), BF16),
        scratch_shapes=[pltpu.VMEM((2, ROW_BLK, tf), F32), pltpu.VMEM((2, ROW_BLK, tf), F32),
                        pltpu.VMEM((ROW_BLK, D_MODEL), F32)],
        compiler_params=_cp(("arbitrary", "arbitrary")),
        name="conv_ffn",
    )(h, w_up, w_up, dw, dw, db3, db3, w_down)


def kernel(x_prompt, x_sample, cache_k, cache_v, state_s5, c, c_ctx, w_ada, b_ada, w_in, pool_w, pool_scale, s5_lambda_re, s5_lambda_im, s5_log_dt, s5_b_re, s5_b_im, s5_c_re, s5_c_im, s5_d, s5_w_glu, attn_sink, conv_dw, conv_db, conv_ln_g, conv_ln_b, w_br_pool, w_br_s5, w_br_attn, w_br_conv, w_out, ln1_g, ln1_b, ffn_w_up, ffn_dw, ffn_db, ffn_w_down, ln2_g, ln2_b):
    x_parts = (x_prompt.reshape(T_CTX, D_MODEL), x_sample.reshape(T_LAT, D_MODEL))
    cvec = jnp.concatenate(
        [c_ctx[None, :], c, jnp.zeros((MOD_ROWS - 1 - DEC_BATCH, D_MODEL), F32)], axis=0)
    mod = _mod_table(cvec, w_ada, b_ada).reshape(DEPTH, MOD_ROWS, 6, D_MODEL)
    rope_cos, rope_sin = _rope_tables()

    ks_out, vs_out, ss_out = [], [], []
    h = _ln_mod(x_parts[0], x_parts[1], mod[0], 0, 1)
    for l in range(DEPTH):
        proj = _matmul(h, w_in, l, 0, MIX_COLS)
        gates = _matmul(h, w_in, l, COL_GATE, N_BRANCH * D_MODEL, out_dtype=BF16, gate=True)

        y_pool = _pool_mixer(proj, pool_w[l], pool_scale[l])
        y_conv = _conv_mixer(proj, conv_dw[l], conv_db[l], conv_ln_g[l], conv_ln_b[l])

        bq, cq, a_vec = _s5_params(s5_lambda_re[l], s5_lambda_im[l], s5_log_dt[l], s5_b_re[l],
                                   s5_b_im[l], s5_c_re[l], s5_c_im[l])
        proj3 = proj.reshape(T_ALL // SEQ, SEQ, MIX_COLS)
        h0_ctx = jnp.zeros((2, BATCH, S5_LANES), F32)
        y_s5_ctx, hf_ctx = _s5_mixer(proj3, bq, cq, a_vec, h0_ctx, s5_d[l], s5_w_glu[l])
        y_s5_lat = _s5_lat_mixer(proj, bq, cq, a_vec, _s5_state_to_lanes(state_s5[:, l]),
                                 s5_d[l], s5_w_glu[l])

        sink_tile = jnp.broadcast_to(attn_sink[l][:, None], (N_HEADS, HEAD_DIM))
        attn_ctx, k_new, v_new = _ctx_attention(proj, sink_tile)
        attn_lat = _lat_attention(proj, cache_k[:, l].reshape(DEC_BATCH, PAST_LEN, KV_WIDTH),
                                  cache_v[:, l].reshape(DEC_BATCH, PAST_LEN, KV_WIDTH),
                                  rope_cos, rope_sin, sink_tile)

        merged = _merge(gates, y_pool, y_s5_ctx, y_s5_lat, attn_ctx, attn_lat, y_conv, l,
                        w_br_pool, w_br_s5, w_br_attn, w_br_conv)
        mixed = _matmul(merged, w_out, l, out_dtype=BF16)
        x, h2 = _res_ln(x_parts, mixed, mod[l], 2, ln1_g[l], ln1_b[l], mod[l], 3, 4)
        f = _conv_ffn(h2, l, ffn_w_up, ffn_dw, ffn_db, ffn_w_down)
        if l + 1 < DEPTH:
            x, h = _res_ln((x,), f, mod[l], 5, ln2_g[l], ln2_b[l], mod[l + 1], 0, 1)
            x_parts = (x,)
        else:
            y_prompt, y_sample = _res_ln((x,), f, mod[l], 5, ln2_g[l], ln2_b[l])

        ks_out.append(k_new.reshape(BATCH, SEQ, N_KV_HEADS, HEAD_DIM))
        vs_out.append(v_new.reshape(BATCH, SEQ, N_KV_HEADS, HEAD_DIM))
        ss_out.append(_s5_lanes_to_state(hf_ctx))

    return (y_prompt.reshape(BATCH, SEQ, D_MODEL), y_sample.reshape(DEC_BATCH, DEC_SEQ, D_MODEL),
            jnp.stack(ks_out, axis=1), jnp.stack(vs_out, axis=1), jnp.stack(ss_out, axis=1))
```
